```python
import math
import jax, jax.numpy as jnp
from jax import lax
import numpy as np

D_MODEL = 1024
BATCH = 32
SEQ = 2048
DEPTH = 1
DEC_BATCH = 32
DEC_SEQ = 64
PAST_LEN = 2048

CHUNK = 64
HEAD_DIM = 64
A_Q_HEADS = 8
A_KV_HEADS = 2
A_GROUP = A_Q_HEADS // A_KV_HEADS
A_WINDOW = 128
A_BACK_CHUNKS = A_WINDOW // CHUNK
B_HEADS = 8
B_BACK_CHUNKS = 8
B_REACH = B_BACK_CHUNKS * CHUNK
REL_CLIP = 256
T5_BUCKETS = 32
T5_MAX_DIST = 128
N_GROUPS = 4
EXPERTS_PER_GROUP = 8
N_EXPERTS = N_GROUPS * EXPERTS_PER_GROUP
TOP_K_IN_GROUP = 2
D_EXPERT = D_MODEL // 4
EPS = 1e-6

QA_W = A_Q_HEADS * HEAD_DIM
KVA_W = A_KV_HEADS * HEAD_DIM
B_W = B_HEADS * HEAD_DIM
IN_W = QA_W + 2 * KVA_W + 3 * B_W
SPLITS = [QA_W, QA_W + KVA_W, QA_W + 2 * KVA_W, QA_W + 2 * KVA_W + B_W, QA_W + 2 * KVA_W + 2 * B_W]

kernel_name = 'hybrid_streaming_encoder_step'


def rms_norm(x, g):
    xf = x.astype(jnp.float32)
    y = xf * lax.rsqrt(jnp.mean(xf * xf, axis=-1, keepdims=True) + EPS)
    return (y * g.astype(jnp.float32)).astype(x.dtype)


def modulate(h, shift, scale):
    return h * (1 + scale[:, None, :]) + shift[:, None, :]


def ada_params(c, w_ada, b_ada):
    return jnp.split(jax.nn.silu(c) @ w_ada + b_ada, 6, axis=-1)


def t5_bucket(rel):
    half = T5_BUCKETS // 2
    exact = half // 2
    ret = jnp.where(rel > 0, half, 0)
    n = jnp.abs(rel)
    nf = jnp.maximum(n, 1).astype(jnp.float32)
    large = exact + (jnp.log(nf / exact) / math.log(T5_MAX_DIST / exact) * (half - exact)).astype(jnp.int32)
    large = jnp.minimum(large, half - 1)
    return ret + jnp.where(n < exact, n, large)


def t5_window_bias(t5_table, t, hist):
    rel = jnp.arange(hist + t)[None, :] - hist - jnp.arange(t)[:, None]
    bias = t5_table[t5_bucket(rel)]
    return jnp.transpose(bias, (2, 0, 1)).astype(jnp.float32).reshape(A_KV_HEADS, A_GROUP, t, hist + t)


def chunk_rel_bias(rel_table, t, hist):
    d = jnp.arange(t)[:, None] + hist - jnp.arange(hist + t)[None, :]
    idx = jnp.clip(d, -REL_CLIP, REL_CLIP) + REL_CLIP
    return rel_table[:, idx].astype(jnp.float32)[:, None]


def band_attend(q, k, v, bias, valid, sink):
    s = jnp.einsum('btkgd,bmkd->bkgtm', q, k).astype(jnp.float32) * (HEAD_DIM ** -0.5) + bias
    if valid is not None:
        s = jnp.where(valid, s, -1e30)
    if sink is None:
        p = jax.nn.softmax(s, axis=-1)
    else:
        sk = sink.astype(jnp.float32)[None, :, :, None, None]
        mx = jnp.maximum(jnp.max(s, axis=-1, keepdims=True), sk)
        e = jnp.exp(s - mx)
        p = e / (jnp.sum(e, axis=-1, keepdims=True) + jnp.exp(sk - mx))
    return jnp.einsum('bkgtm,bmkd->btkgd', p.astype(v.dtype), v)


def chunked_band_prompt(q, k, v, n_back, bias, sink):
    b, s, hkv, g, dh = q.shape
    nc = s // CHUNK
    band = (n_back + 1) * CHUNK
    pad = ((0, 0), (n_back * CHUNK, 0), (0, 0), (0, 0))
    kp = jnp.pad(k, pad)
    vp = jnp.pad(v, pad)
    qc = jnp.moveaxis(q.reshape(b, nc, CHUNK, hkv, g, dh), 1, 0)
    m_local = jnp.arange(band)

    def one_chunk(args):
        j, qj = args
        kj = lax.dynamic_slice_in_dim(kp, j * CHUNK, band, axis=1)
        vj = lax.dynamic_slice_in_dim(vp, j * CHUNK, band, axis=1)
        valid = m_local >= (n_back - j) * CHUNK
        return band_attend(qj, kj, vj, bias, valid, sink)

    out = lax.map(one_chunk, (jnp.arange(nc), qc))
    return jnp.moveaxis(out, 0, 1).reshape(b, s, hkv * g * dh)


def split_mixer_inputs(p):
    b, t, _ = p.shape
    qa, ka, va, qb, kb, vb = jnp.split(p, SPLITS, axis=-1)
    return (qa.reshape(b, t, A_KV_HEADS, A_GROUP, HEAD_DIM),
            ka.reshape(b, t, A_KV_HEADS, HEAD_DIM),
            va.reshape(b, t, A_KV_HEADS, HEAD_DIM),
            qb.reshape(b, t, B_HEADS, 1, HEAD_DIM),
            kb.reshape(b, t, B_HEADS, HEAD_DIM),
            vb.reshape(b, t, B_HEADS, HEAD_DIM))


def mixer_sublayer(x, shift, scale, gate, g_pre, g_post, w_in, w_proj_a, w_proj_b, w_gate, b_gate, w_o, attend):
    h = modulate(rms_norm(x, g_pre), shift, scale)
    oa, ob, new_state = attend(*split_mixer_inputs(h @ w_in))
    ga, gb = jnp.split(jax.nn.sigmoid(h @ w_gate + b_gate), 2, axis=-1)
    mixed = (ga * (oa @ w_proj_a) + gb * (ob @ w_proj_b)) @ w_o
    return x + gate[:, None, :] * rms_norm(mixed, g_post), new_state


def hier_moe(h, w_route_g, b_route_g, w_route_e, b_route_e, w_e_gate, w_e_up, w_e_down):
    b, s, d = h.shape
    t = h.reshape(-1, d)
    n = t.shape[0]
    g_logits = (t @ w_route_g).astype(jnp.float32) + b_route_g.astype(jnp.float32)
    g_prob = jax.nn.softmax(g_logits, axis=-1)
    g_idx = jnp.argmax(g_logits, axis=-1)
    g_w = jnp.take_along_axis(g_prob, g_idx[:, None], axis=-1)
    e_logits = ((t @ w_route_e).astype(jnp.float32) + b_route_e.astype(jnp.float32)).reshape(n, N_GROUPS, EXPERTS_PER_GROUP)
    e_sel = jnp.take_along_axis(e_logits, g_idx[:, None, None], axis=1)[:, 0]
    top_v, top_i = lax.top_k(e_sel, TOP_K_IN_GROUP)
    w = g_w * jax.nn.softmax(top_v, axis=-1)
    expert_id = g_idx[:, None] * EXPERTS_PER_GROUP + top_i
    combine = jnp.einsum('nk,nke->ne', w, jax.nn.one_hot(expert_id, N_EXPERTS, dtype=jnp.float32)).astype(t.dtype)
    y = jnp.zeros_like(t)
    for e in range(N_EXPERTS):
        he = jax.nn.silu(t @ w_e_gate[e]) * (t @ w_e_up[e])
        y = y + combine[:, e:e + 1] * (he @ w_e_down[e])
    return y.reshape(b, s, d)


def moe_sublayer(x, shift, scale, gate, g_pre, g_post, w_route_g, b_route_g, w_route_e, b_route_e, w_e_gate, w_e_up, w_e_down):
    h = modulate(rms_norm(x, g_pre), shift, scale)
    y = hier_moe(h, w_route_g, b_route_g, w_route_e, b_route_e, w_e_gate, w_e_up, w_e_down)
    return x + gate[:, None, :] * rms_norm(y, g_post)


def setup_inputs(seed: int = 0) -> dict:
    key = jax.random.key(seed)
    ks = iter(jax.random.split(key, 40))

    def nrm(shape, scale):
        return scale * jax.random.normal(next(ks), shape, jnp.float32)

    la = min(A_WINDOW, PAST_LEN)
    lb = min(B_REACH, PAST_LEN)
    return {
        'x_prompt': nrm((BATCH, SEQ, D_MODEL), 1.0),
        'x_sample': nrm((DEC_BATCH, DEC_SEQ, D_MODEL), 1.0),
        'c_prompt': nrm((BATCH, D_MODEL), 1.0),
        'c_sample': nrm((DEC_BATCH, D_MODEL), 1.0),
        'cache_a_k': nrm((DEPTH, DEC_BATCH, la, A_KV_HEADS, HEAD_DIM), 1.0),
        'cache_a_v': nrm((DEPTH, DEC_BATCH, la, A_KV_HEADS, HEAD_DIM), 1.0),
        'cache_b_k': nrm((DEPTH, DEC_BATCH, lb, B_HEADS, HEAD_DIM), 1.0),
        'cache_b_v': nrm((DEPTH, DEC_BATCH, lb, B_HEADS, HEAD_DIM), 1.0),
        'w_ada': nrm((DEPTH, D_MODEL, 6 * D_MODEL), 0.3 * D_MODEL ** -0.5),
        'b_ada': nrm((DEPTH, 6 * D_MODEL), 0.02),
        'g_pre_mix': 1.0 + nrm((DEPTH, D_MODEL), 0.05),
        'g_post_mix': 1.0 + nrm((DEPTH, D_MODEL), 0.05),
        'g_pre_ffn': 1.0 + nrm((DEPTH, D_MODEL), 0.05),
        'g_post_ffn': 1.0 + nrm((DEPTH, D_MODEL), 0.05),
        'w_in': nrm((DEPTH, D_MODEL, IN_W), D_MODEL ** -0.5),
        'a_sinks': nrm((DEPTH, A_Q_HEADS), 0.5),
        't5_table': nrm((T5_BUCKETS, A_Q_HEADS), 0.5),
        'b_rel_table': nrm((DEPTH, B_HEADS, 2 * REL_CLIP + 1), 0.5),
        'w_proj_a': nrm((DEPTH, QA_W, D_MODEL), QA_W ** -0.5),
        'w_proj_b': nrm((DEPTH, B_W, D_MODEL), B_W ** -0.5),
        'w_gate': nrm((DEPTH, D_MODEL, 2 * D_MODEL), D_MODEL ** -0.5),
        'b_gate': nrm((DEPTH, 2 * D_MODEL), 0.02),
        'w_o': nrm((DEPTH, D_MODEL, D_MODEL), D_MODEL ** -0.5),
        'w_route_g': nrm((DEPTH, D_MODEL, N_GROUPS), D_MODEL ** -0.5),
        'b_route_g': nrm((DEPTH, N_GROUPS), 0.01),
        'w_route_e': nrm((DEPTH, D_MODEL, N_EXPERTS), D_MODEL ** -0.5),
        'b_route_e': nrm((DEPTH, N_EXPERTS), 0.01),
        'w_e_gate': nrm((DEPTH, N_EXPERTS, D_MODEL, D_EXPERT), D_MODEL ** -0.5),
        'w_e_up': nrm((DEPTH, N_EXPERTS, D_MODEL, D_EXPERT), D_MODEL ** -0.5),
        'w_e_down': nrm((DEPTH, N_EXPERTS, D_EXPERT, D_MODEL), D_EXPERT ** -0.5),
    }


def reference(x_prompt, x_sample, c_prompt, c_sample, cache_a_k, cache_a_v, cache_b_k, cache_b_v,
              w_ada, b_ada, g_pre_mix, g_post_mix, g_pre_ffn, g_post_ffn, w_in, a_sinks, t5_table,
              b_rel_table, w_proj_a, w_proj_b, w_gate, b_gate, w_o, w_route_g, b_route_g,
              w_route_e, b_route_e, w_e_gate, w_e_up, w_e_down):
    xp, xs = x_prompt, x_sample
    nak_p, nav_p, nbk_p, nbv_p = [], [], [], []
    nak_s, nav_s, nbk_s, nbv_s = [], [], [], []
    for l in range(DEPTH):
        sinks = a_sinks[l].reshape(A_KV_HEADS, A_GROUP)
        bias_a_p = t5_window_bias(t5_table, CHUNK, A_WINDOW)
        bias_b_p = chunk_rel_bias(b_rel_table[l], CHUNK, B_REACH)
        cak, cav, cbk, cbv = cache_a_k[l], cache_a_v[l], cache_b_k[l], cache_b_v[l]
        la, lb = cak.shape[1], cbk.shape[1]

        def attend_prompt(qa, ka, va, qb, kb, vb):
            oa = chunked_band_prompt(qa, ka, va, A_BACK_CHUNKS, bias_a_p, sinks)
            ob = chunked_band_prompt(qb, kb, vb, B_BACK_CHUNKS, bias_b_p, None)
            return oa, ob, (ka[:, -A_WINDOW:], va[:, -A_WINDOW:], kb[:, -B_REACH:], vb[:, -B_REACH:])

        def attend_sample(qa, ka, va, qb, kb, vb):
            b, t = qa.shape[0], qa.shape[1]
            ka_band = jnp.concatenate([cak, ka], axis=1)
            va_band = jnp.concatenate([cav, va], axis=1)
            kb_band = jnp.concatenate([cbk, kb], axis=1)
            vb_band = jnp.concatenate([cbv, vb], axis=1)
            oa = band_attend(qa, ka_band, va_band, t5_window_bias(t5_table, t, la), None, sinks).reshape(b, t, QA_W)
            ob = band_attend(qb, kb_band, vb_band, chunk_rel_bias(b_rel_table[l], t, lb), None, None).reshape(b, t, B_W)
            return oa, ob, (ka_band[:, -la:], va_band[:, -la:], kb_band[:, -lb:], vb_band[:, -lb:])

        p1s, p1c, p1g, p2s, p2c, p2g = ada_params(c_prompt, w_ada[l], b_ada[l])
        s1s, s1c, s1g, s2s, s2c, s2g = ada_params(c_sample, w_ada[l], b_ada[l])

        xp, st_p = mixer_sublayer(xp, p1s, p1c, p1g, g_pre_mix[l], g_post_mix[l], w_in[l], w_proj_a[l],
                                  w_proj_b[l], w_gate[l], b_gate[l], w_o[l], attend_prompt)
        xs, st_s = mixer_sublayer(xs, s1s, s1c, s1g, g_pre_mix[l], g_post_mix[l], w_in[l], w_proj_a[l],
                                  w_proj_b[l], w_gate[l], b_gate[l], w_o[l], attend_sample)
        xp = moe_sublayer(xp, p2s, p2c, p2g, g_pre_ffn[l], g_post_ffn[l], w_route_g[l], b_route_g[l],
                          w_route_e[l], b_route_e[l], w_e_gate[l], w_e_up[l], w_e_down[l])
        xs = moe_sublayer(xs, s2s, s2c, s2g, g_pre_ffn[l], g_post_ffn[l], w_route_g[l], b_route_g[l],
                          w_route_e[l], b_route_e[l], w_e_gate[l], w_e_up[l], w_e_down[l])
        nak_p.append(st_p[0]); nav_p.append(st_p[1]); nbk_p.append(st_p[2]); nbv_p.append(st_p[3])
        nak_s.append(st_s[0]); nav_s.append(st_s[1]); nbk_s.append(st_s[2]); nbv_s.append(st_s[3])

    return (xp, xs,
            jnp.stack(nak_p), jnp.stack(nav_p), jnp.stack(nbk_p), jnp.stack(nbv_p),
            jnp.stack(nak_s), jnp.stack(nav_s), jnp.stack(nbk_s), jnp.stack(nbv_s))
```

```python
import functools
import math

import numpy as np
import jax
import jax.numpy as jnp
from jax import lax
from jax.experimental import pallas as pl
from jax.experimental.pallas import tpu as pltpu

D_MODEL = 1024
CHUNK = 64
HEAD_DIM = 64
A_Q_HEADS = 8
A_KV_HEADS = 2
A_GROUP = A_Q_HEADS // A_KV_HEADS
A_WINDOW = 128
A_BACK = A_WINDOW // CHUNK
B_HEADS = 8
B_BACK = 8
B_REACH = B_BACK * CHUNK
REL_CLIP = 256
T5_BUCKETS = 32
T5_MAX_DIST = 128
N_GROUPS = 4
EXPERTS_PER_GROUP = 8
N_EXPERTS = N_GROUPS * EXPERTS_PER_GROUP
D_EXPERT = D_MODEL // 4
EPS = 1e-6

QA_W = A_Q_HEADS * HEAD_DIM
KVA_W = A_KV_HEADS * HEAD_DIM
B_W = B_HEADS * HEAD_DIM
IN_W = QA_W + 2 * KVA_W + 3 * B_W
A_BAND = A_WINDOW + CHUNK
B_BAND = B_REACH + CHUNK

PAIRS_PER_GROUP = EXPERTS_PER_GROUP * (EXPERTS_PER_GROUP - 1) // 2
N_CLASSES = N_GROUPS * PAIRS_PER_GROUP
ROUTE_ROWS = 40

F32 = jnp.float32
BF16 = jnp.bfloat16

VMEM_LIMIT_BYTES = 56 * 1024 * 1024

SEQ_TILE = 512
GATHER_TILE = 512


def _cparams(*sem):
    return pltpu.CompilerParams(dimension_semantics=sem, vmem_limit_bytes=VMEM_LIMIT_BYTES)


def _norm_mod(x, g, scale, shift):
    y = x * lax.rsqrt(jnp.mean(x * x, axis=-1, keepdims=True) + EPS)
    return (y * g) * (1.0 + scale) + shift


def _rms(x, g):
    return (x * lax.rsqrt(jnp.mean(x * x, axis=-1, keepdims=True) + EPS)) * g


def _ada_kernel(c_ref, w_ref, b_ref, o_ref):
    c = c_ref[...]
    s = (c * jax.nn.sigmoid(c)).astype(BF16)
    o_ref[...] = jnp.dot(s, w_ref[...].astype(BF16), preferred_element_type=F32) + b_ref[...]


def _ada(c, w_ada, b_ada):
    n, d = c.shape
    wn = w_ada.shape[1]
    tn = 512
    return pl.pallas_call(
        _ada_kernel,
        out_shape=jax.ShapeDtypeStruct((n, wn), F32),
        grid=(wn // tn,),
        in_specs=[
            pl.BlockSpec((n, d), lambda j: (0, 0)),
            pl.BlockSpec((d, tn), lambda j: (0, j)),
            pl.BlockSpec((1, tn), lambda j: (0, j)),
        ],
        out_specs=pl.BlockSpec((n, tn), lambda j: (0, j)),
        compiler_params=_cparams("arbitrary"),
        name="ada",
    )(c, w_ada, b_ada.reshape(1, wn))


_COL_QA = (0, QA_W)
_COL_KA = (QA_W, QA_W + KVA_W)
_COL_VA = (QA_W + KVA_W, QA_W + 2 * KVA_W)
_COL_QB = (QA_W + 2 * KVA_W, QA_W + 2 * KVA_W + B_W)
_COL_KB = (_COL_QB[1], _COL_QB[1] + B_W)
_COL_VB = (_COL_KB[1], _COL_KB[1] + B_W)
Q_SCALE = HEAD_DIM ** -0.5


def _project(x_ref, mod_ref, g_ref, w_ref):
    h = _norm_mod(x_ref[0], g_ref[...], mod_ref[0, 1:2, :], mod_ref[0, 0:1, :])
    return jnp.dot(h.astype(BF16), w_ref[...], preferred_element_type=F32)


def _cols(p, c):
    return p[:, c[0]:c[1]]


def _pre_prompt_kernel(x_ref, mod_ref, g_ref, w_ref,
                       qa_ref, ka_ref, va_ref, qb_ref, kb_ref, vb_ref,
                       sak_ref, sav_ref, sbk_ref, sbv_ref, *, n_tiles, tile):
    p = _project(x_ref, mod_ref, g_ref, w_ref)
    qa_ref[0] = (_cols(p, _COL_QA) * Q_SCALE).astype(BF16)
    ka_ref[0] = _cols(p, _COL_KA).astype(BF16)
    va_ref[0] = _cols(p, _COL_VA).astype(BF16)
    qb_ref[0] = (_cols(p, _COL_QB) * Q_SCALE).astype(BF16)
    kb_ref[0] = _cols(p, _COL_KB).astype(BF16)
    vb_ref[0] = _cols(p, _COL_VB).astype(BF16)

    @pl.when(pl.program_id(1) == n_tiles - 1)
    def _():
        sak_ref[0] = _cols(p, _COL_KA)[tile - A_WINDOW:, :]
        sav_ref[0] = _cols(p, _COL_VA)[tile - A_WINDOW:, :]
        sbk_ref[0] = _cols(p, _COL_KB)[tile - B_REACH:, :]
        sbv_ref[0] = _cols(p, _COL_VB)[tile - B_REACH:, :]


def _pre_prompt(x, mods, g_pre, w_in_bf):
    b, s, d = x.shape
    tile = SEQ_TILE
    assert s % tile == 0 and tile >= B_REACH and s >= B_REACH
    nt = s // tile
    tok = lambda w: pl.BlockSpec((1, tile, w), lambda i, j: (i, j, 0))
    state = lambda r, w: pl.BlockSpec((1, r, w), lambda i, j: (i, 0, 0))
    return pl.pallas_call(
        functools.partial(_pre_prompt_kernel, n_tiles=nt, tile=tile),
        out_shape=(
            jax.ShapeDtypeStruct((b, s, QA_W), BF16),
            jax.ShapeDtypeStruct((b, s, KVA_W), BF16),
            jax.ShapeDtypeStruct((b, s, KVA_W), BF16),
            jax.ShapeDtypeStruct((b, s, B_W), BF16),
            jax.ShapeDtypeStruct((b, s, B_W), BF16),
            jax.ShapeDtypeStruct((b, s, B_W), BF16),
            jax.ShapeDtypeStruct((b, A_WINDOW, KVA_W), F32),
            jax.ShapeDtypeStruct((b, A_WINDOW, KVA_W), F32),
            jax.ShapeDtypeStruct((b, B_REACH, B_W), F32),
            jax.ShapeDtypeStruct((b, B_REACH, B_W), F32),
        ),
        grid=(b, nt),
        in_specs=[
            tok(d),
            pl.BlockSpec((1, 6, d), lambda i, j: (i, 0, 0)),
            pl.BlockSpec((1, d), lambda i, j: (0, 0)),
            pl.BlockSpec((d, IN_W), lambda i, j: (0, 0)),
        ],
        out_specs=(
            tok(QA_W), tok(KVA_W), tok(KVA_W), tok(B_W), tok(B_W), tok(B_W),
            state(A_WINDOW, KVA_W), state(A_WINDOW, KVA_W), state(B_REACH, B_W), state(B_REACH, B_W),
        ),
        compiler_params=_cparams("parallel", "arbitrary"),
        name="pre_prompt",
    )(x, mods, g_pre, w_in_bf)


def _pre_sample_kernel(x_ref, mod_ref, g_ref, w_ref, cak_ref, cav_ref, cbk_ref, cbv_ref,
                       qa_ref, ka_ref, va_ref, qb_ref, kb_ref, vb_ref,
                       sak_ref, sav_ref, sbk_ref, sbv_ref, *, t, la, lb):
    p = _project(x_ref, mod_ref, g_ref, w_ref)
    qa_ref[0] = (_cols(p, _COL_QA) * Q_SCALE).astype(BF16)
    qb_ref[0] = (_cols(p, _COL_QB) * Q_SCALE).astype(BF16)
    for cache_ref, band_ref, state_ref, col, hist in (
            (cak_ref, ka_ref, sak_ref, _COL_KA, la), (cav_ref, va_ref, sav_ref, _COL_VA, la),
            (cbk_ref, kb_ref, sbk_ref, _COL_KB, lb), (cbv_ref, vb_ref, sbv_ref, _COL_VB, lb)):
        new = _cols(p, col)
        cache = cache_ref[0]
        band_ref[0, 0:hist, :] = cache.astype(BF16)
        band_ref[0, hist:hist + t, :] = new.astype(BF16)
        state_ref[0, 0:hist - t, :] = cache[t:, :]
        state_ref[0, hist - t:hist, :] = new


def _pre_sample(x, mods, g_pre, w_in_bf, cak, cav, cbk, cbv):
    b, t, d = x.shape
    la, lb = cak.shape[1], cbk.shape[1]
    assert t == CHUNK and la == A_WINDOW and lb == B_REACH
    per_b = lambda r, w: pl.BlockSpec((1, r, w), lambda i: (i, 0, 0))
    return pl.pallas_call(
        functools.partial(_pre_sample_kernel, t=t, la=la, lb=lb),
        out_shape=(
            jax.ShapeDtypeStruct((b, t, QA_W), BF16),
            jax.ShapeDtypeStruct((b, la + t, KVA_W), BF16),
            jax.ShapeDtypeStruct((b, la + t, KVA_W), BF16),
            jax.ShapeDtypeStruct((b, t, B_W), BF16),
            jax.ShapeDtypeStruct((b, lb + t, B_W), BF16),
            jax.ShapeDtypeStruct((b, lb + t, B_W), BF16),
            jax.ShapeDtypeStruct((b, la, KVA_W), F32),
            jax.ShapeDtypeStruct((b, la, KVA_W), F32),
            jax.ShapeDtypeStruct((b, lb, B_W), F32),
            jax.ShapeDtypeStruct((b, lb, B_W), F32),
        ),
        grid=(b,),
        in_specs=[
            per_b(t, d),
            per_b(6, d),
            pl.BlockSpec((1, d), lambda i: (0, 0)),
            pl.BlockSpec((d, IN_W), lambda i: (0, 0)),
            per_b(la, KVA_W), per_b(la, KVA_W), per_b(lb, B_W), per_b(lb, B_W),
        ],
        out_specs=(
            per_b(t, QA_W), per_b(la + t, KVA_W), per_b(la + t, KVA_W),
            per_b(t, B_W), per_b(lb + t, B_W), per_b(lb + t, B_W),
            per_b(la, KVA_W), per_b(la, KVA_W), per_b(lb, B_W), per_b(lb, B_W),
        ),
        compiler_params=_cparams("parallel"),
        name="pre_sample",
    )(x, mods, g_pre, w_in_bf, cak, cav, cbk, cbv)


def _attend_a(q, k, v, bias_ref, sink_ref, nk):
    boff = A_BAND - nk
    outs = []
    for kv in range(A_KV_HEADS):
        qs = jnp.concatenate(
            [q[:, (kv * A_GROUP + g) * HEAD_DIM:(kv * A_GROUP + g + 1) * HEAD_DIM] for g in range(A_GROUP)], axis=0)
        kh = k[:, kv * HEAD_DIM:(kv + 1) * HEAD_DIM]
        vh = v[:, kv * HEAD_DIM:(kv + 1) * HEAD_DIM]
        s = lax.dot_general(qs, kh, (((1,), (1,)), ((), ())), preferred_element_type=F32)
        s = s + bias_ref[kv, :, boff:boff + nk]
        sk = sink_ref[kv]
        mx = jnp.maximum(jnp.max(s, axis=-1, keepdims=True), sk)
        e = jnp.exp(s - mx)
        den = jnp.sum(e, axis=-1, keepdims=True) + jnp.exp(sk - mx)
        o = jnp.dot(e.astype(BF16), vh, preferred_element_type=F32) / den
        outs.extend(o[g * CHUNK:(g + 1) * CHUNK, :] for g in range(A_GROUP))
    return jnp.concatenate(outs, axis=1).astype(BF16)


def _attend_b(q, k, v, bias_ref, nk):
    boff = B_BAND - nk
    outs = []
    for h in range(B_HEADS):
        sl = slice(h * HEAD_DIM, (h + 1) * HEAD_DIM)
        s = lax.dot_general(q[:, sl], k[:, sl], (((1,), (1,)), ((), ())), preferred_element_type=F32)
        s = s + bias_ref[h, :, boff:boff + nk]
        mx = jnp.max(s, axis=-1, keepdims=True)
        e = jnp.exp(s - mx)
        den = jnp.sum(e, axis=-1, keepdims=True)
        outs.append(jnp.dot(e.astype(BF16), v[:, sl], preferred_element_type=F32) / den)
    return jnp.concatenate(outs, axis=1).astype(BF16)


def _attn_prompt_kernel(qa_ref, ka_ref, va_ref, qb_ref, kb_ref, vb_ref, ba_ref, sk_ref, bb_ref,
                        oa_ref, ob_ref, *, tile):
    i = pl.program_id(1)
    n_chunks = tile // CHUNK

    @pl.when(i == 0)
    def _():
        for c in range(n_chunks):
            rows = slice(c * CHUNK, (c + 1) * CHUNK)
            na = min(c, A_BACK) + 1
            ka = ka_ref[0, (c + 1 - na) * CHUNK:(c + 1) * CHUNK, :]
            va = va_ref[0, (c + 1 - na) * CHUNK:(c + 1) * CHUNK, :]
            oa_ref[0, rows, :] = _attend_a(qa_ref[0, rows, :], ka, va, ba_ref, sk_ref, na * CHUNK)
            nb = min(c, B_BACK) + 1
            kb = kb_ref[0, (c + 1 - nb) * CHUNK:(c + 1) * CHUNK, :]
            vb = vb_ref[0, (c + 1 - nb) * CHUNK:(c + 1) * CHUNK, :]
            ob_ref[0, rows, :] = _attend_b(qb_ref[0, rows, :], kb, vb, bb_ref, nb * CHUNK)

    @pl.when(i > 0)
    def _():
        def chunk(c, carry):
            r0 = pl.multiple_of(c * CHUNK, CHUNK)
            g0 = i * tile + c * CHUNK
            a0 = pl.multiple_of(g0 - A_WINDOW, CHUNK)
            b0 = pl.multiple_of(g0 - B_REACH, CHUNK)
            ka = ka_ref[0, pl.ds(a0, A_BAND), :]
            va = va_ref[0, pl.ds(a0, A_BAND), :]
            oa_ref[0, pl.ds(r0, CHUNK), :] = _attend_a(qa_ref[0, pl.ds(r0, CHUNK), :], ka, va, ba_ref, sk_ref, A_BAND)
            kb = kb_ref[0, pl.ds(b0, B_BAND), :]
            vb = vb_ref[0, pl.ds(b0, B_BAND), :]
            ob_ref[0, pl.ds(r0, CHUNK), :] = _attend_b(qb_ref[0, pl.ds(r0, CHUNK), :], kb, vb, bb_ref, B_BAND)
            return carry
        lax.fori_loop(0, n_chunks, chunk, 0)


def _attn_prompt(qa, ka, va, qb, kb, vb, bias_a, sinks, bias_b):
    b, s, _ = qa.shape
    tile = SEQ_TILE
    assert s % tile == 0 and tile >= B_REACH
    tok = lambda w: pl.BlockSpec((1, tile, w), lambda i, j: (i, j, 0))
    seq = lambda w: pl.BlockSpec((1, s, w), lambda i, j: (i, 0, 0))
    const = lambda shp: pl.BlockSpec(shp, lambda i, j: (0,) * len(shp))
    return pl.pallas_call(
        functools.partial(_attn_prompt_kernel, tile=tile),
        out_shape=(jax.ShapeDtypeStruct((b, s, QA_W), BF16), jax.ShapeDtypeStruct((b, s, B_W), BF16)),
        grid=(b, s // tile),
        in_specs=[tok(QA_W), seq(KVA_W), seq(KVA_W), tok(B_W), seq(B_W), seq(B_W),
                  const(bias_a.shape), const(sinks.shape), const(bias_b.shape)],
        out_specs=(tok(QA_W), tok(B_W)),
        compiler_params=_cparams("parallel", "arbitrary"),
        name="attn_prompt",
    )(qa, ka, va, qb, kb, vb, bias_a, sinks, bias_b)


def _attn_sample_kernel(qa_ref, ka_ref, va_ref, qb_ref, kb_ref, vb_ref, ba_ref, sk_ref, bb_ref, oa_ref, ob_ref):
    oa_ref[0] = _attend_a(qa_ref[0], ka_ref[0], va_ref[0], ba_ref, sk_ref, A_BAND)
    ob_ref[0] = _attend_b(qb_ref[0], kb_ref[0], vb_ref[0], bb_ref, B_BAND)


def _attn_sample(qa, ka, va, qb, kb, vb, bias_a, sinks, bias_b):
    b, t, _ = qa.shape
    per_b = lambda r, w: pl.BlockSpec((1, r, w), lambda i: (i, 0, 0))
    const = lambda shp: pl.BlockSpec(shp, lambda i: (0,) * len(shp))
    return pl.pallas_call(
        _attn_sample_kernel,
        out_shape=(jax.ShapeDtypeStruct((b, t, QA_W), BF16), jax.ShapeDtypeStruct((b, t, B_W), BF16)),
        grid=(b,),
        in_specs=[per_b(t, QA_W), per_b(A_BAND, KVA_W), per_b(A_BAND, KVA_W),
                  per_b(t, B_W), per_b(B_BAND, B_W), per_b(B_BAND, B_W),
                  const(bias_a.shape), const(sinks.shape), const(bias_b.shape)],
        out_specs=(per_b(t, QA_W), per_b(t, B_W)),
        compiler_params=_cparams("parallel"),
        name="attn_sample",
    )(qa, ka, va, qb, kb, vb, bias_a, sinks, bias_b)


def _route(lt):
    t = lt.shape[1]
    el = lt[0:N_EXPERTS]
    gl = lt[N_EXPERTS:N_EXPERTS + N_GROUPS]
    gmax = jnp.max(gl, axis=0, keepdims=True)
    gi = lax.broadcasted_iota(jnp.int32, (N_GROUPS, t), 0)
    gidx = jnp.min(jnp.where(gl == gmax, gi, N_GROUPS), axis=0, keepdims=True)
    g_w = 1.0 / jnp.sum(jnp.exp(gl - gmax), axis=0, keepdims=True)
    e_sel = el[(N_GROUPS - 1) * EXPERTS_PER_GROUP:]
    for g in range(N_GROUPS - 2, -1, -1):
        e_sel = jnp.where(gidx == g, el[g * EXPERTS_PER_GROUP:(g + 1) * EXPERTS_PER_GROUP], e_sel)
    ei = lax.broadcasted_iota(jnp.int32, (EXPERTS_PER_GROUP, t), 0)
    m1 = jnp.max(e_sel, axis=0, keepdims=True)
    i1 = jnp.min(jnp.where(e_sel == m1, ei, EXPERTS_PER_GROUP), axis=0, keepdims=True)
    rest = jnp.where(ei == i1, -jnp.inf, e_sel)
    m2 = jnp.max(rest, axis=0, keepdims=True)
    i2 = jnp.min(jnp.where(rest == m2, ei, EXPERTS_PER_GROUP), axis=0, keepdims=True)
    ex = jnp.exp(m2 - m1)
    den = 1.0 + ex
    w1 = g_w * (1.0 / den)
    w2 = g_w * (ex / den)
    lo = jnp.minimum(i1, i2)
    hi = jnp.maximum(i1, i2)
    first_is_lo = i1 < i2
    w_lo = jnp.where(first_is_lo, w1, w2)
    w_hi = jnp.where(first_is_lo, w2, w1)
    pair = ((lo * (2 * EXPERTS_PER_GROUP - 1 - lo)) >> 1) + (hi - lo - 1)
    return gidx * PAIRS_PER_GROUP + pair, w_lo, w_hi


def _post_kernel(x_ref, oa_ref, ob_ref, mod_ref, g1_ref, g1p_ref, g2_ref,
                 wg_ref, bg_ref, wpa_ref, wpb_ref, wo_ref, wr_ref, br_ref,
                 x1_ref, h2_ref, cls_ref, rw_ref):
    x = x_ref[0]
    h = _norm_mod(x, g1_ref[...], mod_ref[0, 1:2, :], mod_ref[0, 0:1, :]).astype(BF16)
    gates = jax.nn.sigmoid(jnp.dot(h, wg_ref[...], preferred_element_type=F32) + bg_ref[...])
    pa = jnp.dot(oa_ref[0], wpa_ref[...], preferred_element_type=F32)
    pb = jnp.dot(ob_ref[0], wpb_ref[...], preferred_element_type=F32)
    mixed = gates[:, :D_MODEL] * pa + gates[:, D_MODEL:] * pb
    y = jnp.dot(mixed.astype(BF16), wo_ref[...], preferred_element_type=F32)
    x1 = x + mod_ref[0, 2:3, :] * _rms(y, g1p_ref[...])
    x1_ref[0] = x1
    h2 = _norm_mod(x1, g2_ref[...], mod_ref[0, 4:5, :], mod_ref[0, 3:4, :]).astype(BF16)
    h2_ref[0] = h2.astype(F32)
    lt = lax.dot_general(wr_ref[...], h2, (((1,), (1,)), ((), ())), preferred_element_type=F32) + br_ref[...]
    cls, w_lo, w_hi = _route(lt)
    cls_ref[0, 0] = cls
    rw_ref[0, 0, 0:1, :] = w_lo
    rw_ref[0, 0, 1:2, :] = w_hi


def _post(x, oa, ob, mods, g_pre_mix, g_post_mix, g_pre_ffn, wg, bg, wpa, wpb, wo, wr, br, tile):
    b, s, d = x.shape
    nt = s // tile
    tok = lambda w: pl.BlockSpec((1, tile, w), lambda i, j: (i, j, 0))
    const = lambda shp: pl.BlockSpec(shp, lambda i, j: (0,) * len(shp))
    return pl.pallas_call(
        _post_kernel,
        out_shape=(
            jax.ShapeDtypeStruct((b, s, d), F32),
            jax.ShapeDtypeStruct((b, s, d), F32),
            jax.ShapeDtypeStruct((b, nt, 1, tile), jnp.int32),
            jax.ShapeDtypeStruct((b, nt, 2, tile), F32),
        ),
        grid=(b, nt),
        in_specs=[
            tok(d), tok(QA_W), tok(B_W),
            pl.BlockSpec((1, 6, d), lambda i, j: (i, 0, 0)),
            const((1, d)), const((1, d)), const((1, d)),
            const(wg.shape), const(bg.shape), const(wpa.shape), const(wpb.shape), const(wo.shape),
            const(wr.shape), const(br.shape),
        ],
        out_specs=(
            tok(d), tok(d),
            pl.BlockSpec((1, 1, 1, tile), lambda i, j: (i, j, 0, 0)),
            pl.BlockSpec((1, 1, 2, tile), lambda i, j: (i, j, 0, 0)),
        ),
        compiler_params=_cparams("parallel", "arbitrary"),
        name="post",
    )(x, oa, ob, mods, g_pre_mix, g_post_mix, g_pre_ffn, wg, bg, wpa, wpb, wo, wr, br)


def _gather_kernel(idx_ref, src_ref, out_ref, sems, *, tile, n_steps):
    t = pl.program_id(0)
    slot = t % 2

    def row_copy(row, src_row, sem):
        return pltpu.make_async_copy(src_ref.at[pl.ds(src_row, 1)], out_ref.at[pl.ds(row, 1)], sem)

    def issue(r, carry):
        row = t * tile + r
        row_copy(row, idx_ref[row], sems.at[slot]).start()
        return carry
    lax.fori_loop(0, tile, issue, 0, unroll=8)

    def drain(sem):
        def wait_one(r, carry):
            row_copy(0, 0, sem).wait()
            return carry
        lax.fori_loop(0, tile, wait_one, 0, unroll=8)

    @pl.when(t > 0)
    def _():
        drain(sems.at[1 - slot])

    @pl.when(t == n_steps - 1)
    def _():
        drain(sems.at[slot])


def _row_gather(src, idx):
    r = idx.shape[0]
    d = src.shape[1]
    tile = GATHER_TILE
    assert r % tile == 0
    n_steps = r // tile
    return pl.pallas_call(
        functools.partial(_gather_kernel, tile=tile, n_steps=n_steps),
        out_shape=jax.ShapeDtypeStruct((r, d), src.dtype),
        grid_spec=pltpu.PrefetchScalarGridSpec(
            num_scalar_prefetch=1,
            grid=(n_steps,),
            in_specs=[pl.BlockSpec(memory_space=pl.ANY)],
            out_specs=pl.BlockSpec(memory_space=pl.ANY),
            scratch_shapes=[pltpu.SemaphoreType.DMA((2,))],
        ),
        compiler_params=_cparams("arbitrary"),
        name="row_gather",
    )(idx, src)


def _col_from_row(w_row, n):
    ri = lax.broadcasted_iota(jnp.int32, (n, n), 0)
    ci = lax.broadcasted_iota(jnp.int32, (n, n), 1)
    return jnp.sum(jnp.where(ri == ci, jnp.broadcast_to(w_row, (n, n)), 0.0), axis=1, keepdims=True)


def _expert(xb, wgu_ref, wd_ref):
    gu = jnp.dot(xb, wgu_ref[0], preferred_element_type=F32)
    gate = gu[:, :D_EXPERT]
    he = (gate * jax.nn.sigmoid(gate)) * gu[:, D_EXPERT:]
    return jnp.dot(he.astype(BF16), wd_ref[0], preferred_element_type=F32)


def _moe_kernel(ea_ref, eb_ref, nu_ref, xs_ref, w_ref, wgu_a_ref, wd_a_ref, wgu_b_ref, wd_b_ref, ys_ref, *, tile):
    t = pl.program_id(0)

    @pl.when(t < nu_ref[0])
    def _():
        xb = xs_ref[...].astype(BF16)
        w_lo = _col_from_row(w_ref[0, 0:1, :], tile)
        w_hi = _col_from_row(w_ref[0, 1:2, :], tile)
        ys_ref[...] = w_lo * _expert(xb, wgu_a_ref, wd_a_ref) + w_hi * _expert(xb, wgu_b_ref, wd_b_ref)

    @pl.when(t >= nu_ref[0])
    def _():
        ys_ref[...] = jnp.zeros_like(ys_ref)


def _moe_grouped(xs, w_rows, tile_ea, tile_eb, n_used, wgu, wd, tile):
    p, d = xs.shape
    nt = p // tile
    last = lambda nu: jnp.maximum(nu[0] - 1, 0)
    return pl.pallas_call(
        functools.partial(_moe_kernel, tile=tile),
        out_shape=jax.ShapeDtypeStruct((p, d), F32),
        grid_spec=pltpu.PrefetchScalarGridSpec(
            num_scalar_prefetch=3,
            grid=(nt,),
            in_specs=[
                pl.BlockSpec((tile, d), lambda t, ea, eb, nu: (jnp.minimum(t, last(nu)), 0)),
                pl.BlockSpec((1, 2, tile), lambda t, ea, eb, nu: (jnp.minimum(t, last(nu)), 0, 0)),
                pl.BlockSpec((1, d, 2 * D_EXPERT), lambda t, ea, eb, nu: (ea[t], 0, 0)),
                pl.BlockSpec((1, D_EXPERT, d), lambda t, ea, eb, nu: (ea[t], 0, 0)),
                pl.BlockSpec((1, d, 2 * D_EXPERT), lambda t, ea, eb, nu: (eb[t], 0, 0)),
                pl.BlockSpec((1, D_EXPERT, d), lambda t, ea, eb, nu: (eb[t], 0, 0)),
            ],
            out_specs=pl.BlockSpec((tile, d), lambda t, ea, eb, nu: (t, 0)),
        ),
        compiler_params=_cparams("arbitrary"),
        name="moe_grouped",
    )(tile_ea, tile_eb, n_used, xs, w_rows, wgu, wd, wgu, wd)


_PAIR_LO, _PAIR_HI = np.triu_indices(EXPERTS_PER_GROUP, k=1)
_CLASS_LO = np.concatenate([g * EXPERTS_PER_GROUP + _PAIR_LO for g in range(N_GROUPS)]).astype(np.int32)
_CLASS_HI = np.concatenate([g * EXPERTS_PER_GROUP + _PAIR_HI for g in range(N_GROUPS)]).astype(np.int32)


def _moe(h2, cls, w_lo, w_hi, wgu, wd, tile):
    n, d = h2.shape
    p_max = -(-(n + N_CLASSES * (tile - 1)) // GATHER_TILE) * GATHER_TILE
    p_max = -(-p_max // tile) * tile
    nt = p_max // tile
    order = jnp.argsort(cls, stable=True).astype(jnp.int32)
    sorted_cls = cls[order]
    counts = jnp.sum((cls[:, None] == jnp.arange(N_CLASSES, dtype=jnp.int32)[None, :]).astype(jnp.int32), axis=0)
    padded = ((counts + tile - 1) // tile) * tile
    pad_end = jnp.cumsum(padded)
    gstart = pad_end - padded
    ustart = jnp.cumsum(counts) - counts
    pos_sorted = gstart[sorted_cls] + jnp.arange(n, dtype=jnp.int32) - ustart[sorted_cls]
    pos = jnp.zeros((n,), jnp.int32).at[order].set(pos_sorted)
    src_tok = jnp.zeros((p_max,), jnp.int32).at[pos_sorted].set(order)
    n_used = (pad_end[-1] // tile).astype(jnp.int32).reshape(1)
    tile_cls = jnp.minimum(jnp.searchsorted(pad_end, jnp.arange(nt, dtype=jnp.int32) * tile, side="right"),
                           N_CLASSES - 1).astype(jnp.int32)
    tile_ea = jnp.asarray(_CLASS_LO)[tile_cls]
    tile_eb = jnp.asarray(_CLASS_HI)[tile_cls]
    w_rows = jnp.stack([w_lo[src_tok], w_hi[src_tok]], axis=0)
    w_rows = w_rows.reshape(2, nt, tile).transpose(1, 0, 2)

    xs = _row_gather(h2, src_tok)
    ys = _moe_grouped(xs, w_rows, tile_ea, tile_eb, n_used, wgu, wd, tile)
    n_pad = -(-n // GATHER_TILE) * GATHER_TILE
    pos_pad = jnp.concatenate([pos, jnp.zeros((n_pad - n,), jnp.int32)]) if n_pad != n else pos
    y = _row_gather(ys, pos_pad)
    return y[:n] if n_pad != n else y


def _final_kernel(x1_ref, y_ref, mod_ref, g_ref, o_ref):
    o_ref[0] = x1_ref[0] + mod_ref[0, 5:6, :] * _rms(y_ref[0], g_ref[...])


def _final(x1, y, mods, g_post_ffn, tile):
    b, s, d = x1.shape
    tok = pl.BlockSpec((1, tile, d), lambda i, j: (i, j, 0))
    return pl.pallas_call(
        _final_kernel,
        out_shape=jax.ShapeDtypeStruct((b, s, d), F32),
        grid=(b, s // tile),
        in_specs=[tok, tok, pl.BlockSpec((1, 6, d), lambda i, j: (i, 0, 0)), pl.BlockSpec((1, d), lambda i, j: (0, 0))],
        out_specs=tok,
        compiler_params=_cparams("parallel", "arbitrary"),
        name="final",
    )(x1, y, mods, g_post_ffn)


def _t5_bucket(rel):
    half = T5_BUCKETS // 2
    exact = half // 2
    ret = jnp.where(rel > 0, half, 0)
    n = jnp.abs(rel)
    nf = jnp.maximum(n, 1).astype(F32)
    large = exact + (jnp.log(nf / exact) / math.log(T5_MAX_DIST / exact) * (half - exact)).astype(jnp.int32)
    large = jnp.minimum(large, half - 1)
    return ret + jnp.where(n < exact, n, large)


def _bias_tables(t5_table, rel_table):
    rel = jnp.arange(A_BAND)[None, :] - A_WINDOW - jnp.arange(CHUNK)[:, None]
    bias_a = jnp.transpose(t5_table[_t5_bucket(rel)], (2, 0, 1)).astype(F32)
    bias_a = bias_a.reshape(A_KV_HEADS, A_GROUP * CHUNK, A_BAND)
    dist = jnp.arange(CHUNK)[:, None] + B_REACH - jnp.arange(B_BAND)[None, :]
    bias_b = rel_table[:, jnp.clip(dist, -REL_CLIP, REL_CLIP) + REL_CLIP].astype(F32)
    return bias_a, bias_b


def _moe_and_final(x1, h2, cls, rw, mods, g_post_ffn, wgu, wd, seq_tile, moe_tile):
    b, s, d = x1.shape
    n = b * s
    cls_f = cls.reshape(n)
    w_lo = rw[:, :, 0, :].reshape(n)
    w_hi = rw[:, :, 1, :].reshape(n)
    y = _moe(h2.reshape(n, d), cls_f, w_lo, w_hi, wgu, wd, moe_tile)
    return _final(x1, y.reshape(b, s, d), mods, g_post_ffn, seq_tile)


def kernel(x_prompt, x_sample, c_prompt, c_sample, cache_a_k, cache_a_v, cache_b_k, cache_b_v, w_ada, b_ada, g_pre_mix, g_post_mix, g_pre_ffn, g_post_ffn, w_in, a_sinks, t5_table, b_rel_table, w_proj_a, w_proj_b, w_gate, b_gate, w_o, w_route_g, b_route_g, w_route_e, b_route_e, w_e_gate, w_e_up, w_e_down):
    depth = w_in.shape[0]
    assert depth == 1
    l = 0
    bp, sp, d = x_prompt.shape
    bs, ts, _ = x_sample.shape

    mods = _ada(jnp.concatenate([c_prompt, c_sample], axis=0), w_ada[l], b_ada[l]).reshape(bp + bs, 6, d)
    mods_p, mods_s = mods[:bp], mods[bp:]

    w_in_bf = w_in[l].astype(BF16)
    wg, wpa, wpb, wo = w_gate[l].astype(BF16), w_proj_a[l].astype(BF16), w_proj_b[l].astype(BF16), w_o[l].astype(BF16)
    bg = b_gate[l].reshape(1, 2 * d)
    pad_rows = ROUTE_ROWS - N_EXPERTS - N_GROUPS
    wr = jnp.concatenate([w_route_e[l].T, w_route_g[l].T, jnp.zeros((pad_rows, d), F32)], axis=0).astype(BF16)
    br = jnp.concatenate([b_route_e[l], b_route_g[l], jnp.zeros((pad_rows,), F32)]).reshape(ROUTE_ROWS, 1)
    wgu = jnp.concatenate([w_e_gate[l], w_e_up[l]], axis=-1).astype(BF16)
    wd = w_e_down[l].astype(BF16)
    g1, g1p, g2, g2p = (g[l].reshape(1, d) for g in (g_pre_mix, g_post_mix, g_pre_ffn, g_post_ffn))

    bias_a, bias_b = _bias_tables(t5_table, b_rel_table[l])
    sinks = jnp.broadcast_to(a_sinks[l].reshape(A_KV_HEADS, A_GROUP, 1, 1), (A_KV_HEADS, A_GROUP, CHUNK, 1))
    sinks = sinks.reshape(A_KV_HEADS, A_GROUP * CHUNK, 1).astype(F32)

    qa, ka, va, qb, kb, vb, sak_p, sav_p, sbk_p, sbv_p = _pre_prompt(x_prompt, mods_p, g1, w_in_bf)
    oa, ob = _attn_prompt(qa, ka, va, qb, kb, vb, bias_a, sinks, bias_b)
    x1, h2, cls, rw = _post(x_prompt, oa, ob, mods_p, g1, g1p, g2, wg, bg, wpa, wpb, wo, wr, br, SEQ_TILE)
    y_prompt = _moe_and_final(x1, h2, cls, rw, mods_p, g2p, wgu, wd, SEQ_TILE, 256)

    la, lb = cache_a_k.shape[2], cache_b_k.shape[2]
    cak = cache_a_k[l].reshape(bs, la, KVA_W)
    cav = cache_a_v[l].reshape(bs, la, KVA_W)
    cbk = cache_b_k[l].reshape(bs, lb, B_W)
    cbv = cache_b_v[l].reshape(bs, lb, B_W)
    qa, ka, va, qb, kb, vb, sak_s, sav_s, sbk_s, sbv_s = _pre_sample(x_sample, mods_s, g1, w_in_bf, cak, cav, cbk, cbv)
    oa, ob = _attn_sample(qa, ka, va, qb, kb, vb, bias_a, sinks, bias_b)
    x1, h2, cls, rw = _post(x_sample, oa, ob, mods_s, g1, g1p, g2, wg, bg, wpa, wpb, wo, wr, br, ts)
    y_sample = _moe_and_final(x1, h2, cls, rw, mods_s, g2p, wgu, wd, ts, 64)

    a_state = lambda v, b, r: v.reshape(1, b, r, A_KV_HEADS, HEAD_DIM)
    b_state = lambda v, b, r: v.reshape(1, b, r, B_HEADS, HEAD_DIM)
    return (y_prompt, y_sample,
            a_state(sak_p, bp, A_WINDOW), a_state(sav_p, bp, A_WINDOW),
            b_state(sbk_p, bp, B_REACH), b_state(sbv_p, bp, B_REACH),
            a_state(sak_s, bs, la), a_state(sav_s, bs, la),
            b_state(sbk_s, bs, lb), b_state(sbv_s, bs, lb))
```

```python
import functools
import math

import numpy as np
import jax
import jax.numpy as jnp
from jax import lax
from jax.experimental import pallas as pl
from jax.experimental.pallas import tpu as pltpu

D_MODEL = 1024
CHUNK = 64
HEAD_DIM = 64
A_Q_HEADS = 8
A_KV_HEADS = 2
A_GROUP = A_Q_HEADS // A_KV_HEADS
A_WINDOW = 128
A_BACK = A_WINDOW // CHUNK
B_HEADS = 8
B_BACK = 8
B_REACH = B_BACK * CHUNK
REL_CLIP = 256
T5_BUCKETS = 32
T5_MAX_DIST = 128
N_GROUPS = 4
EXPERTS_PER_GROUP = 8
N_EXPERTS = N_GROUPS * EXPERTS_PER_GROUP
D_EXPERT = D_MODEL // 4
EPS = 1e-6

QA_W = A_Q_HEADS * HEAD_DIM
KVA_W = A_KV_HEADS * HEAD_DIM
B_W = B_HEADS * HEAD_DIM
IN_W = QA_W + 2 * KVA_W + 3 * B_W
A_BAND = A_WINDOW + CHUNK
B_BAND = B_REACH + CHUNK

PAIRS_PER_GROUP = EXPERTS_PER_GROUP * (EXPERTS_PER_GROUP - 1) // 2
N_CLASSES = N_GROUPS * PAIRS_PER_GROUP
ROUTE_ROWS = 40
CLASS_ROWS = 128
ROW_W = D_MODEL + 128

F32 = jnp.float32
BF16 = jnp.bfloat16

VMEM_LIMIT_BYTES = 56 * 1024 * 1024

SEQ_TILE = 512


def _cparams(*sem):
    return pltpu.CompilerParams(dimension_semantics=sem, vmem_limit_bytes=VMEM_LIMIT_BYTES)


def _norm_mod(x, g, scale, shift):
    y = x * lax.rsqrt(jnp.mean(x * x, axis=-1, keepdims=True) + EPS)
    return (y * g) * (1.0 + scale) + shift


def _rms(x, g):
    return (x * lax.rsqrt(jnp.mean(x * x, axis=-1, keepdims=True) + EPS)) * g


def _ada_kernel(c_ref, w_ref, b_ref, o_ref):
    c = c_ref[...]
    s = (c * jax.nn.sigmoid(c)).astype(BF16)
    o_ref[...] = jnp.dot(s, w_ref[...].astype(BF16), preferred_element_type=F32) + b_ref[...]


def _ada(c, w_ada, b_ada):
    n, d = c.shape
    wn = w_ada.shape[1]
    tn = 512
    return pl.pallas_call(
        _ada_kernel,
        out_shape=jax.ShapeDtypeStruct((n, wn), F32),
        grid=(wn // tn,),
        in_specs=[
            pl.BlockSpec((n, d), lambda j: (0, 0)),
            pl.BlockSpec((d, tn), lambda j: (0, j)),
            pl.BlockSpec((1, tn), lambda j: (0, j)),
        ],
        out_specs=pl.BlockSpec((n, tn), lambda j: (0, j)),
        compiler_params=_cparams("arbitrary"),
        name="ada",
    )(c, w_ada, b_ada.reshape(1, wn))


_COL_QA = (0, QA_W)
_COL_KA = (QA_W, QA_W + KVA_W)
_COL_VA = (QA_W + KVA_W, QA_W + 2 * KVA_W)
_COL_QB = (QA_W + 2 * KVA_W, QA_W + 2 * KVA_W + B_W)
_COL_KB = (_COL_QB[1], _COL_QB[1] + B_W)
_COL_VB = (_COL_KB[1], _COL_KB[1] + B_W)
Q_SCALE = HEAD_DIM ** -0.5


def _project(x_ref, mod_ref, g_ref, w_ref):
    h = _norm_mod(x_ref[0], g_ref[...], mod_ref[0, 1:2, :], mod_ref[0, 0:1, :])
    return jnp.dot(h.astype(BF16), w_ref[...], preferred_element_type=F32)


def _cols(p, c):
    return p[:, c[0]:c[1]]


def _pre_prompt_kernel(x_ref, mod_ref, g_ref, w_ref,
                       qa_ref, ka_ref, va_ref, qb_ref, kb_ref, vb_ref,
                       sak_ref, sav_ref, sbk_ref, sbv_ref, *, n_tiles, tile):
    p = _project(x_ref, mod_ref, g_ref, w_ref)
    qa_ref[0] = (_cols(p, _COL_QA) * Q_SCALE).astype(BF16)
    ka_ref[0] = _cols(p, _COL_KA).astype(BF16)
    va_ref[0] = _cols(p, _COL_VA).astype(BF16)
    qb_ref[0] = (_cols(p, _COL_QB) * Q_SCALE).astype(BF16)
    kb_ref[0] = _cols(p, _COL_KB).astype(BF16)
    vb_ref[0] = _cols(p, _COL_VB).astype(BF16)

    @pl.when(pl.program_id(1) == n_tiles - 1)
    def _():
        sak_ref[0] = _cols(p, _COL_KA)[tile - A_WINDOW:, :]
        sav_ref[0] = _cols(p, _COL_VA)[tile - A_WINDOW:, :]
        sbk_ref[0] = _cols(p, _COL_KB)[tile - B_REACH:, :]
        sbv_ref[0] = _cols(p, _COL_VB)[tile - B_REACH:, :]


def _pre_prompt(x, mods, g_pre, w_in_bf):
    b, s, d = x.shape
    tile = SEQ_TILE
    assert s % tile == 0 and tile >= B_REACH and s >= B_REACH
    nt = s // tile
    tok = lambda w: pl.BlockSpec((1, tile, w), lambda i, j: (i, j, 0))
    state = lambda r, w: pl.BlockSpec((1, r, w), lambda i, j: (i, 0, 0))
    return pl.pallas_call(
        functools.partial(_pre_prompt_kernel, n_tiles=nt, tile=tile),
        out_shape=(
            jax.ShapeDtypeStruct((b, s, QA_W), BF16),
            jax.ShapeDtypeStruct((b, s, KVA_W), BF16),
            jax.ShapeDtypeStruct((b, s, KVA_W), BF16),
            jax.ShapeDtypeStruct((b, s, B_W), BF16),
            jax.ShapeDtypeStruct((b, s, B_W), BF16),
            jax.ShapeDtypeStruct((b, s, B_W), BF16),
            jax.ShapeDtypeStruct((b, A_WINDOW, KVA_W), F32),
            jax.ShapeDtypeStruct((b, A_WINDOW, KVA_W), F32),
            jax.ShapeDtypeStruct((b, B_REACH, B_W), F32),
            jax.ShapeDtypeStruct((b, B_REACH, B_W), F32),
        ),
        grid=(b, nt),
        in_specs=[
            tok(d),
            pl.BlockSpec((1, 6, d), lambda i, j: (i, 0, 0)),
            pl.BlockSpec((1, d), lambda i, j: (0, 0)),
            pl.BlockSpec((d, IN_W), lambda i, j: (0, 0)),
        ],
        out_specs=(
            tok(QA_W), tok(KVA_W), tok(KVA_W), tok(B_W), tok(B_W), tok(B_W),
            state(A_WINDOW, KVA_W), state(A_WINDOW, KVA_W), state(B_REACH, B_W), state(B_REACH, B_W),
        ),
        compiler_params=_cparams("parallel", "arbitrary"),
        name="pre_prompt",
    )(x, mods, g_pre, w_in_bf)


def _pre_sample_kernel(x_ref, mod_ref, g_ref, w_ref, cak_ref, cav_ref, cbk_ref, cbv_ref,
                       qa_ref, ka_ref, va_ref, qb_ref, kb_ref, vb_ref,
                       sak_ref, sav_ref, sbk_ref, sbv_ref, *, t, la, lb):
    p = _project(x_ref, mod_ref, g_ref, w_ref)
    qa_ref[0] = (_cols(p, _COL_QA) * Q_SCALE).astype(BF16)
    qb_ref[0] = (_cols(p, _COL_QB) * Q_SCALE).astype(BF16)
    for cache_ref, band_ref, state_ref, col, hist in (
            (cak_ref, ka_ref, sak_ref, _COL_KA, la), (cav_ref, va_ref, sav_ref, _COL_VA, la),
            (cbk_ref, kb_ref, sbk_ref, _COL_KB, lb), (cbv_ref, vb_ref, sbv_ref, _COL_VB, lb)):
        new = _cols(p, col)
        cache = cache_ref[0]
        band_ref[0, 0:hist, :] = cache.astype(BF16)
        band_ref[0, hist:hist + t, :] = new.astype(BF16)
        state_ref[0, 0:hist - t, :] = cache[t:, :]
        state_ref[0, hist - t:hist, :] = new


def _pre_sample(x, mods, g_pre, w_in_bf, cak, cav, cbk, cbv):
    b, t, d = x.shape
    la, lb = cak.shape[1], cbk.shape[1]
    assert t == CHUNK and la == A_WINDOW and lb == B_REACH
    per_b = lambda r, w: pl.BlockSpec((1, r, w), lambda i: (i, 0, 0))
    return pl.pallas_call(
        functools.partial(_pre_sample_kernel, t=t, la=la, lb=lb),
        out_shape=(
            jax.ShapeDtypeStruct((b, t, QA_W), BF16),
            jax.ShapeDtypeStruct((b, la + t, KVA_W), BF16),
            jax.ShapeDtypeStruct((b, la + t, KVA_W), BF16),
            jax.ShapeDtypeStruct((b, t, B_W), BF16),
            jax.ShapeDtypeStruct((b, lb + t, B_W), BF16),
            jax.ShapeDtypeStruct((b, lb + t, B_W), BF16),
            jax.ShapeDtypeStruct((b, la, KVA_W), F32),
            jax.ShapeDtypeStruct((b, la, KVA_W), F32),
            jax.ShapeDtypeStruct((b, lb, B_W), F32),
            jax.ShapeDtypeStruct((b, lb, B_W), F32),
        ),
        grid=(b,),
        in_specs=[
            per_b(t, d),
            per_b(6, d),
            pl.BlockSpec((1, d), lambda i: (0, 0)),
            pl.BlockSpec((d, IN_W), lambda i: (0, 0)),
            per_b(la, KVA_W), per_b(la, KVA_W), per_b(lb, B_W), per_b(lb, B_W),
        ],
        out_specs=(
            per_b(t, QA_W), per_b(la + t, KVA_W), per_b(la + t, KVA_W),
            per_b(t, B_W), per_b(lb + t, B_W), per_b(lb + t, B_W),
            per_b(la, KVA_W), per_b(la, KVA_W), per_b(lb, B_W), per_b(lb, B_W),
        ),
        compiler_params=_cparams("parallel"),
        name="pre_sample",
    )(x, mods, g_pre, w_in_bf, cak, cav, cbk, cbv)


def _attend_a(q, k, v, bias_ref, sink_ref, nk):
    boff = A_BAND - nk
    outs = []
    for kv in range(A_KV_HEADS):
        qs = jnp.concatenate(
            [q[:, (kv * A_GROUP + g) * HEAD_DIM:(kv * A_GROUP + g + 1) * HEAD_DIM] for g in range(A_GROUP)], axis=0)
        kh = k[:, kv * HEAD_DIM:(kv + 1) * HEAD_DIM]
        vh = v[:, kv * HEAD_DIM:(kv + 1) * HEAD_DIM]
        s = lax.dot_general(qs, kh, (((1,), (1,)), ((), ())), preferred_element_type=F32)
        s = s + bias_ref[kv, :, boff:boff + nk]
        sk = sink_ref[kv]
        mx = jnp.maximum(jnp.max(s, axis=-1, keepdims=True), sk)
        e = jnp.exp(s - mx)
        den = jnp.sum(e, axis=-1, keepdims=True) + jnp.exp(sk - mx)
        o = jnp.dot(e.astype(BF16), vh, preferred_element_type=F32) / den
        outs.extend(o[g * CHUNK:(g + 1) * CHUNK, :] for g in range(A_GROUP))
    return jnp.concatenate(outs, axis=1).astype(BF16)


def _attend_b(q, k, v, bias_ref, nk):
    boff = B_BAND - nk
    outs = []
    for h in range(B_HEADS):
        sl = slice(h * HEAD_DIM, (h + 1) * HEAD_DIM)
        s = lax.dot_general(q[:, sl], k[:, sl], (((1,), (1,)), ((), ())), preferred_element_type=F32)
        s = s + bias_ref[h, :, boff:boff + nk]
        mx = jnp.max(s, axis=-1, keepdims=True)
        e = jnp.exp(s - mx)
        den = jnp.sum(e, axis=-1, keepdims=True)
        outs.append(jnp.dot(e.astype(BF16), v[:, sl], preferred_element_type=F32) / den)
    return jnp.concatenate(outs, axis=1).astype(BF16)


def _attn_prompt_kernel(qa_ref, ka_ref, va_ref, qb_ref, kb_ref, vb_ref, ba_ref, sk_ref, bb_ref,
                        oa_ref, ob_ref, *, tile):
    i = pl.program_id(1)
    n_chunks = tile // CHUNK

    @pl.when(i == 0)
    def _():
        for c in range(n_chunks):
            rows = slice(c * CHUNK, (c + 1) * CHUNK)
            na = min(c, A_BACK) + 1
            ka = ka_ref[0, (c + 1 - na) * CHUNK:(c + 1) * CHUNK, :]
            va = va_ref[0, (c + 1 - na) * CHUNK:(c + 1) * CHUNK, :]
            oa_ref[0, rows, :] = _attend_a(qa_ref[0, rows, :], ka, va, ba_ref, sk_ref, na * CHUNK)
            nb = min(c, B_BACK) + 1
            kb = kb_ref[0, (c + 1 - nb) * CHUNK:(c + 1) * CHUNK, :]
            vb = vb_ref[0, (c + 1 - nb) * CHUNK:(c + 1) * CHUNK, :]
            ob_ref[0, rows, :] = _attend_b(qb_ref[0, rows, :], kb, vb, bb_ref, nb * CHUNK)

    @pl.when(i > 0)
    def _():
        def chunk(c, carry):
            r0 = pl.multiple_of(c * CHUNK, CHUNK)
            g0 = i * tile + c * CHUNK
            a0 = pl.multiple_of(g0 - A_WINDOW, CHUNK)
            b0 = pl.multiple_of(g0 - B_REACH, CHUNK)
            ka = ka_ref[0, pl.ds(a0, A_BAND), :]
            va = va_ref[0, pl.ds(a0, A_BAND), :]
            oa_ref[0, pl.ds(r0, CHUNK), :] = _attend_a(qa_ref[0, pl.ds(r0, CHUNK), :], ka, va, ba_ref, sk_ref, A_BAND)
            kb = kb_ref[0, pl.ds(b0, B_BAND), :]
            vb = vb_ref[0, pl.ds(b0, B_BAND), :]
            ob_ref[0, pl.ds(r0, CHUNK), :] = _attend_b(qb_ref[0, pl.ds(r0, CHUNK), :], kb, vb, bb_ref, B_BAND)
            return carry
        lax.fori_loop(0, n_chunks, chunk, 0)


def _attn_prompt(qa, ka, va, qb, kb, vb, bias_a, sinks, bias_b):
    b, s, _ = qa.shape
    tile = SEQ_TILE
    assert s % tile == 0 and tile >= B_REACH
    tok = lambda w: pl.BlockSpec((1, tile, w), lambda i, j: (i, j, 0))
    seq = lambda w: pl.BlockSpec((1, s, w), lambda i, j: (i, 0, 0))
    const = lambda shp: pl.BlockSpec(shp, lambda i, j: (0,) * len(shp))
    return pl.pallas_call(
        functools.partial(_attn_prompt_kernel, tile=tile),
        out_shape=(jax.ShapeDtypeStruct((b, s, QA_W), BF16), jax.ShapeDtypeStruct((b, s, B_W), BF16)),
        grid=(b, s // tile),
        in_specs=[tok(QA_W), seq(KVA_W), seq(KVA_W), tok(B_W), seq(B_W), seq(B_W),
                  const(bias_a.shape), const(sinks.shape), const(bias_b.shape)],
        out_specs=(tok(QA_W), tok(B_W)),
        compiler_params=_cparams("parallel", "arbitrary"),
        name="attn_prompt",
    )(qa, ka, va, qb, kb, vb, bias_a, sinks, bias_b)


def _attn_sample_kernel(qa_ref, ka_ref, va_ref, qb_ref, kb_ref, vb_ref, ba_ref, sk_ref, bb_ref, oa_ref, ob_ref):
    oa_ref[0] = _attend_a(qa_ref[0], ka_ref[0], va_ref[0], ba_ref, sk_ref, A_BAND)
    ob_ref[0] = _attend_b(qb_ref[0], kb_ref[0], vb_ref[0], bb_ref, B_BAND)


def _attn_sample(qa, ka, va, qb, kb, vb, bias_a, sinks, bias_b):
    b, t, _ = qa.shape
    per_b = lambda r, w: pl.BlockSpec((1, r, w), lambda i: (i, 0, 0))
    const = lambda shp: pl.BlockSpec(shp, lambda i: (0,) * len(shp))
    return pl.pallas_call(
        _attn_sample_kernel,
        out_shape=(jax.ShapeDtypeStruct((b, t, QA_W), BF16), jax.ShapeDtypeStruct((b, t, B_W), BF16)),
        grid=(b,),
        in_specs=[per_b(t, QA_W), per_b(A_BAND, KVA_W), per_b(A_BAND, KVA_W),
                  per_b(t, B_W), per_b(B_BAND, B_W), per_b(B_BAND, B_W),
                  const(bias_a.shape), const(sinks.shape), const(bias_b.shape)],
        out_specs=(per_b(t, QA_W), per_b(t, B_W)),
        compiler_params=_cparams("parallel"),
        name="attn_sample",
    )(qa, ka, va, qb, kb, vb, bias_a, sinks, bias_b)


def _route(lt):
    t = lt.shape[1]
    el = lt[0:N_EXPERTS]
    gl = lt[N_EXPERTS:N_EXPERTS + N_GROUPS]
    gmax = jnp.max(gl, axis=0, keepdims=True)
    gi = lax.broadcasted_iota(jnp.int32, (N_GROUPS, t), 0)
    gidx = jnp.min(jnp.where(gl == gmax, gi, N_GROUPS), axis=0, keepdims=True)
    g_w = 1.0 / jnp.sum(jnp.exp(gl - gmax), axis=0, keepdims=True)
    e_sel = el[(N_GROUPS - 1) * EXPERTS_PER_GROUP:]
    for g in range(N_GROUPS - 2, -1, -1):
        e_sel = jnp.where(gidx == g, el[g * EXPERTS_PER_GROUP:(g + 1) * EXPERTS_PER_GROUP], e_sel)
    ei = lax.broadcasted_iota(jnp.int32, (EXPERTS_PER_GROUP, t), 0)
    m1 = jnp.max(e_sel, axis=0, keepdims=True)
    i1 = jnp.min(jnp.where(e_sel == m1, ei, EXPERTS_PER_GROUP), axis=0, keepdims=True)
    rest = jnp.where(ei == i1, -jnp.inf, e_sel)
    m2 = jnp.max(rest, axis=0, keepdims=True)
    i2 = jnp.min(jnp.where(rest == m2, ei, EXPERTS_PER_GROUP), axis=0, keepdims=True)
    ex = jnp.exp(m2 - m1)
    den = 1.0 + ex
    w1 = g_w * (1.0 / den)
    w2 = g_w * (ex / den)
    lo = jnp.minimum(i1, i2)
    hi = jnp.maximum(i1, i2)
    first_is_lo = i1 < i2
    w_lo = jnp.where(first_is_lo, w1, w2)
    w_hi = jnp.where(first_is_lo, w2, w1)
    pair = ((lo * (2 * EXPERTS_PER_GROUP - 1 - lo)) >> 1) + (hi - lo - 1)
    return gidx * PAIRS_PER_GROUP + pair, w_lo, w_hi


def _col_from_row(w_row, n):
    ri = lax.broadcasted_iota(jnp.int32, (n, n), 0)
    ci = lax.broadcasted_iota(jnp.int32, (n, n), 1)
    return jnp.sum(jnp.where(ri == ci, jnp.broadcast_to(w_row, (n, n)), 0.0), axis=1, keepdims=True)


def _rank_in_class(cls, run_ref, tri_ref):
    t = cls.shape[1]
    onehot = lax.broadcasted_iota(jnp.int32, (CLASS_ROWS, t), 0) == cls
    ones = jnp.where(onehot, 1.0, 0.0)
    before = jnp.dot(ones.astype(BF16), tri_ref[...], preferred_element_type=F32)
    run = run_ref[...]
    rank = jnp.sum(jnp.where(onehot, before + run, 0.0), axis=0, keepdims=True)
    run_ref[...] = run + jnp.sum(ones, axis=1, keepdims=True)
    return rank.astype(jnp.int32)


def _post_kernel(x_ref, oa_ref, ob_ref, mod_ref, g1_ref, g1p_ref, g2_ref,
                 wg_ref, bg_ref, wpa_ref, wpb_ref, wo_ref, wr_ref, br_ref, tri_ref,
                 x1_ref, h2_ref, cls_ref, rank_ref, cnt_ref, run_ref):
    @pl.when((pl.program_id(0) == 0) & (pl.program_id(1) == 0))
    def _():
        run_ref[...] = jnp.zeros_like(run_ref)

    x = x_ref[0]
    h = _norm_mod(x, g1_ref[...], mod_ref[0, 1:2, :], mod_ref[0, 0:1, :]).astype(BF16)
    gates = jax.nn.sigmoid(jnp.dot(h, wg_ref[...], preferred_element_type=F32) + bg_ref[...])
    pa = jnp.dot(oa_ref[0], wpa_ref[...], preferred_element_type=F32)
    pb = jnp.dot(ob_ref[0], wpb_ref[...], preferred_element_type=F32)
    mixed = gates[:, :D_MODEL] * pa + gates[:, D_MODEL:] * pb
    y = jnp.dot(mixed.astype(BF16), wo_ref[...], preferred_element_type=F32)
    x1 = x + mod_ref[0, 2:3, :] * _rms(y, g1p_ref[...])
    x1_ref[0] = x1
    h2 = _norm_mod(x1, g2_ref[...], mod_ref[0, 4:5, :], mod_ref[0, 3:4, :]).astype(BF16)
    lt = lax.dot_general(wr_ref[...], h2, (((1,), (1,)), ((), ())), preferred_element_type=F32) + br_ref[...]
    cls, w_lo, w_hi = _route(lt)
    cls_ref[0, 0] = cls
    rank_ref[0, 0] = _rank_in_class(cls, run_ref, tri_ref)
    cnt_ref[...] = run_ref[...]
    t = x.shape[0]
    lane = lax.broadcasted_iota(jnp.int32, (t, ROW_W - D_MODEL), 1)
    h2_ref[0, :, 0:D_MODEL] = h2.astype(F32)
    h2_ref[0, :, D_MODEL:ROW_W] = jnp.where(lane == 0, _col_from_row(w_lo, t),
                                            jnp.where(lane == 1, _col_from_row(w_hi, t), 0.0))


def _post(x, oa, ob, mods, g_pre_mix, g_post_mix, g_pre_ffn, wg, bg, wpa, wpb, wo, wr, br, tile):
    b, s, d = x.shape
    nt = s // tile
    tri = (jnp.arange(tile)[:, None] < jnp.arange(tile)[None, :]).astype(BF16)
    tok = lambda w: pl.BlockSpec((1, tile, w), lambda i, j: (i, j, 0))
    const = lambda shp: pl.BlockSpec(shp, lambda i, j: (0,) * len(shp))
    per_tile = pl.BlockSpec((1, 1, 1, tile), lambda i, j: (i, j, 0, 0))
    return pl.pallas_call(
        _post_kernel,
        out_shape=(
            jax.ShapeDtypeStruct((b, s, d), F32),
            jax.ShapeDtypeStruct((b, s, ROW_W), F32),
            jax.ShapeDtypeStruct((b, nt, 1, tile), jnp.int32),
            jax.ShapeDtypeStruct((b, nt, 1, tile), jnp.int32),
            jax.ShapeDtypeStruct((CLASS_ROWS, 1), F32),
        ),
        grid=(b, nt),
        in_specs=[
            tok(d), tok(QA_W), tok(B_W),
            pl.BlockSpec((1, 6, d), lambda i, j: (i, 0, 0)),
            const((1, d)), const((1, d)), const((1, d)),
            const(wg.shape), const(bg.shape), const(wpa.shape), const(wpb.shape), const(wo.shape),
            const(wr.shape), const(br.shape), const(tri.shape),
        ],
        out_specs=(tok(d), tok(ROW_W), per_tile, per_tile, const((CLASS_ROWS, 1))),
        scratch_shapes=[pltpu.VMEM((CLASS_ROWS, 1), F32)],
        compiler_params=_cparams("arbitrary", "arbitrary"),
        name="post",
    )(x, oa, ob, mods, g_pre_mix, g_post_mix, g_pre_ffn, wg, bg, wpa, wpb, wo, wr, br, tri)


def _dispatch_kernel(pos_ref, pad_start_ref, pad_len_ref, nu_ref, h_ref, xs_ref, zero_ref, sem, *,
                     tile, n_steps, moe_tile, n_moe_tiles):
    t = pl.program_id(0)

    def wait_rows(n):
        def wait_one(r, carry):
            pltpu.make_async_copy(h_ref.at[pl.ds(0, 1)], xs_ref.at[pl.ds(0, 1)], sem).wait()
            return carry
        lax.fori_loop(0, n, wait_one, 0)

    def issue(r, carry):
        pltpu.make_async_copy(h_ref.at[pl.ds(r, 1)], xs_ref.at[pl.ds(pos_ref[t * tile + r], 1)], sem).start()
        return carry
    lax.fori_loop(0, tile, issue, 0, unroll=8)
    wait_rows(tile)

    @pl.when(t == n_steps - 1)
    def _():
        zero_ref[...] = jnp.zeros_like(zero_ref)

        def per_class(c, carry):
            start, n = pad_start_ref[c], pad_len_ref[c]

            def fill(r, inner):
                pltpu.make_async_copy(zero_ref.at[pl.ds(0, 1)], xs_ref.at[pl.ds(start + r, 1)], sem).start()
                return inner
            lax.fori_loop(0, n, fill, 0)
            wait_rows(n)
            return carry
        lax.fori_loop(0, N_CLASSES, per_class, 0)

        def tile_copy(k):
            return pltpu.make_async_copy(zero_ref, xs_ref.at[pl.ds(k * moe_tile, moe_tile)], sem)

        def zero_tile(k, carry):
            tile_copy(k).start()
            return carry
        lax.fori_loop(nu_ref[0], n_moe_tiles, zero_tile, 0)

        def wait_tile(k, carry):
            tile_copy(0).wait()
            return carry
        lax.fori_loop(nu_ref[0], n_moe_tiles, wait_tile, 0)


def _dispatch(h2e, pos, pad_start, pad_len, n_used, p_max, tile, moe_tile):
    n, w = h2e.shape
    n_steps = n // tile
    return pl.pallas_call(
        functools.partial(_dispatch_kernel, tile=tile, n_steps=n_steps, moe_tile=moe_tile,
                          n_moe_tiles=p_max // moe_tile),
        out_shape=jax.ShapeDtypeStruct((p_max, w), F32),
        grid_spec=pltpu.PrefetchScalarGridSpec(
            num_scalar_prefetch=4,
            grid=(n_steps,),
            in_specs=[pl.BlockSpec((tile, w), lambda t, *_: (t, 0))],
            out_specs=pl.BlockSpec(memory_space=pl.ANY),
            scratch_shapes=[pltpu.VMEM((moe_tile, w), F32), pltpu.SemaphoreType.DMA(())],
        ),
        compiler_params=_cparams("arbitrary"),
        name="dispatch",
    )(pos, pad_start, pad_len, n_used, h2e)


def _expert(xb, wgu_ref, wd_ref):
    gu = jnp.dot(xb, wgu_ref[0], preferred_element_type=F32)
    gate = gu[:, :D_EXPERT]
    he = (gate * jax.nn.sigmoid(gate)) * gu[:, D_EXPERT:]
    return jnp.dot(he.astype(BF16), wd_ref[0], preferred_element_type=F32)


def _moe_kernel(ea_ref, eb_ref, nu_ref, xs_ref, wgu_a_ref, wd_a_ref, wgu_b_ref, wd_b_ref, ys_ref):
    t = pl.program_id(0)

    @pl.when(t < nu_ref[0])
    def _():
        xb = xs_ref[:, 0:D_MODEL].astype(BF16)
        w_lo = xs_ref[:, D_MODEL:D_MODEL + 1]
        w_hi = xs_ref[:, D_MODEL + 1:D_MODEL + 2]
        ys_ref[...] = w_lo * _expert(xb, wgu_a_ref, wd_a_ref) + w_hi * _expert(xb, wgu_b_ref, wd_b_ref)

    @pl.when(t >= nu_ref[0])
    def _():
        ys_ref[...] = jnp.zeros_like(ys_ref)


def _moe_grouped(xs, tile_ea, tile_eb, n_used, wgu, wd, tile):
    p, w = xs.shape
    d = D_MODEL
    nt = p // tile
    last = lambda nu: jnp.maximum(nu[0] - 1, 0)
    return pl.pallas_call(
        _moe_kernel,
        out_shape=jax.ShapeDtypeStruct((p, d), F32),
        grid_spec=pltpu.PrefetchScalarGridSpec(
            num_scalar_prefetch=3,
            grid=(nt,),
            in_specs=[
                pl.BlockSpec((tile, w), lambda t, ea, eb, nu: (jnp.minimum(t, last(nu)), 0)),
                pl.BlockSpec((1, d, 2 * D_EXPERT), lambda t, ea, eb, nu: (ea[t], 0, 0)),
                pl.BlockSpec((1, D_EXPERT, d), lambda t, ea, eb, nu: (ea[t], 0, 0)),
                pl.BlockSpec((1, d, 2 * D_EXPERT), lambda t, ea, eb, nu: (eb[t], 0, 0)),
                pl.BlockSpec((1, D_EXPERT, d), lambda t, ea, eb, nu: (eb[t], 0, 0)),
            ],
            out_specs=pl.BlockSpec((tile, d), lambda t, ea, eb, nu: (t, 0)),
        ),
        compiler_params=_cparams("arbitrary"),
        name="moe_grouped",
    )(tile_ea, tile_eb, n_used, xs, wgu, wd, wgu, wd)


_PAIR_LO, _PAIR_HI = np.triu_indices(EXPERTS_PER_GROUP, k=1)
_CLASS_LO = np.concatenate([g * EXPERTS_PER_GROUP + _PAIR_LO for g in range(N_GROUPS)]).astype(np.int32)
_CLASS_HI = np.concatenate([g * EXPERTS_PER_GROUP + _PAIR_HI for g in range(N_GROUPS)]).astype(np.int32)


def _moe_plan(counts, cls, rank, n, tile):
    p_max = -(-(n + N_CLASSES * (tile - 1)) // tile) * tile
    nt = p_max // tile
    counts = counts[:N_CLASSES].astype(jnp.int32)
    padded = ((counts + tile - 1) // tile) * tile
    pad_end = jnp.cumsum(padded)
    gstart = pad_end - padded
    pos = gstart[cls] + rank
    n_used = (pad_end[-1] // tile).astype(jnp.int32).reshape(1)
    tile_cls = jnp.sum((pad_end[None, :] <= (jnp.arange(nt, dtype=jnp.int32) * tile)[:, None]).astype(jnp.int32), axis=1)
    tile_cls = jnp.minimum(tile_cls, N_CLASSES - 1)
    tile_ea = jnp.asarray(_CLASS_LO)[tile_cls]
    tile_eb = jnp.asarray(_CLASS_HI)[tile_cls]
    return p_max, pos, gstart + counts, padded - counts, tile_ea, tile_eb, n_used


def _final_kernel(pos_ref, x1_ref, ys_ref, mod_ref, g_ref, o_ref, ybuf, sems, *, tile, n_steps):
    s = pl.program_id(0)
    slot = s % 2

    def issue(step, dst_slot):
        def one(r, carry):
            pltpu.make_async_copy(ys_ref.at[pl.ds(pos_ref[step * tile + r], 1)],
                                  ybuf.at[dst_slot, pl.ds(r, 1)], sems.at[dst_slot]).start()
            return carry
        lax.fori_loop(0, tile, one, 0, unroll=8)

    @pl.when(s == 0)
    def _():
        issue(0, 0)

    @pl.when(s + 1 < n_steps)
    def _():
        issue(s + 1, 1 - slot)

    def wait_one(r, carry):
        pltpu.make_async_copy(ys_ref.at[pl.ds(0, 1)], ybuf.at[slot, pl.ds(0, 1)], sems.at[slot]).wait()
        return carry
    lax.fori_loop(0, tile, wait_one, 0)

    o_ref[...] = x1_ref[...] + mod_ref[0, 5:6, :] * _rms(ybuf[slot], g_ref[...])


def _final(x1, ys, pos, mods, g_post_ffn, tile):
    b, s, d = x1.shape
    n = b * s
    per_b = s // tile
    n_steps = n // tile
    tok = pl.BlockSpec((tile, d), lambda i, *_: (i, 0))
    out = pl.pallas_call(
        functools.partial(_final_kernel, tile=tile, n_steps=n_steps),
        out_shape=jax.ShapeDtypeStruct((n, d), F32),
        grid_spec=pltpu.PrefetchScalarGridSpec(
            num_scalar_prefetch=1,
            grid=(n_steps,),
            in_specs=[
                tok,
                pl.BlockSpec(memory_space=pl.ANY),
                pl.BlockSpec((1, 6, d), lambda i, *_: (i // per_b, 0, 0)),
                pl.BlockSpec((1, d), lambda i, *_: (0, 0)),
            ],
            out_specs=tok,
            scratch_shapes=[pltpu.VMEM((2, tile, d), F32), pltpu.SemaphoreType.DMA((2,))],
        ),
        compiler_params=_cparams("arbitrary"),
        name="final",
    )(pos, x1.reshape(n, d), ys, mods, g_post_ffn)
    return out.reshape(b, s, d)


def _t5_bucket(rel):
    half = T5_BUCKETS // 2
    exact = half // 2
    ret = jnp.where(rel > 0, half, 0)
    n = jnp.abs(rel)
    nf = jnp.maximum(n, 1).astype(F32)
    large = exact + (jnp.log(nf / exact) / math.log(T5_MAX_DIST / exact) * (half - exact)).astype(jnp.int32)
    large = jnp.minimum(large, half - 1)
    return ret + jnp.where(n < exact, n, large)


def _bias_tables(t5_table, rel_table):
    rel = jnp.arange(A_BAND)[None, :] - A_WINDOW - jnp.arange(CHUNK)[:, None]
    bias_a = jnp.transpose(t5_table[_t5_bucket(rel)], (2, 0, 1)).astype(F32)
    bias_a = bias_a.reshape(A_KV_HEADS, A_GROUP * CHUNK, A_BAND)
    dist = jnp.arange(CHUNK)[:, None] + B_REACH - jnp.arange(B_BAND)[None, :]
    bias_b = rel_table[:, jnp.clip(dist, -REL_CLIP, REL_CLIP) + REL_CLIP].astype(F32)
    return bias_a, bias_b


def _moe_and_final(x1, h2e, cls, rank, counts, mods, g_post_ffn, wgu, wd, seq_tile, moe_tile):
    b, s, d = x1.shape
    n = b * s
    p_max, pos, pad_start, pad_len, tile_ea, tile_eb, n_used = _moe_plan(
        counts.reshape(CLASS_ROWS), cls.reshape(n), rank.reshape(n), n, moe_tile)
    xs = _dispatch(h2e.reshape(n, ROW_W), pos, pad_start, pad_len, n_used, p_max, seq_tile, moe_tile)
    ys = _moe_grouped(xs, tile_ea, tile_eb, n_used, wgu, wd, moe_tile)
    return _final(x1, ys, pos, mods, g_post_ffn, seq_tile)


def kernel(x_prompt, x_sample, c_prompt, c_sample, cache_a_k, cache_a_v, cache_b_k, cache_b_v, w_ada, b_ada, g_pre_mix, g_post_mix, g_pre_ffn, g_post_ffn, w_in, a_sinks, t5_table, b_rel_table, w_proj_a, w_proj_b, w_gate, b_gate, w_o, w_route_g, b_route_g, w_route_e, b_route_e, w_e_gate, w_e_up, w_e_down):
    depth = w_in.shape[0]
    assert depth == 1
    l = 0
    bp, sp, d = x_prompt.shape
    bs, ts, _ = x_sample.shape

    mods = _ada(jnp.concatenate([c_prompt, c_sample], axis=0), w_ada[l], b_ada[l]).reshape(bp + bs, 6, d)
    mods_p, mods_s = mods[:bp], mods[bp:]

    w_in_bf = w_in[l].astype(BF16)
    wg, wpa, wpb, wo = w_gate[l].astype(BF16), w_proj_a[l].astype(BF16), w_proj_b[l].astype(BF16), w_o[l].astype(BF16)
    bg = b_gate[l].reshape(1, 2 * d)
    pad_rows = ROUTE_ROWS - N_EXPERTS - N_GROUPS
    wr = jnp.concatenate([w_route_e[l].T, w_route_g[l].T, jnp.zeros((pad_rows, d), F32)], axis=0).astype(BF16)
    br = jnp.concatenate([b_route_e[l], b_route_g[l], jnp.zeros((pad_rows,), F32)]).reshape(ROUTE_ROWS, 1)
    wgu = jnp.concatenate([w_e_gate[l], w_e_up[l]], axis=-1).astype(BF16)
    wd = w_e_down[l].astype(BF16)
    g1, g1p, g2, g2p = (g[l].reshape(1, d) for g in (g_pre_mix, g_post_mix, g_pre_ffn, g_post_ffn))

    bias_a, bias_b = _bias_tables(t5_table, b_rel_table[l])
    sinks = jnp.broadcast_to(a_sinks[l].reshape(A_KV_HEADS, A_GROUP, 1, 1), (A_KV_HEADS, A_GROUP, CHUNK, 1))
    sinks = sinks.reshape(A_KV_HEADS, A_GROUP * CHUNK, 1).astype(F32)

    qa, ka, va, qb, kb, vb, sak_p, sav_p, sbk_p, sbv_p = _pre_prompt(x_prompt, mods_p, g1, w_in_bf)
    oa, ob = _attn_prompt(qa, ka, va, qb, kb, vb, bias_a, sinks, bias_b)
    x1, h2e, cls, rank, counts = _post(x_prompt, oa, ob, mods_p, g1, g1p, g2, wg, bg, wpa, wpb, wo, wr, br, SEQ_TILE)
    y_prompt = _moe_and_final(x1, h2e, cls, rank, counts, mods_p, g2p, wgu, wd, SEQ_TILE, 256)

    la, lb = cache_a_k.shape[2], cache_b_k.shape[2]
    cak = cache_a_k[l].reshape(bs, la, KVA_W)
    cav = cache_a_v[l].reshape(bs, la, KVA_W)
    cbk = cache_b_k[l].reshape(bs, lb, B_W)
    cbv = cache_b_v[l].reshape(bs, lb, B_W)
    qa, ka, va, qb, kb, vb, sak_s, sav_s, sbk_s, sbv_s = _pre_sample(x_sample, mods_s, g1, w_in_bf, cak, cav, cbk, cbv)
    oa, ob = _attn_sample(qa, ka, va, qb, kb, vb, bias_a, sinks, bias_b)
    x1, h2e, cls, rank, counts = _post(x_sample, oa, ob, mods_s, g1, g1p, g2, wg, bg, wpa, wpb, wo, wr, br, ts)
    y_sample = _moe_and_final(x1, h2e, cls, rank, counts, mods_s, g2p, wgu, wd, ts, 64)

    a_state = lambda v, b, r: v.reshape(1, b, r, A_KV_HEADS, HEAD_DIM)
    b_state = lambda v, b, r: v.reshape(1, b, r, B_HEADS, HEAD_DIM)
    return (y_prompt, y_sample,
            a_state(sak_p, bp, A_WINDOW), a_state(sav_p, bp, A_WINDOW),
            b_state(sbk_p, bp, B_REACH), b_state(sbv_p, bp, B_REACH),
            a_state(sak_s, bs, la), a_state(sav_s, bs, la),
            b_state(sbk_s, bs, lb), b_state(sbv_s, bs, lb))
```

```python
import functools
import math

import numpy as np
import jax
import jax.numpy as jnp
from jax import lax
from jax.experimental import pallas as pl
from jax.experimental.pallas import tpu as pltpu

D_MODEL = 1024
CHUNK = 64
HEAD_DIM = 64
A_Q_HEADS = 8
A_KV_HEADS = 2
A_GROUP = A_Q_HEADS // A_KV_HEADS
A_WINDOW = 128
A_BACK = A_WINDOW // CHUNK
B_HEADS = 8
B_BACK = 8
B_REACH = B_BACK * CHUNK
REL_CLIP = 256
T5_BUCKETS = 32
T5_MAX_DIST = 128
N_GROUPS = 4
EXPERTS_PER_GROUP = 8
N_EXPERTS = N_GROUPS * EXPERTS_PER_GROUP
D_EXPERT = D_MODEL // 4
EPS = 1e-6

QA_W = A_Q_HEADS * HEAD_DIM
KVA_W = A_KV_HEADS * HEAD_DIM
B_W = B_HEADS * HEAD_DIM
IN_W = QA_W + 2 * KVA_W + 3 * B_W
A_BAND = A_WINDOW + CHUNK
B_BAND = B_REACH + CHUNK

PAIRS_PER_GROUP = EXPERTS_PER_GROUP * (EXPERTS_PER_GROUP - 1) // 2
N_CLASSES = N_GROUPS * PAIRS_PER_GROUP
ROUTE_ROWS = 40
CLASS_ROWS = 128
ROW_W = D_MODEL + 128

F32 = jnp.float32
BF16 = jnp.bfloat16

VMEM_LIMIT_BYTES = 56 * 1024 * 1024

SEQ_TILE = 512
ISSUE_UNROLL = 8


def _cparams(*sem):
    return pltpu.CompilerParams(dimension_semantics=sem, vmem_limit_bytes=VMEM_LIMIT_BYTES)


def _norm_mod(x, g, scale, shift):
    y = x * lax.rsqrt(jnp.mean(x * x, axis=-1, keepdims=True) + EPS)
    return (y * g) * (1.0 + scale) + shift


def _rms(x, g):
    return (x * lax.rsqrt(jnp.mean(x * x, axis=-1, keepdims=True) + EPS)) * g


def _ada_kernel(c_ref, w_ref, b_ref, o_ref):
    c = c_ref[...]
    s = (c * jax.nn.sigmoid(c)).astype(BF16)
    o_ref[...] = jnp.dot(s, w_ref[...].astype(BF16), preferred_element_type=F32) + b_ref[...]


def _ada(c, w_ada, b_ada):
    n, d = c.shape
    wn = w_ada.shape[1]
    tn = 512
    return pl.pallas_call(
        _ada_kernel,
        out_shape=jax.ShapeDtypeStruct((n, wn), F32),
        grid=(wn // tn,),
        in_specs=[
            pl.BlockSpec((n, d), lambda j: (0, 0)),
            pl.BlockSpec((d, tn), lambda j: (0, j)),
            pl.BlockSpec((1, tn), lambda j: (0, j)),
        ],
        out_specs=pl.BlockSpec((n, tn), lambda j: (0, j)),
        compiler_params=_cparams("arbitrary"),
        name="ada",
    )(c, w_ada, b_ada.reshape(1, wn))


_COL_QA = (0, QA_W)
_COL_KA = (QA_W, QA_W + KVA_W)
_COL_VA = (QA_W + KVA_W, QA_W + 2 * KVA_W)
_COL_QB = (QA_W + 2 * KVA_W, QA_W + 2 * KVA_W + B_W)
_COL_KB = (_COL_QB[1], _COL_QB[1] + B_W)
_COL_VB = (_COL_KB[1], _COL_KB[1] + B_W)
Q_SCALE = HEAD_DIM ** -0.5


def _project(x_ref, mod_ref, g_ref, w_ref):
    h = _norm_mod(x_ref[0], g_ref[...], mod_ref[0, 1:2, :], mod_ref[0, 0:1, :])
    return jnp.dot(h.astype(BF16), w_ref[...], preferred_element_type=F32)


def _cols(p, c):
    return p[:, c[0]:c[1]]


def _pre_prompt_kernel(x_ref, mod_ref, g_ref, w_ref,
                       qa_ref, ka_ref, va_ref, qb_ref, kb_ref, vb_ref,
                       sak_ref, sav_ref, sbk_ref, sbv_ref, *, n_tiles, tile):
    p = _project(x_ref, mod_ref, g_ref, w_ref)
    qa_ref[0] = (_cols(p, _COL_QA) * Q_SCALE).astype(BF16)
    ka_ref[0] = _cols(p, _COL_KA).astype(BF16)
    va_ref[0] = _cols(p, _COL_VA).astype(BF16)
    qb_ref[0] = (_cols(p, _COL_QB) * Q_SCALE).astype(BF16)
    kb_ref[0] = _cols(p, _COL_KB).astype(BF16)
    vb_ref[0] = _cols(p, _COL_VB).astype(BF16)

    @pl.when(pl.program_id(1) == n_tiles - 1)
    def _():
        sak_ref[0] = _cols(p, _COL_KA)[tile - A_WINDOW:, :]
        sav_ref[0] = _cols(p, _COL_VA)[tile - A_WINDOW:, :]
        sbk_ref[0] = _cols(p, _COL_KB)[tile - B_REACH:, :]
        sbv_ref[0] = _cols(p, _COL_VB)[tile - B_REACH:, :]


def _pre_prompt(x, mods, g_pre, w_in_bf):
    b, s, d = x.shape
    tile = SEQ_TILE
    assert s % tile == 0 and tile >= B_REACH and s >= B_REACH
    nt = s // tile
    tok = lambda w: pl.BlockSpec((1, tile, w), lambda i, j: (i, j, 0))
    state = lambda r, w: pl.BlockSpec((1, r, w), lambda i, j: (i, 0, 0))
    return pl.pallas_call(
        functools.partial(_pre_prompt_kernel, n_tiles=nt, tile=tile),
        out_shape=(
            jax.ShapeDtypeStruct((b, s, QA_W), BF16),
            jax.ShapeDtypeStruct((b, s, KVA_W), BF16),
            jax.ShapeDtypeStruct((b, s, KVA_W), BF16),
            jax.ShapeDtypeStruct((b, s, B_W), BF16),
            jax.ShapeDtypeStruct((b, s, B_W), BF16),
            jax.ShapeDtypeStruct((b, s, B_W), BF16),
            jax.ShapeDtypeStruct((b, A_WINDOW, KVA_W), F32),
            jax.ShapeDtypeStruct((b, A_WINDOW, KVA_W), F32),
            jax.ShapeDtypeStruct((b, B_REACH, B_W), F32),
            jax.ShapeDtypeStruct((b, B_REACH, B_W), F32),
        ),
        grid=(b, nt),
        in_specs=[
            tok(d),
            pl.BlockSpec((1, 6, d), lambda i, j: (i, 0, 0)),
            pl.BlockSpec((1, d), lambda i, j: (0, 0)),
            pl.BlockSpec((d, IN_W), lambda i, j: (0, 0)),
        ],
        out_specs=(
            tok(QA_W), tok(KVA_W), tok(KVA_W), tok(B_W), tok(B_W), tok(B_W),
            state(A_WINDOW, KVA_W), state(A_WINDOW, KVA_W), state(B_REACH, B_W), state(B_REACH, B_W),
        ),
        compiler_params=_cparams("parallel", "arbitrary"),
        name="pre_prompt",
    )(x, mods, g_pre, w_in_bf)


def _pre_sample_kernel(x_ref, mod_ref, g_ref, w_ref, cak_ref, cav_ref, cbk_ref, cbv_ref,
                       qa_ref, ka_ref, va_ref, qb_ref, kb_ref, vb_ref,
                       sak_ref, sav_ref, sbk_ref, sbv_ref, *, t, la, lb):
    p = _project(x_ref, mod_ref, g_ref, w_ref)
    qa_ref[0] = (_cols(p, _COL_QA) * Q_SCALE).astype(BF16)
    qb_ref[0] = (_cols(p, _COL_QB) * Q_SCALE).astype(BF16)
    for cache_ref, band_ref, state_ref, col, hist in (
            (cak_ref, ka_ref, sak_ref, _COL_KA, la), (cav_ref, va_ref, sav_ref, _COL_VA, la),
            (cbk_ref, kb_ref, sbk_ref, _COL_KB, lb), (cbv_ref, vb_ref, sbv_ref, _COL_VB, lb)):
        new = _cols(p, col)
        cache = cache_ref[0]
        band_ref[0, 0:hist, :] = cache.astype(BF16)
        band_ref[0, hist:hist + t, :] = new.astype(BF16)
        state_ref[0, 0:hist - t, :] = cache[t:, :]
        state_ref[0, hist - t:hist, :] = new


def _pre_sample(x, mods, g_pre, w_in_bf, cak, cav, cbk, cbv):
    b, t, d = x.shape
    la, lb = cak.shape[1], cbk.shape[1]
    assert t == CHUNK and la == A_WINDOW and lb == B_REACH
    per_b = lambda r, w: pl.BlockSpec((1, r, w), lambda i: (i, 0, 0))
    return pl.pallas_call(
        functools.partial(_pre_sample_kernel, t=t, la=la, lb=lb),
        out_shape=(
            jax.ShapeDtypeStruct((b, t, QA_W), BF16),
            jax.ShapeDtypeStruct((b, la + t, KVA_W), BF16),
            jax.ShapeDtypeStruct((b, la + t, KVA_W), BF16),
            jax.ShapeDtypeStruct((b, t, B_W), BF16),
            jax.ShapeDtypeStruct((b, lb + t, B_W), BF16),
            jax.ShapeDtypeStruct((b, lb + t, B_W), BF16),
            jax.ShapeDtypeStruct((b, la, KVA_W), F32),
            jax.ShapeDtypeStruct((b, la, KVA_W), F32),
            jax.ShapeDtypeStruct((b, lb, B_W), F32),
            jax.ShapeDtypeStruct((b, lb, B_W), F32),
        ),
        grid=(b,),
        in_specs=[
            per_b(t, d),
            per_b(6, d),
            pl.BlockSpec((1, d), lambda i: (0, 0)),
            pl.BlockSpec((d, IN_W), lambda i: (0, 0)),
            per_b(la, KVA_W), per_b(la, KVA_W), per_b(lb, B_W), per_b(lb, B_W),
        ],
        out_specs=(
            per_b(t, QA_W), per_b(la + t, KVA_W), per_b(la + t, KVA_W),
            per_b(t, B_W), per_b(lb + t, B_W), per_b(lb + t, B_W),
            per_b(la, KVA_W), per_b(la, KVA_W), per_b(lb, B_W), per_b(lb, B_W),
        ),
        compiler_params=_cparams("parallel"),
        name="pre_sample",
    )(x, mods, g_pre, w_in_bf, cak, cav, cbk, cbv)


def _attend_a(q, k, v, bias_ref, sink_ref, nk):
    boff = A_BAND - nk
    outs = []
    for kv in range(A_KV_HEADS):
        qs = jnp.concatenate(
            [q[:, (kv * A_GROUP + g) * HEAD_DIM:(kv * A_GROUP + g + 1) * HEAD_DIM] for g in range(A_GROUP)], axis=0)
        kh = k[:, kv * HEAD_DIM:(kv + 1) * HEAD_DIM]
        vh = v[:, kv * HEAD_DIM:(kv + 1) * HEAD_DIM]
        s = lax.dot_general(qs, kh, (((1,), (1,)), ((), ())), preferred_element_type=F32)
        s = s + bias_ref[kv, :, boff:boff + nk]
        sk = sink_ref[kv]
        mx = jnp.maximum(jnp.max(s, axis=-1, keepdims=True), sk)
        e = jnp.exp(s - mx)
        den = jnp.sum(e, axis=-1, keepdims=True) + jnp.exp(sk - mx)
        o = jnp.dot(e.astype(BF16), vh, preferred_element_type=F32) / den
        outs.extend(o[g * CHUNK:(g + 1) * CHUNK, :] for g in range(A_GROUP))
    return jnp.concatenate(outs, axis=1).astype(BF16)


def _attend_b(q, k, v, bias_ref, nk):
    boff = B_BAND - nk
    outs = []
    for h in range(B_HEADS):
        sl = slice(h * HEAD_DIM, (h + 1) * HEAD_DIM)
        s = lax.dot_general(q[:, sl], k[:, sl], (((1,), (1,)), ((), ())), preferred_element_type=F32)
        s = s + bias_ref[h, :, boff:boff + nk]
        mx = jnp.max(s, axis=-1, keepdims=True)
        e = jnp.exp(s - mx)
        den = jnp.sum(e, axis=-1, keepdims=True)
        outs.append(jnp.dot(e.astype(BF16), v[:, sl], preferred_element_type=F32) / den)
    return jnp.concatenate(outs, axis=1).astype(BF16)


MASKED = -1e30
PAIR_W = 2 * HEAD_DIM
ATT_BLOCK = 2 * CHUNK
A_COLS = A_WINDOW + ATT_BLOCK
B_COLS = B_REACH + ATT_BLOCK


def _pair_attend(q2, k2, v2, bias, sinks, first_valid):
    c = k2.shape[0] // 2
    s = lax.dot_general(q2, k2, (((1,), (1,)), ((), ())), preferred_element_type=F32) + bias
    es, dens = [], []
    for half in range(2):
        sh = s[:, half * c:(half + 1) * c]
        if first_valid > 0:
            sh = jnp.where(lax.broadcasted_iota(jnp.int32, sh.shape, 1) >= first_valid, sh, MASKED)
        mx = jnp.max(sh, axis=-1, keepdims=True)
        if sinks is not None:
            mx = jnp.maximum(mx, sinks[half])
        e = jnp.exp(sh - mx)
        den = jnp.sum(e, axis=-1, keepdims=True)
        if sinks is not None:
            den = den + jnp.exp(sinks[half] - mx)
        es.append(e.astype(BF16))
        dens.append(den)
    o = jnp.dot(jnp.concatenate(es, axis=1), v2, preferred_element_type=F32)
    lane = lax.broadcasted_iota(jnp.int32, o.shape, 1)
    return o / jnp.where(lane < HEAD_DIM, dens[0], dens[1])


def _attn_prompt_kernel(qa_ref, ka_ref, va_ref, qb_ref, kb_ref, vb_ref, ba_ref, sk_ref, bb_ref, lo_ref, hi_ref,
                        oa_ref, ob_ref, ak_ref, av_ref, bk_ref, bv_ref, *, tile):
    j = pl.program_id(1)

    @pl.when(j == 0)
    def _():
        lo, hi = lo_ref[:, 0:PAIR_W], hi_ref[:, 0:PAIR_W]
        for src, dst in ((ka_ref, ak_ref), (va_ref, av_ref)):
            dst[:, 0:A_WINDOW, :] = jnp.zeros((4, A_WINDOW, PAIR_W), BF16)
            x = src[0]
            swapped = jnp.concatenate([x[:, HEAD_DIM:], x[:, :HEAD_DIM]], axis=1)
            dst[0, A_WINDOW:, :] = x * lo
            dst[1, A_WINDOW:, :] = swapped * hi
            dst[2, A_WINDOW:, :] = swapped * lo
            dst[3, A_WINDOW:, :] = x * hi
        for src, dst in ((kb_ref, bk_ref), (vb_ref, bv_ref)):
            dst[:, 0:B_REACH, :] = jnp.zeros((2, B_REACH, B_W), BF16)
            x = src[0]
            dst[0, B_REACH:, :] = x * lo_ref[...]
            dst[1, B_REACH:, :] = x * hi_ref[...]

    def block(g, r0, first_a, first_b):
        rows = pl.ds(r0, ATT_BLOCK)
        band_a = pl.ds(g, A_COLS)
        band_b = pl.ds(g, B_COLS)
        for kv in range(A_KV_HEADS):
            c0, c1 = slice(2 * kv * PAIR_W, (2 * kv + 1) * PAIR_W), slice((2 * kv + 1) * PAIR_W, (2 * kv + 2) * PAIR_W)
            q2 = jnp.concatenate([qa_ref[0, rows, c0], qa_ref[0, rows, c1]], axis=0)
            k2 = jnp.concatenate([ak_ref[2 * kv, band_a, :], ak_ref[2 * kv + 1, band_a, :]], axis=0)
            v2 = jnp.concatenate([av_ref[2 * kv, band_a, :], av_ref[2 * kv + 1, band_a, :]], axis=0)
            o = _pair_attend(q2, k2, v2, ba_ref[kv], (sk_ref[kv, 0], sk_ref[kv, 1]), first_a)
            oa_ref[0, rows, c0] = o[0:ATT_BLOCK].astype(BF16)
            oa_ref[0, rows, c1] = o[ATT_BLOCK:].astype(BF16)
        for p in range(B_HEADS // 2):
            cols = slice(p * PAIR_W, (p + 1) * PAIR_W)
            k2 = jnp.concatenate([bk_ref[0, band_b, cols], bk_ref[1, band_b, cols]], axis=0)
            v2 = jnp.concatenate([bv_ref[0, band_b, cols], bv_ref[1, band_b, cols]], axis=0)
            o = _pair_attend(qb_ref[0, rows, cols], k2, v2, bb_ref[p], None, first_b)
            ob_ref[0, rows, cols] = o.astype(BF16)

    n_blocks = tile // ATT_BLOCK

    @pl.when(j == 0)
    def _():
        for blk in range(n_blocks):
            g = blk * ATT_BLOCK
            block(g, g, max(A_WINDOW - g, 0), max(B_REACH - g, 0))

    @pl.when(j > 0)
    def _():
        def body(blk, carry):
            r0 = pl.multiple_of(blk * ATT_BLOCK, ATT_BLOCK)
            block(pl.multiple_of(j * tile + r0, ATT_BLOCK), r0, 0, 0)
            return carry
        lax.fori_loop(0, n_blocks, body, 0)


def _attn_prompt(qa, ka, va, qb, kb, vb, bias2_a, sinks2, bias2_b):
    b, s, _ = qa.shape
    tile = SEQ_TILE
    assert s % tile == 0 and tile % ATT_BLOCK == 0
    lane = np.arange(B_W) % PAIR_W
    lo = jnp.asarray((lane < HEAD_DIM).astype(np.float32).reshape(1, B_W), BF16)
    hi = jnp.asarray((lane >= HEAD_DIM).astype(np.float32).reshape(1, B_W), BF16)
    tok = lambda w: pl.BlockSpec((1, tile, w), lambda i, j: (i, j, 0))
    seq = lambda w: pl.BlockSpec((1, s, w), lambda i, j: (i, 0, 0))
    const = lambda shp: pl.BlockSpec(shp, lambda i, j: (0,) * len(shp))
    return pl.pallas_call(
        functools.partial(_attn_prompt_kernel, tile=tile),
        out_shape=(jax.ShapeDtypeStruct((b, s, QA_W), BF16), jax.ShapeDtypeStruct((b, s, B_W), BF16)),
        grid=(b, s // tile),
        in_specs=[tok(QA_W), seq(KVA_W), seq(KVA_W), tok(B_W), seq(B_W), seq(B_W),
                  const(bias2_a.shape), const(sinks2.shape), const(bias2_b.shape), const(lo.shape), const(hi.shape)],
        out_specs=(tok(QA_W), tok(B_W)),
        scratch_shapes=[pltpu.VMEM((4, A_WINDOW + s, PAIR_W), BF16), pltpu.VMEM((4, A_WINDOW + s, PAIR_W), BF16),
                        pltpu.VMEM((2, B_REACH + s, B_W), BF16), pltpu.VMEM((2, B_REACH + s, B_W), BF16)],
        compiler_params=_cparams("arbitrary", "arbitrary"),
        name="attn_prompt",
    )(qa, ka, va, qb, kb, vb, bias2_a, sinks2, bias2_b, lo, hi)


def _attn_sample_kernel(qa_ref, ka_ref, va_ref, qb_ref, kb_ref, vb_ref, ba_ref, sk_ref, bb_ref, oa_ref, ob_ref):
    oa_ref[0] = _attend_a(qa_ref[0], ka_ref[0], va_ref[0], ba_ref, sk_ref, A_BAND)
    ob_ref[0] = _attend_b(qb_ref[0], kb_ref[0], vb_ref[0], bb_ref, B_BAND)


def _attn_sample(qa, ka, va, qb, kb, vb, bias_a, sinks, bias_b):
    b, t, _ = qa.shape
    per_b = lambda r, w: pl.BlockSpec((1, r, w), lambda i: (i, 0, 0))
    const = lambda shp: pl.BlockSpec(shp, lambda i: (0,) * len(shp))
    return pl.pallas_call(
        _attn_sample_kernel,
        out_shape=(jax.ShapeDtypeStruct((b, t, QA_W), BF16), jax.ShapeDtypeStruct((b, t, B_W), BF16)),
        grid=(b,),
        in_specs=[per_b(t, QA_W), per_b(A_BAND, KVA_W), per_b(A_BAND, KVA_W),
                  per_b(t, B_W), per_b(B_BAND, B_W), per_b(B_BAND, B_W),
                  const(bias_a.shape), const(sinks.shape), const(bias_b.shape)],
        out_specs=(per_b(t, QA_W), per_b(t, B_W)),
        compiler_params=_cparams("parallel"),
        name="attn_sample",
    )(qa, ka, va, qb, kb, vb, bias_a, sinks, bias_b)


def _route(lt):
    t = lt.shape[1]
    el = lt[0:N_EXPERTS]
    gl = lt[N_EXPERTS:N_EXPERTS + N_GROUPS]
    gmax = jnp.max(gl, axis=0, keepdims=True)
    gi = lax.broadcasted_iota(jnp.int32, (N_GROUPS, t), 0)
    gidx = jnp.min(jnp.where(gl == gmax, gi, N_GROUPS), axis=0, keepdims=True)
    g_w = 1.0 / jnp.sum(jnp.exp(gl - gmax), axis=0, keepdims=True)
    e_sel = el[(N_GROUPS - 1) * EXPERTS_PER_GROUP:]
    for g in range(N_GROUPS - 2, -1, -1):
        e_sel = jnp.where(gidx == g, el[g * EXPERTS_PER_GROUP:(g + 1) * EXPERTS_PER_GROUP], e_sel)
    ei = lax.broadcasted_iota(jnp.int32, (EXPERTS_PER_GROUP, t), 0)
    m1 = jnp.max(e_sel, axis=0, keepdims=True)
    i1 = jnp.min(jnp.where(e_sel == m1, ei, EXPERTS_PER_GROUP), axis=0, keepdims=True)
    rest = jnp.where(ei == i1, -jnp.inf, e_sel)
    m2 = jnp.max(rest, axis=0, keepdims=True)
    i2 = jnp.min(jnp.where(rest == m2, ei, EXPERTS_PER_GROUP), axis=0, keepdims=True)
    ex = jnp.exp(m2 - m1)
    den = 1.0 + ex
    w1 = g_w * (1.0 / den)
    w2 = g_w * (ex / den)
    lo = jnp.minimum(i1, i2)
    hi = jnp.maximum(i1, i2)
    first_is_lo = i1 < i2
    w_lo = jnp.where(first_is_lo, w1, w2)
    w_hi = jnp.where(first_is_lo, w2, w1)
    pair = ((lo * (2 * EXPERTS_PER_GROUP - 1 - lo)) >> 1) + (hi - lo - 1)
    return gidx * PAIRS_PER_GROUP + pair, w_lo, w_hi


def _col_from_row(w_row, n):
    ri = lax.broadcasted_iota(jnp.int32, (n, n), 0)
    ci = lax.broadcasted_iota(jnp.int32, (n, n), 1)
    return jnp.sum(jnp.where(ri == ci, jnp.broadcast_to(w_row, (n, n)), 0.0), axis=1, keepdims=True)


def _rank_in_class(cls, run_ref, tri_ref):
    t = cls.shape[1]
    onehot = lax.broadcasted_iota(jnp.int32, (CLASS_ROWS, t), 0) == cls
    ones = jnp.where(onehot, 1.0, 0.0)
    before = jnp.dot(ones.astype(BF16), tri_ref[...], preferred_element_type=F32)
    run = run_ref[...]
    rank = jnp.sum(jnp.where(onehot, before + run, 0.0), axis=0, keepdims=True)
    run_ref[...] = run + jnp.sum(ones, axis=1, keepdims=True)
    return rank.astype(jnp.int32)


def _post_kernel(x_ref, oa_ref, ob_ref, mod_ref, g1_ref, g1p_ref, g2_ref,
                 wg_ref, bg_ref, wpa_ref, wpb_ref, wo_ref, wr_ref, br_ref, tri_ref,
                 x1_ref, h2_ref, cls_ref, rank_ref, cnt_ref, run_ref):
    @pl.when((pl.program_id(0) == 0) & (pl.program_id(1) == 0))
    def _():
        run_ref[...] = jnp.zeros_like(run_ref)

    x = x_ref[0]
    h = _norm_mod(x, g1_ref[...], mod_ref[0, 1:2, :], mod_ref[0, 0:1, :]).astype(BF16)
    gates = jax.nn.sigmoid(jnp.dot(h, wg_ref[...], preferred_element_type=F32) + bg_ref[...])
    pa = jnp.dot(oa_ref[0], wpa_ref[...], preferred_element_type=F32)
    pb = jnp.dot(ob_ref[0], wpb_ref[...], preferred_element_type=F32)
    mixed = gates[:, :D_MODEL] * pa + gates[:, D_MODEL:] * pb
    y = jnp.dot(mixed.astype(BF16), wo_ref[...], preferred_element_type=F32)
    x1 = x + mod_ref[0, 2:3, :] * _rms(y, g1p_ref[...])
    x1_ref[0] = x1
    h2 = _norm_mod(x1, g2_ref[...], mod_ref[0, 4:5, :], mod_ref[0, 3:4, :]).astype(BF16)
    lt = lax.dot_general(wr_ref[...], h2, (((1,), (1,)), ((), ())), preferred_element_type=F32) + br_ref[...]
    cls, w_lo, w_hi = _route(lt)
    cls_ref[0, 0] = cls
    rank_ref[0, 0] = _rank_in_class(cls, run_ref, tri_ref)
    cnt_ref[...] = run_ref[...]
    t = x.shape[0]
    lane = lax.broadcasted_iota(jnp.int32, (t, ROW_W - D_MODEL), 1)
    h2_ref[0, :, 0:D_MODEL] = h2.astype(F32)
    h2_ref[0, :, D_MODEL:ROW_W] = jnp.where(lane == 0, _col_from_row(w_lo, t),
                                            jnp.where(lane == 1, _col_from_row(w_hi, t), 0.0))


def _post(x, oa, ob, mods, g_pre_mix, g_post_mix, g_pre_ffn, wg, bg, wpa, wpb, wo, wr, br, tile):
    b, s, d = x.shape
    nt = s // tile
    tri = (jnp.arange(tile)[:, None] < jnp.arange(tile)[None, :]).astype(BF16)
    tok = lambda w: pl.BlockSpec((1, tile, w), lambda i, j: (i, j, 0))
    const = lambda shp: pl.BlockSpec(shp, lambda i, j: (0,) * len(shp))
    per_tile = pl.BlockSpec((1, 1, 1, tile), lambda i, j: (i, j, 0, 0))
    return pl.pallas_call(
        _post_kernel,
        out_shape=(
            jax.ShapeDtypeStruct((b, s, d), F32),
            jax.ShapeDtypeStruct((b, s, ROW_W), F32),
            jax.ShapeDtypeStruct((b, nt, 1, tile), jnp.int32),
            jax.ShapeDtypeStruct((b, nt, 1, tile), jnp.int32),
            jax.ShapeDtypeStruct((CLASS_ROWS, 1), F32),
        ),
        grid=(b, nt),
        in_specs=[
            tok(d), tok(QA_W), tok(B_W),
            pl.BlockSpec((1, 6, d), lambda i, j: (i, 0, 0)),
            const((1, d)), const((1, d)), const((1, d)),
            const(wg.shape), const(bg.shape), const(wpa.shape), const(wpb.shape), const(wo.shape),
            const(wr.shape), const(br.shape), const(tri.shape),
        ],
        out_specs=(tok(d), tok(ROW_W), per_tile, per_tile, const((CLASS_ROWS, 1))),
        scratch_shapes=[pltpu.VMEM((CLASS_ROWS, 1), F32)],
        compiler_params=_cparams("arbitrary", "arbitrary"),
        name="post",
    )(x, oa, ob, mods, g_pre_mix, g_post_mix, g_pre_ffn, wg, bg, wpa, wpb, wo, wr, br, tri)


def _dispatch_kernel(pos_ref, pad_start_ref, pad_len_ref, nu_ref, h_ref, xs_ref, zero_ref, sem, *,
                     tile, n_steps, moe_tile, n_moe_tiles):
    t = pl.program_id(0)

    def wait_rows(n):
        def wait_one(r, carry):
            pltpu.make_async_copy(h_ref.at[pl.ds(0, 1)], xs_ref.at[pl.ds(0, 1)], sem).wait()
            return carry
        lax.fori_loop(0, n, wait_one, 0)

    def issue(r8, carry):
        for u in range(ISSUE_UNROLL):
            r = r8 * ISSUE_UNROLL + u
            pltpu.make_async_copy(h_ref.at[pl.ds(r, 1)], xs_ref.at[pl.ds(pos_ref[t * tile + r], 1)], sem).start(
                priority=u % 2)
        return carry
    lax.fori_loop(0, tile // ISSUE_UNROLL, issue, 0)
    wait_rows(tile)

    @pl.when(t == n_steps - 1)
    def _():
        zero_ref[...] = jnp.zeros_like(zero_ref)

        def per_class(c, carry):
            start, n = pad_start_ref[c], pad_len_ref[c]

            def fill(r, inner):
                pltpu.make_async_copy(zero_ref.at[pl.ds(0, 1)], xs_ref.at[pl.ds(start + r, 1)], sem).start()
                return inner
            lax.fori_loop(0, n, fill, 0)
            wait_rows(n)
            return carry
        lax.fori_loop(0, N_CLASSES, per_class, 0)

        def tile_copy(k):
            return pltpu.make_async_copy(zero_ref, xs_ref.at[pl.ds(k * moe_tile, moe_tile)], sem)

        def zero_tile(k, carry):
            tile_copy(k).start()
            return carry
        lax.fori_loop(nu_ref[0], n_moe_tiles, zero_tile, 0)

        def wait_tile(k, carry):
            tile_copy(0).wait()
            return carry
        lax.fori_loop(nu_ref[0], n_moe_tiles, wait_tile, 0)


def _dispatch(h2e, pos, pad_start, pad_len, n_used, p_max, tile, moe_tile):
    n, w = h2e.shape
    n_steps = n // tile
    return pl.pallas_call(
        functools.partial(_dispatch_kernel, tile=tile, n_steps=n_steps, moe_tile=moe_tile,
                          n_moe_tiles=p_max // moe_tile),
        out_shape=jax.ShapeDtypeStruct((p_max, w), F32),
        grid_spec=pltpu.PrefetchScalarGridSpec(
            num_scalar_prefetch=4,
            grid=(n_steps,),
            in_specs=[pl.BlockSpec((tile, w), lambda t, *_: (t, 0))],
            out_specs=pl.BlockSpec(memory_space=pl.ANY),
            scratch_shapes=[pltpu.VMEM((moe_tile, w), F32), pltpu.SemaphoreType.DMA(())],
        ),
        compiler_params=_cparams("arbitrary"),
        name="dispatch",
    )(pos, pad_start, pad_len, n_used, h2e)


def _expert(xb, wgu_ref, wd_ref):
    gu = jnp.dot(xb, wgu_ref[0], preferred_element_type=F32)
    gate = gu[:, :D_EXPERT]
    he = (gate * jax.nn.sigmoid(gate)) * gu[:, D_EXPERT:]
    return jnp.dot(he.astype(BF16), wd_ref[0], preferred_element_type=F32)


def _moe_kernel(ea_ref, eb_ref, nu_ref, xs_ref, wgu_a_ref, wd_a_ref, wgu_b_ref, wd_b_ref, ys_ref):
    t = pl.program_id(0)

    @pl.when(t < nu_ref[0])
    def _():
        xb = xs_ref[:, 0:D_MODEL].astype(BF16)
        w_lo = xs_ref[:, D_MODEL:D_MODEL + 1]
        w_hi = xs_ref[:, D_MODEL + 1:D_MODEL + 2]
        ys_ref[...] = w_lo * _expert(xb, wgu_a_ref, wd_a_ref) + w_hi * _expert(xb, wgu_b_ref, wd_b_ref)

    @pl.when(t >= nu_ref[0])
    def _():
        ys_ref[...] = jnp.zeros_like(ys_ref)


def _moe_grouped(xs, tile_ea, tile_eb, n_used, wgu, wd, tile):
    p, w = xs.shape
    d = D_MODEL
    nt = p // tile
    last = lambda nu: jnp.maximum(nu[0] - 1, 0)
    return pl.pallas_call(
        _moe_kernel,
        out_shape=jax.ShapeDtypeStruct((p, d), F32),
        grid_spec=pltpu.PrefetchScalarGridSpec(
            num_scalar_prefetch=3,
            grid=(nt,),
            in_specs=[
                pl.BlockSpec((tile, w), lambda t, ea, eb, nu: (jnp.minimum(t, last(nu)), 0)),
                pl.BlockSpec((1, d, 2 * D_EXPERT), lambda t, ea, eb, nu: (ea[t], 0, 0)),
                pl.BlockSpec((1, D_EXPERT, d), lambda t, ea, eb, nu: (ea[t], 0, 0)),
                pl.BlockSpec((1, d, 2 * D_EXPERT), lambda t, ea, eb, nu: (eb[t], 0, 0)),
                pl.BlockSpec((1, D_EXPERT, d), lambda t, ea, eb, nu: (eb[t], 0, 0)),
            ],
            out_specs=pl.BlockSpec((tile, d), lambda t, ea, eb, nu: (t, 0)),
        ),
        compiler_params=_cparams("arbitrary"),
        name="moe_grouped",
    )(tile_ea, tile_eb, n_used, xs, wgu, wd, wgu, wd)


_PAIR_LO, _PAIR_HI = np.triu_indices(EXPERTS_PER_GROUP, k=1)
_CLASS_LO = np.concatenate([g * EXPERTS_PER_GROUP + _PAIR_LO for g in range(N_GROUPS)]).astype(np.int32)
_CLASS_HI = np.concatenate([g * EXPERTS_PER_GROUP + _PAIR_HI for g in range(N_GROUPS)]).astype(np.int32)


def _moe_plan(counts, cls, rank, n, tile):
    p_max = -(-(n + N_CLASSES * (tile - 1)) // tile) * tile
    nt = p_max // tile
    counts = counts[:N_CLASSES].astype(jnp.int32)
    padded = ((counts + tile - 1) // tile) * tile
    pad_end = jnp.cumsum(padded)
    gstart = pad_end - padded
    class_ids = jnp.arange(N_CLASSES, dtype=jnp.int32)
    pos = rank + jnp.sum(jnp.where(cls[:, None] == class_ids[None, :], gstart[None, :], 0), axis=1)
    n_used = (pad_end[-1] // tile).astype(jnp.int32).reshape(1)
    tile_cls = jnp.sum((pad_end[None, :] <= (jnp.arange(nt, dtype=jnp.int32) * tile)[:, None]).astype(jnp.int32), axis=1)
    tile_cls = jnp.minimum(tile_cls, N_CLASSES - 1)
    tile_ea = jnp.asarray(_CLASS_LO)[tile_cls]
    tile_eb = jnp.asarray(_CLASS_HI)[tile_cls]
    return p_max, pos, gstart + counts, padded - counts, tile_ea, tile_eb, n_used


def _final_kernel(pos_ref, x1_ref, ys_ref, mod_ref, g_ref, o_ref, ybuf, sems, *, tile, n_steps):
    s = pl.program_id(0)
    slot = s % 2

    def issue(step, dst_slot):
        def one(r8, carry):
            for u in range(ISSUE_UNROLL):
                r = r8 * ISSUE_UNROLL + u
                pltpu.make_async_copy(ys_ref.at[pl.ds(pos_ref[step * tile + r], 1)],
                                      ybuf.at[dst_slot, pl.ds(r, 1)], sems.at[dst_slot]).start(priority=u % 2)
            return carry
        lax.fori_loop(0, tile // ISSUE_UNROLL, one, 0)

    @pl.when(s == 0)
    def _():
        issue(0, 0)

    @pl.when(s + 1 < n_steps)
    def _():
        issue(s + 1, 1 - slot)

    def wait_one(r, carry):
        pltpu.make_async_copy(ys_ref.at[pl.ds(0, 1)], ybuf.at[slot, pl.ds(0, 1)], sems.at[slot]).wait()
        return carry
    lax.fori_loop(0, tile, wait_one, 0)

    o_ref[...] = x1_ref[...] + mod_ref[0, 5:6, :] * _rms(ybuf[slot], g_ref[...])


def _final(x1, ys, pos, mods, g_post_ffn, tile):
    b, s, d = x1.shape
    n = b * s
    per_b = s // tile
    n_steps = n // tile
    tok = pl.BlockSpec((tile, d), lambda i, *_: (i, 0))
    out = pl.pallas_call(
        functools.partial(_final_kernel, tile=tile, n_steps=n_steps),
        out_shape=jax.ShapeDtypeStruct((n, d), F32),
        grid_spec=pltpu.PrefetchScalarGridSpec(
            num_scalar_prefetch=1,
            grid=(n_steps,),
            in_specs=[
                tok,
                pl.BlockSpec(memory_space=pl.ANY),
                pl.BlockSpec((1, 6, d), lambda i, *_: (i // per_b, 0, 0)),
                pl.BlockSpec((1, d), lambda i, *_: (0, 0)),
            ],
            out_specs=tok,
            scratch_shapes=[pltpu.VMEM((2, tile, d), F32), pltpu.SemaphoreType.DMA((2,))],
        ),
        compiler_params=_cparams("arbitrary"),
        name="final",
    )(pos, x1.reshape(n, d), ys, mods, g_post_ffn)
    return out.reshape(b, s, d)


def _t5_bucket(rel):
    half = T5_BUCKETS // 2
    exact = half // 2
    ret = jnp.where(rel > 0, half, 0)
    n = jnp.abs(rel)
    nf = jnp.maximum(n, 1).astype(F32)
    large = exact + (jnp.log(nf / exact) / math.log(T5_MAX_DIST / exact) * (half - exact)).astype(jnp.int32)
    large = jnp.minimum(large, half - 1)
    return ret + jnp.where(n < exact, n, large)


def _toeplitz(u, n_rows, n_cols):
    return jnp.stack([u[..., n_rows - 1 - i:n_rows - 1 - i + n_cols] for i in range(n_rows)], axis=-2)


def _bias_tables(t5_table, rel_table):
    ja = jnp.arange(CHUNK - 1 + A_BAND)
    ua = t5_table[_t5_bucket(ja - (CHUNK - 1) - A_WINDOW)].T.astype(F32)
    jb = jnp.arange(CHUNK - 1 + B_BAND)
    ub = rel_table[:, jnp.clip((CHUNK - 1) - jb + B_REACH, -REL_CLIP, REL_CLIP) + REL_CLIP].astype(F32)
    return _toeplitz(ua, CHUNK, A_BAND), _toeplitz(ub, CHUNK, B_BAND)


def _two_chunk(bias):
    masked = jnp.full(bias.shape[:-1] + (CHUNK,), MASKED, F32)
    return jnp.concatenate([jnp.concatenate([bias, masked], axis=-1), jnp.concatenate([masked, bias], axis=-1)], axis=-2)


def _pair_tables(bias_a, bias_b, sinks):
    a2, b2 = _two_chunk(bias_a), _two_chunk(bias_b)
    stack2 = lambda x, h0, h1: jnp.concatenate([x[h0], x[h1]], axis=0)
    bias2_a = jnp.stack([jnp.concatenate([stack2(a2, 4 * kv, 4 * kv + 2), stack2(a2, 4 * kv + 1, 4 * kv + 3)], axis=1)
                         for kv in range(A_KV_HEADS)])
    bias2_b = jnp.concatenate([b2[0::2], b2[1::2]], axis=-1)
    sk = jnp.broadcast_to(sinks.astype(F32).reshape(A_Q_HEADS, 1, 1), (A_Q_HEADS, ATT_BLOCK, 1))
    sinks2 = jnp.stack([jnp.stack([stack2(sk, 4 * kv, 4 * kv + 2), stack2(sk, 4 * kv + 1, 4 * kv + 3)])
                        for kv in range(A_KV_HEADS)])
    return bias2_a, sinks2, bias2_b


def _moe_and_final(x1, h2e, cls, rank, counts, mods, g_post_ffn, wgu, wd, seq_tile, moe_tile):
    b, s, d = x1.shape
    n = b * s
    p_max, pos, pad_start, pad_len, tile_ea, tile_eb, n_used = _moe_plan(
        counts.reshape(CLASS_ROWS), cls.reshape(n), rank.reshape(n), n, moe_tile)
    xs = _dispatch(h2e.reshape(n, ROW_W), pos, pad_start, pad_len, n_used, p_max, seq_tile, moe_tile)
    ys = _moe_grouped(xs, tile_ea, tile_eb, n_used, wgu, wd, moe_tile)
    return _final(x1, ys, pos, mods, g_post_ffn, seq_tile)


def kernel(x_prompt, x_sample, c_prompt, c_sample, cache_a_k, cache_a_v, cache_b_k, cache_b_v, w_ada, b_ada, g_pre_mix, g_post_mix, g_pre_ffn, g_post_ffn, w_in, a_sinks, t5_table, b_rel_table, w_proj_a, w_proj_b, w_gate, b_gate, w_o, w_route_g, b_route_g, w_route_e, b_route_e, w_e_gate, w_e_up, w_e_down):
    depth = w_in.shape[0]
    assert depth == 1
    l = 0
    bp, sp, d = x_prompt.shape
    bs, ts, _ = x_sample.shape

    mods = _ada(jnp.concatenate([c_prompt, c_sample], axis=0), w_ada[l], b_ada[l]).reshape(bp + bs, 6, d)
    mods_p, mods_s = mods[:bp], mods[bp:]

    w_in_bf = w_in[l].astype(BF16)
    wg, wpa, wpb, wo = w_gate[l].astype(BF16), w_proj_a[l].astype(BF16), w_proj_b[l].astype(BF16), w_o[l].astype(BF16)
    bg = b_gate[l].reshape(1, 2 * d)
    pad_rows = ROUTE_ROWS - N_EXPERTS - N_GROUPS
    wr = jnp.concatenate([w_route_e[l].T, w_route_g[l].T, jnp.zeros((pad_rows, d), F32)], axis=0).astype(BF16)
    br = jnp.concatenate([b_route_e[l], b_route_g[l], jnp.zeros((pad_rows,), F32)]).reshape(ROUTE_ROWS, 1)
    wgu = jnp.concatenate([w_e_gate[l], w_e_up[l]], axis=-1).astype(BF16)
    wd = w_e_down[l].astype(BF16)
    g1, g1p, g2, g2p = (g[l].reshape(1, d) for g in (g_pre_mix, g_post_mix, g_pre_ffn, g_post_ffn))

    bias_a, bias_b = _bias_tables(t5_table, b_rel_table[l])
    bias2_a, sinks2, bias2_b = _pair_tables(bias_a, bias_b, a_sinks[l])
    bias_a = bias_a.reshape(A_KV_HEADS, A_GROUP * CHUNK, A_BAND)
    sinks = jnp.broadcast_to(a_sinks[l].reshape(A_KV_HEADS, A_GROUP, 1, 1), (A_KV_HEADS, A_GROUP, CHUNK, 1))
    sinks = sinks.reshape(A_KV_HEADS, A_GROUP * CHUNK, 1).astype(F32)

    qa, ka, va, qb, kb, vb, sak_p, sav_p, sbk_p, sbv_p = _pre_prompt(x_prompt, mods_p, g1, w_in_bf)
    oa, ob = _attn_prompt(qa, ka, va, qb, kb, vb, bias2_a, sinks2, bias2_b)
    x1, h2e, cls, rank, counts = _post(x_prompt, oa, ob, mods_p, g1, g1p, g2, wg, bg, wpa, wpb, wo, wr, br, SEQ_TILE)
    y_prompt = _moe_and_final(x1, h2e, cls, rank, counts, mods_p, g2p, wgu, wd, SEQ_TILE, 256)

    la, lb = cache_a_k.shape[2], cache_b_k.shape[2]
    cak = cache_a_k[l].reshape(bs, la, KVA_W)
    cav = cache_a_v[l].reshape(bs, la, KVA_W)
    cbk = cache_b_k[l].reshape(bs, lb, B_W)
    cbv = cache_b_v[l].reshape(bs, lb, B_W)
    qa, ka, va, qb, kb, vb, sak_s, sav_s, sbk_s, sbv_s = _pre_sample(x_sample, mods_s, g1, w_in_bf, cak, cav, cbk, cbv)
    oa, ob = _attn_sample(qa, ka, va, qb, kb, vb, bias_a, sinks, bias_b)
    x1, h2e, cls, rank, counts = _post(x_sample, oa, ob, mods_s, g1, g1p, g2, wg, bg, wpa, wpb, wo, wr, br, ts)
    y_sample = _moe_and_final(x1, h2e, cls, rank, counts, mods_s, g2p, wgu, wd, ts, 64)

    a_state = lambda v, b, r: v.reshape(1, b, r, A_KV_HEADS, HEAD_DIM)
    b_state = lambda v, b, r: v.reshape(1, b, r, B_HEADS, HEAD_DIM)
    return (y_prompt, y_sample,
            a_state(sak_p, bp, A_WINDOW), a_state(sav_p, bp, A_WINDOW),
            b_state(sbk_p, bp, B_REACH), b_state(sbv_p, bp, B_REACH),
            a_state(sak_s, bs, la), a_state(sav_s, bs, la),
            b_state(sbk_s, bs, lb), b_state(sbv_s, bs, lb))
```

```python
import functools
import math

import numpy as np
import jax
import jax.numpy as jnp
from jax import lax
from jax.experimental import pallas as pl
from jax.experimental.pallas import tpu as pltpu

D_MODEL = 1024
CHUNK = 64
HEAD_DIM = 64
A_Q_HEADS = 8
A_KV_HEADS = 2
A_GROUP = A_Q_HEADS // A_KV_HEADS
A_WINDOW = 128
A_BACK = A_WINDOW // CHUNK
B_HEADS = 8
B_BACK = 8
B_REACH = B_BACK * CHUNK
REL_CLIP = 256
T5_BUCKETS = 32
T5_MAX_DIST = 128
N_GROUPS = 4
EXPERTS_PER_GROUP = 8
N_EXPERTS = N_GROUPS * EXPERTS_PER_GROUP
D_EXPERT = D_MODEL // 4
EPS = 1e-6

QA_W = A_Q_HEADS * HEAD_DIM
KVA_W = A_KV_HEADS * HEAD_DIM
B_W = B_HEADS * HEAD_DIM
IN_W = QA_W + 2 * KVA_W + 3 * B_W
A_BAND = A_WINDOW + CHUNK
B_BAND = B_REACH + CHUNK

PAIRS_PER_GROUP = EXPERTS_PER_GROUP * (EXPERTS_PER_GROUP - 1) // 2
N_CLASSES = N_GROUPS * PAIRS_PER_GROUP
ROUTE_ROWS = 40
CLASS_ROWS = 128
LANES = 128
IN_SLAB = D_MODEL // LANES + 1
OUT_SLAB = D_MODEL // LANES

F32 = jnp.float32
BF16 = jnp.bfloat16

VMEM_LIMIT_BYTES = 56 * 1024 * 1024

SEQ_TILE = 512
ISSUE_UNROLL = 8


def _cparams(*sem):
    return pltpu.CompilerParams(dimension_semantics=sem, vmem_limit_bytes=VMEM_LIMIT_BYTES)


def _norm_mod(x, g, scale, shift):
    y = x * lax.rsqrt(jnp.mean(x * x, axis=-1, keepdims=True) + EPS)
    return (y * g) * (1.0 + scale) + shift


def _rms(x, g):
    return (x * lax.rsqrt(jnp.mean(x * x, axis=-1, keepdims=True) + EPS)) * g


def _ada_kernel(c_ref, w_ref, b_ref, o_ref):
    c = c_ref[...]
    s = (c * jax.nn.sigmoid(c)).astype(BF16)
    o_ref[...] = jnp.dot(s, w_ref[...].astype(BF16), preferred_element_type=F32) + b_ref[...]


def _ada(c, w_ada, b_ada):
    n, d = c.shape
    wn = w_ada.shape[1]
    tn = 512
    return pl.pallas_call(
        _ada_kernel,
        out_shape=jax.ShapeDtypeStruct((n, wn), F32),
        grid=(wn // tn,),
        in_specs=[
            pl.BlockSpec((n, d), lambda j: (0, 0)),
            pl.BlockSpec((d, tn), lambda j: (0, j)),
            pl.BlockSpec((1, tn), lambda j: (0, j)),
        ],
        out_specs=pl.BlockSpec((n, tn), lambda j: (0, j)),
        compiler_params=_cparams("arbitrary"),
        name="ada",
    )(c, w_ada, b_ada.reshape(1, wn))


_COL_QA = (0, QA_W)
_COL_KA = (QA_W, QA_W + KVA_W)
_COL_VA = (QA_W + KVA_W, QA_W + 2 * KVA_W)
_COL_QB = (QA_W + 2 * KVA_W, QA_W + 2 * KVA_W + B_W)
_COL_KB = (_COL_QB[1], _COL_QB[1] + B_W)
_COL_VB = (_COL_KB[1], _COL_KB[1] + B_W)
Q_SCALE = HEAD_DIM ** -0.5


def _project(x_ref, mod_ref, g_ref, w_ref):
    h = _norm_mod(x_ref[0], g_ref[...], mod_ref[0, 1:2, :], mod_ref[0, 0:1, :])
    return jnp.dot(h.astype(BF16), w_ref[...], preferred_element_type=F32)


def _cols(p, c):
    return p[:, c[0]:c[1]]


def _pre_prompt_kernel(x_ref, mod_ref, g_ref, w_ref,
                       qa_ref, ka_ref, va_ref, qb_ref, kb_ref, vb_ref,
                       sak_ref, sav_ref, sbk_ref, sbv_ref, *, n_tiles, tile):
    p = _project(x_ref, mod_ref, g_ref, w_ref)
    qa_ref[0] = (_cols(p, _COL_QA) * Q_SCALE).astype(BF16)
    ka_ref[0] = _cols(p, _COL_KA).astype(BF16)
    va_ref[0] = _cols(p, _COL_VA).astype(BF16)
    qb_ref[0] = (_cols(p, _COL_QB) * Q_SCALE).astype(BF16)
    kb_ref[0] = _cols(p, _COL_KB).astype(BF16)
    vb_ref[0] = _cols(p, _COL_VB).astype(BF16)

    @pl.when(pl.program_id(1) == n_tiles - 1)
    def _():
        sak_ref[0] = _cols(p, _COL_KA)[tile - A_WINDOW:, :]
        sav_ref[0] = _cols(p, _COL_VA)[tile - A_WINDOW:, :]
        sbk_ref[0] = _cols(p, _COL_KB)[tile - B_REACH:, :]
        sbv_ref[0] = _cols(p, _COL_VB)[tile - B_REACH:, :]


def _pre_prompt(x, mods, g_pre, w_in_bf):
    b, s, d = x.shape
    tile = SEQ_TILE
    assert s % tile == 0 and tile >= B_REACH and s >= B_REACH
    nt = s // tile
    tok = lambda w: pl.BlockSpec((1, tile, w), lambda i, j: (i, j, 0))
    state = lambda r, w: pl.BlockSpec((1, r, w), lambda i, j: (i, 0, 0))
    return pl.pallas_call(
        functools.partial(_pre_prompt_kernel, n_tiles=nt, tile=tile),
        out_shape=(
            jax.ShapeDtypeStruct((b, s, QA_W), BF16),
            jax.ShapeDtypeStruct((b, s, KVA_W), BF16),
            jax.ShapeDtypeStruct((b, s, KVA_W), BF16),
            jax.ShapeDtypeStruct((b, s, B_W), BF16),
            jax.ShapeDtypeStruct((b, s, B_W), BF16),
            jax.ShapeDtypeStruct((b, s, B_W), BF16),
            jax.ShapeDtypeStruct((b, A_WINDOW, KVA_W), F32),
            jax.ShapeDtypeStruct((b, A_WINDOW, KVA_W), F32),
            jax.ShapeDtypeStruct((b, B_REACH, B_W), F32),
            jax.ShapeDtypeStruct((b, B_REACH, B_W), F32),
        ),
        grid=(b, nt),
        in_specs=[
            tok(d),
            pl.BlockSpec((1, 6, d), lambda i, j: (i, 0, 0)),
            pl.BlockSpec((1, d), lambda i, j: (0, 0)),
            pl.BlockSpec((d, IN_W), lambda i, j: (0, 0)),
        ],
        out_specs=(
            tok(QA_W), tok(KVA_W), tok(KVA_W), tok(B_W), tok(B_W), tok(B_W),
            state(A_WINDOW, KVA_W), state(A_WINDOW, KVA_W), state(B_REACH, B_W), state(B_REACH, B_W),
        ),
        compiler_params=_cparams("parallel", "arbitrary"),
        name="pre_prompt",
    )(x, mods, g_pre, w_in_bf)


def _pre_sample_kernel(x_ref, mod_ref, g_ref, w_ref, cak_ref, cav_ref, cbk_ref, cbv_ref,
                       qa_ref, ka_ref, va_ref, qb_ref, kb_ref, vb_ref,
                       sak_ref, sav_ref, sbk_ref, sbv_ref, *, t, la, lb):
    p = _project(x_ref, mod_ref, g_ref, w_ref)
    qa_ref[0] = (_cols(p, _COL_QA) * Q_SCALE).astype(BF16)
    qb_ref[0] = (_cols(p, _COL_QB) * Q_SCALE).astype(BF16)
    for cache_ref, band_ref, state_ref, col, hist in (
            (cak_ref, ka_ref, sak_ref, _COL_KA, la), (cav_ref, va_ref, sav_ref, _COL_VA, la),
            (cbk_ref, kb_ref, sbk_ref, _COL_KB, lb), (cbv_ref, vb_ref, sbv_ref, _COL_VB, lb)):
        new = _cols(p, col)
        cache = cache_ref[0]
        band_ref[0, 0:hist, :] = cache.astype(BF16)
        band_ref[0, hist:hist + t, :] = new.astype(BF16)
        state_ref[0, 0:hist - t, :] = cache[t:, :]
        state_ref[0, hist - t:hist, :] = new


def _pre_sample(x, mods, g_pre, w_in_bf, cak, cav, cbk, cbv):
    b, t, d = x.shape
    la, lb = cak.shape[1], cbk.shape[1]
    assert t == CHUNK and la == A_WINDOW and lb == B_REACH
    per_b = lambda r, w: pl.BlockSpec((1, r, w), lambda i: (i, 0, 0))
    return pl.pallas_call(
        functools.partial(_pre_sample_kernel, t=t, la=la, lb=lb),
        out_shape=(
            jax.ShapeDtypeStruct((b, t, QA_W), BF16),
            jax.ShapeDtypeStruct((b, la + t, KVA_W), BF16),
            jax.ShapeDtypeStruct((b, la + t, KVA_W), BF16),
            jax.ShapeDtypeStruct((b, t, B_W), BF16),
            jax.ShapeDtypeStruct((b, lb + t, B_W), BF16),
            jax.ShapeDtypeStruct((b, lb + t, B_W), BF16),
            jax.ShapeDtypeStruct((b, la, KVA_W), F32),
            jax.ShapeDtypeStruct((b, la, KVA_W), F32),
            jax.ShapeDtypeStruct((b, lb, B_W), F32),
            jax.ShapeDtypeStruct((b, lb, B_W), F32),
        ),
        grid=(b,),
        in_specs=[
            per_b(t, d),
            per_b(6, d),
            pl.BlockSpec((1, d), lambda i: (0, 0)),
            pl.BlockSpec((d, IN_W), lambda i: (0, 0)),
            per_b(la, KVA_W), per_b(la, KVA_W), per_b(lb, B_W), per_b(lb, B_W),
        ],
        out_specs=(
            per_b(t, QA_W), per_b(la + t, KVA_W), per_b(la + t, KVA_W),
            per_b(t, B_W), per_b(lb + t, B_W), per_b(lb + t, B_W),
            per_b(la, KVA_W), per_b(la, KVA_W), per_b(lb, B_W), per_b(lb, B_W),
        ),
        compiler_params=_cparams("parallel"),
        name="pre_sample",
    )(x, mods, g_pre, w_in_bf, cak, cav, cbk, cbv)


def _attend_a(q, k, v, bias_ref, sink_ref, nk):
    boff = A_BAND - nk
    outs = []
    for kv in range(A_KV_HEADS):
        qs = jnp.concatenate(
            [q[:, (kv * A_GROUP + g) * HEAD_DIM:(kv * A_GROUP + g + 1) * HEAD_DIM] for g in range(A_GROUP)], axis=0)
        kh = k[:, kv * HEAD_DIM:(kv + 1) * HEAD_DIM]
        vh = v[:, kv * HEAD_DIM:(kv + 1) * HEAD_DIM]
        s = lax.dot_general(qs, kh, (((1,), (1,)), ((), ())), preferred_element_type=F32)
        s = s + bias_ref[kv, :, boff:boff + nk]
        sk = sink_ref[kv]
        mx = jnp.maximum(jnp.max(s, axis=-1, keepdims=True), sk)
        e = jnp.exp(s - mx)
        den = jnp.sum(e, axis=-1, keepdims=True) + jnp.exp(sk - mx)
        o = jnp.dot(e.astype(BF16), vh, preferred_element_type=F32) / den
        outs.extend(o[g * CHUNK:(g + 1) * CHUNK, :] for g in range(A_GROUP))
    return jnp.concatenate(outs, axis=1).astype(BF16)


def _attend_b(q, k, v, bias_ref, nk):
    boff = B_BAND - nk
    outs = []
    for h in range(B_HEADS):
        sl = slice(h * HEAD_DIM, (h + 1) * HEAD_DIM)
        s = lax.dot_general(q[:, sl], k[:, sl], (((1,), (1,)), ((), ())), preferred_element_type=F32)
        s = s + bias_ref[h, :, boff:boff + nk]
        mx = jnp.max(s, axis=-1, keepdims=True)
        e = jnp.exp(s - mx)
        den = jnp.sum(e, axis=-1, keepdims=True)
        outs.append(jnp.dot(e.astype(BF16), v[:, sl], preferred_element_type=F32) / den)
    return jnp.concatenate(outs, axis=1).astype(BF16)


MASKED = -1e30
PAIR_W = 2 * HEAD_DIM
ATT_BLOCK = 2 * CHUNK
A_COLS = A_WINDOW + ATT_BLOCK
B_COLS = B_REACH + ATT_BLOCK


def _pair_attend(q2, k2, v2, bias, sinks, first_valid):
    c = k2.shape[0] // 2
    s = lax.dot_general(q2, k2, (((1,), (1,)), ((), ())), preferred_element_type=F32) + bias
    es, dens = [], []
    for half in range(2):
        sh = s[:, half * c:(half + 1) * c]
        if first_valid > 0:
            sh = jnp.where(lax.broadcasted_iota(jnp.int32, sh.shape, 1) >= first_valid, sh, MASKED)
        mx = jnp.max(sh, axis=-1, keepdims=True)
        if sinks is not None:
            mx = jnp.maximum(mx, sinks[half])
        e = jnp.exp(sh - mx)
        den = jnp.sum(e, axis=-1, keepdims=True)
        if sinks is not None:
            den = den + jnp.exp(sinks[half] - mx)
        es.append(e.astype(BF16))
        dens.append(den)
    o = jnp.dot(jnp.concatenate(es, axis=1), v2, preferred_element_type=F32)
    lane = lax.broadcasted_iota(jnp.int32, o.shape, 1)
    return o / jnp.where(lane < HEAD_DIM, dens[0], dens[1])


def _attn_prompt_kernel(qa_ref, ka_ref, va_ref, qb_ref, kb_ref, vb_ref, ba_ref, sk_ref, bb_ref, lo_ref, hi_ref,
                        oa_ref, ob_ref, ak_ref, av_ref, bk_ref, bv_ref, *, tile):
    j = pl.program_id(1)

    @pl.when(j == 0)
    def _():
        lo, hi = lo_ref[:, 0:PAIR_W], hi_ref[:, 0:PAIR_W]
        for src, dst in ((ka_ref, ak_ref), (va_ref, av_ref)):
            dst[:, 0:A_WINDOW, :] = jnp.zeros((4, A_WINDOW, PAIR_W), BF16)
            x = src[0]
            swapped = jnp.concatenate([x[:, HEAD_DIM:], x[:, :HEAD_DIM]], axis=1)
            dst[0, A_WINDOW:, :] = x * lo
            dst[1, A_WINDOW:, :] = swapped * hi
            dst[2, A_WINDOW:, :] = swapped * lo
            dst[3, A_WINDOW:, :] = x * hi
        for src, dst in ((kb_ref, bk_ref), (vb_ref, bv_ref)):
            dst[:, 0:B_REACH, :] = jnp.zeros((2, B_REACH, B_W), BF16)
            x = src[0]
            dst[0, B_REACH:, :] = x * lo_ref[...]
            dst[1, B_REACH:, :] = x * hi_ref[...]

    def block(g, r0, first_a, first_b):
        rows = pl.ds(r0, ATT_BLOCK)
        band_a = pl.ds(g, A_COLS)
        band_b = pl.ds(g, B_COLS)
        for kv in range(A_KV_HEADS):
            c0, c1 = slice(2 * kv * PAIR_W, (2 * kv + 1) * PAIR_W), slice((2 * kv + 1) * PAIR_W, (2 * kv + 2) * PAIR_W)
            q2 = jnp.concatenate([qa_ref[0, rows, c0], qa_ref[0, rows, c1]], axis=0)
            k2 = jnp.concatenate([ak_ref[2 * kv, band_a, :], ak_ref[2 * kv + 1, band_a, :]], axis=0)
            v2 = jnp.concatenate([av_ref[2 * kv, band_a, :], av_ref[2 * kv + 1, band_a, :]], axis=0)
            o = _pair_attend(q2, k2, v2, ba_ref[kv], (sk_ref[kv, 0], sk_ref[kv, 1]), first_a)
            oa_ref[0, rows, c0] = o[0:ATT_BLOCK].astype(BF16)
            oa_ref[0, rows, c1] = o[ATT_BLOCK:].astype(BF16)
        for p in range(B_HEADS // 2):
            cols = slice(p * PAIR_W, (p + 1) * PAIR_W)
            k2 = jnp.concatenate([bk_ref[0, band_b, cols], bk_ref[1, band_b, cols]], axis=0)
            v2 = jnp.concatenate([bv_ref[0, band_b, cols], bv_ref[1, band_b, cols]], axis=0)
            o = _pair_attend(qb_ref[0, rows, cols], k2, v2, bb_ref[p], None, first_b)
            ob_ref[0, rows, cols] = o.astype(BF16)

    n_blocks = tile // ATT_BLOCK

    @pl.when(j == 0)
    def _():
        for blk in range(n_blocks):
            g = blk * ATT_BLOCK
            block(g, g, max(A_WINDOW - g, 0), max(B_REACH - g, 0))

    @pl.when(j > 0)
    def _():
        def body(blk, carry):
            r0 = pl.multiple_of(blk * ATT_BLOCK, ATT_BLOCK)
            block(pl.multiple_of(j * tile + r0, ATT_BLOCK), r0, 0, 0)
            return carry
        lax.fori_loop(0, n_blocks, body, 0)


def _attn_prompt(qa, ka, va, qb, kb, vb, bias2_a, sinks2, bias2_b):
    b, s, _ = qa.shape
    tile = SEQ_TILE
    assert s % tile == 0 and tile % ATT_BLOCK == 0
    lane = np.arange(B_W) % PAIR_W
    lo = jnp.asarray((lane < HEAD_DIM).astype(np.float32).reshape(1, B_W), BF16)
    hi = jnp.asarray((lane >= HEAD_DIM).astype(np.float32).reshape(1, B_W), BF16)
    tok = lambda w: pl.BlockSpec((1, tile, w), lambda i, j: (i, j, 0))
    seq = lambda w: pl.BlockSpec((1, s, w), lambda i, j: (i, 0, 0))
    const = lambda shp: pl.BlockSpec(shp, lambda i, j: (0,) * len(shp))
    return pl.pallas_call(
        functools.partial(_attn_prompt_kernel, tile=tile),
        out_shape=(jax.ShapeDtypeStruct((b, s, QA_W), BF16), jax.ShapeDtypeStruct((b, s, B_W), BF16)),
        grid=(b, s // tile),
        in_specs=[tok(QA_W), seq(KVA_W), seq(KVA_W), tok(B_W), seq(B_W), seq(B_W),
                  const(bias2_a.shape), const(sinks2.shape), const(bias2_b.shape), const(lo.shape), const(hi.shape)],
        out_specs=(tok(QA_W), tok(B_W)),
        scratch_shapes=[pltpu.VMEM((4, A_WINDOW + s, PAIR_W), BF16), pltpu.VMEM((4, A_WINDOW + s, PAIR_W), BF16),
                        pltpu.VMEM((2, B_REACH + s, B_W), BF16), pltpu.VMEM((2, B_REACH + s, B_W), BF16)],
        compiler_params=_cparams("arbitrary", "arbitrary"),
        name="attn_prompt",
    )(qa, ka, va, qb, kb, vb, bias2_a, sinks2, bias2_b, lo, hi)


def _attn_sample_kernel(qa_ref, ka_ref, va_ref, qb_ref, kb_ref, vb_ref, ba_ref, sk_ref, bb_ref, oa_ref, ob_ref):
    oa_ref[0] = _attend_a(qa_ref[0], ka_ref[0], va_ref[0], ba_ref, sk_ref, A_BAND)
    ob_ref[0] = _attend_b(qb_ref[0], kb_ref[0], vb_ref[0], bb_ref, B_BAND)


def _attn_sample(qa, ka, va, qb, kb, vb, bias_a, sinks, bias_b):
    b, t, _ = qa.shape
    per_b = lambda r, w: pl.BlockSpec((1, r, w), lambda i: (i, 0, 0))
    const = lambda shp: pl.BlockSpec(shp, lambda i: (0,) * len(shp))
    return pl.pallas_call(
        _attn_sample_kernel,
        out_shape=(jax.ShapeDtypeStruct((b, t, QA_W), BF16), jax.ShapeDtypeStruct((b, t, B_W), BF16)),
        grid=(b,),
        in_specs=[per_b(t, QA_W), per_b(A_BAND, KVA_W), per_b(A_BAND, KVA_W),
                  per_b(t, B_W), per_b(B_BAND, B_W), per_b(B_BAND, B_W),
                  const(bias_a.shape), const(sinks.shape), const(bias_b.shape)],
        out_specs=(per_b(t, QA_W), per_b(t, B_W)),
        compiler_params=_cparams("parallel"),
        name="attn_sample",
    )(qa, ka, va, qb, kb, vb, bias_a, sinks, bias_b)


def _route(lt):
    t = lt.shape[1]
    el = lt[0:N_EXPERTS]
    gl = lt[N_EXPERTS:N_EXPERTS + N_GROUPS]
    gmax = jnp.max(gl, axis=0, keepdims=True)
    gi = lax.broadcasted_iota(jnp.int32, (N_GROUPS, t), 0)
    gidx = jnp.min(jnp.where(gl == gmax, gi, N_GROUPS), axis=0, keepdims=True)
    g_w = 1.0 / jnp.sum(jnp.exp(gl - gmax), axis=0, keepdims=True)
    e_sel = el[(N_GROUPS - 1) * EXPERTS_PER_GROUP:]
    for g in range(N_GROUPS - 2, -1, -1):
        e_sel = jnp.where(gidx == g, el[g * EXPERTS_PER_GROUP:(g + 1) * EXPERTS_PER_GROUP], e_sel)
    ei = lax.broadcasted_iota(jnp.int32, (EXPERTS_PER_GROUP, t), 0)
    m1 = jnp.max(e_sel, axis=0, keepdims=True)
    i1 = jnp.min(jnp.where(e_sel == m1, ei, EXPERTS_PER_GROUP), axis=0, keepdims=True)
    rest = jnp.where(ei == i1, -jnp.inf, e_sel)
    m2 = jnp.max(rest, axis=0, keepdims=True)
    i2 = jnp.min(jnp.where(rest == m2, ei, EXPERTS_PER_GROUP), axis=0, keepdims=True)
    ex = jnp.exp(m2 - m1)
    den = 1.0 + ex
    w1 = g_w * (1.0 / den)
    w2 = g_w * (ex / den)
    lo = jnp.minimum(i1, i2)
    hi = jnp.maximum(i1, i2)
    first_is_lo = i1 < i2
    w_lo = jnp.where(first_is_lo, w1, w2)
    w_hi = jnp.where(first_is_lo, w2, w1)
    pair = ((lo * (2 * EXPERTS_PER_GROUP - 1 - lo)) >> 1) + (hi - lo - 1)
    return gidx * PAIRS_PER_GROUP + pair, w_lo, w_hi


def _col_from_row(w_row, n):
    ri = lax.broadcasted_iota(jnp.int32, (n, n), 0)
    ci = lax.broadcasted_iota(jnp.int32, (n, n), 1)
    return jnp.sum(jnp.where(ri == ci, jnp.broadcast_to(w_row, (n, n)), 0.0), axis=1, keepdims=True)


def _rank_in_class(cls, run_ref, tri_ref):
    t = cls.shape[1]
    onehot = lax.broadcasted_iota(jnp.int32, (CLASS_ROWS, t), 0) == cls
    ones = jnp.where(onehot, 1.0, 0.0)
    before = jnp.dot(ones.astype(BF16), tri_ref[...], preferred_element_type=F32)
    run = run_ref[...]
    rank = jnp.sum(jnp.where(onehot, before + run, 0.0), axis=0, keepdims=True)
    run_ref[...] = run + jnp.sum(ones, axis=1, keepdims=True)
    return rank.astype(jnp.int32)


def _post_kernel(x_ref, oa_ref, ob_ref, mod_ref, g1_ref, g1p_ref, g2_ref,
                 wg_ref, bg_ref, wpa_ref, wpb_ref, wo_ref, wr_ref, br_ref, tri_ref,
                 x1_ref, h2_ref, cls_ref, rank_ref, cnt_ref, run_ref):
    @pl.when((pl.program_id(0) == 0) & (pl.program_id(1) == 0))
    def _():
        run_ref[...] = jnp.zeros_like(run_ref)

    x = x_ref[0]
    h = _norm_mod(x, g1_ref[...], mod_ref[0, 1:2, :], mod_ref[0, 0:1, :]).astype(BF16)
    gates = jax.nn.sigmoid(jnp.dot(h, wg_ref[...], preferred_element_type=F32) + bg_ref[...])
    pa = jnp.dot(oa_ref[0], wpa_ref[...], preferred_element_type=F32)
    pb = jnp.dot(ob_ref[0], wpb_ref[...], preferred_element_type=F32)
    mixed = gates[:, :D_MODEL] * pa + gates[:, D_MODEL:] * pb
    y = jnp.dot(mixed.astype(BF16), wo_ref[...], preferred_element_type=F32)
    x1 = x + mod_ref[0, 2:3, :] * _rms(y, g1p_ref[...])
    x1_ref[0] = x1
    h2 = _norm_mod(x1, g2_ref[...], mod_ref[0, 4:5, :], mod_ref[0, 3:4, :]).astype(BF16)
    lt = lax.dot_general(wr_ref[...], h2, (((1,), (1,)), ((), ())), preferred_element_type=F32) + br_ref[...]
    cls, w_lo, w_hi = _route(lt)
    cls_ref[0, 0] = cls
    rank_ref[0, 0] = _rank_in_class(cls, run_ref, tri_ref)
    cnt_ref[...] = run_ref[...]
    t = x.shape[0]
    lane = lax.broadcasted_iota(jnp.int32, (t, LANES), 1)
    h2f = h2.astype(F32)
    for c in range(D_MODEL // LANES):
        h2_ref[pl.ds(c, t, stride=IN_SLAB), :] = h2f[:, c * LANES:(c + 1) * LANES]
    h2_ref[pl.ds(D_MODEL // LANES, t, stride=IN_SLAB), :] = jnp.where(
        lane == 0, _col_from_row(w_lo, t), jnp.where(lane == 1, _col_from_row(w_hi, t), 0.0))


def _post(x, oa, ob, mods, g_pre_mix, g_post_mix, g_pre_ffn, wg, bg, wpa, wpb, wo, wr, br, tile):
    b, s, d = x.shape
    nt = s // tile
    tri = (jnp.arange(tile)[:, None] < jnp.arange(tile)[None, :]).astype(BF16)
    tok = lambda w: pl.BlockSpec((1, tile, w), lambda i, j: (i, j, 0))
    const = lambda shp: pl.BlockSpec(shp, lambda i, j: (0,) * len(shp))
    per_tile = pl.BlockSpec((1, 1, 1, tile), lambda i, j: (i, j, 0, 0))
    return pl.pallas_call(
        _post_kernel,
        out_shape=(
            jax.ShapeDtypeStruct((b, s, d), F32),
            jax.ShapeDtypeStruct((b * s * IN_SLAB, LANES), F32),
            jax.ShapeDtypeStruct((b, nt, 1, tile), jnp.int32),
            jax.ShapeDtypeStruct((b, nt, 1, tile), jnp.int32),
            jax.ShapeDtypeStruct((CLASS_ROWS, 1), F32),
        ),
        grid=(b, nt),
        in_specs=[
            tok(d), tok(QA_W), tok(B_W),
            pl.BlockSpec((1, 6, d), lambda i, j: (i, 0, 0)),
            const((1, d)), const((1, d)), const((1, d)),
            const(wg.shape), const(bg.shape), const(wpa.shape), const(wpb.shape), const(wo.shape),
            const(wr.shape), const(br.shape), const(tri.shape),
        ],
        out_specs=(tok(d), pl.BlockSpec((tile * IN_SLAB, LANES), lambda i, j: (i * nt + j, 0)),
                   per_tile, per_tile, const((CLASS_ROWS, 1))),
        scratch_shapes=[pltpu.VMEM((CLASS_ROWS, 1), F32)],
        compiler_params=_cparams("arbitrary", "arbitrary"),
        name="post",
    )(x, oa, ob, mods, g_pre_mix, g_post_mix, g_pre_ffn, wg, bg, wpa, wpb, wo, wr, br, tri)


def _dispatch_kernel(pos_ref, pad_start_ref, pad_len_ref, nu_ref, h_ref, xs_ref, zero_ref, sem, *,
                     tile, n_steps, moe_tile, n_moe_tiles):
    t = pl.program_id(0)

    def slab_copy(src, src_row, dst_row):
        return pltpu.make_async_copy(src.at[pl.ds(src_row * IN_SLAB, IN_SLAB)],
                                     xs_ref.at[pl.ds(dst_row * IN_SLAB, IN_SLAB)], sem)

    def wait_rows(n):
        def wait_one(r, carry):
            slab_copy(h_ref, 0, 0).wait()
            return carry
        lax.fori_loop(0, n, wait_one, 0)

    def issue(r8, carry):
        for u in range(ISSUE_UNROLL):
            r = r8 * ISSUE_UNROLL + u
            slab_copy(h_ref, r, pos_ref[t * tile + r]).start()
        return carry
    lax.fori_loop(0, tile // ISSUE_UNROLL, issue, 0)
    wait_rows(tile)

    @pl.when(t == n_steps - 1)
    def _():
        zero_ref[...] = jnp.zeros_like(zero_ref)

        def per_class(c, carry):
            start, n = pad_start_ref[c], pad_len_ref[c]

            def fill(r, inner):
                slab_copy(zero_ref, 0, start + r).start()
                return inner
            lax.fori_loop(0, n, fill, 0)
            wait_rows(n)
            return carry
        lax.fori_loop(0, N_CLASSES, per_class, 0)

        def tile_copy(k):
            rows = moe_tile * IN_SLAB
            return pltpu.make_async_copy(zero_ref, xs_ref.at[pl.ds(k * rows, rows)], sem)

        def zero_tile(k, carry):
            tile_copy(k).start()
            return carry
        lax.fori_loop(nu_ref[0], n_moe_tiles, zero_tile, 0)

        def wait_tile(k, carry):
            tile_copy(0).wait()
            return carry
        lax.fori_loop(nu_ref[0], n_moe_tiles, wait_tile, 0)


def _dispatch(h_slabs, pos, pad_start, pad_len, n_used, p_max, tile, moe_tile):
    n = h_slabs.shape[0] // IN_SLAB
    n_steps = n // tile
    return pl.pallas_call(
        functools.partial(_dispatch_kernel, tile=tile, n_steps=n_steps, moe_tile=moe_tile,
                          n_moe_tiles=p_max // moe_tile),
        out_shape=jax.ShapeDtypeStruct((p_max * IN_SLAB, LANES), F32),
        grid_spec=pltpu.PrefetchScalarGridSpec(
            num_scalar_prefetch=4,
            grid=(n_steps,),
            in_specs=[pl.BlockSpec((tile * IN_SLAB, LANES), lambda t, *_: (t, 0))],
            out_specs=pl.BlockSpec(memory_space=pl.ANY),
            scratch_shapes=[pltpu.VMEM((moe_tile * IN_SLAB, LANES), F32), pltpu.SemaphoreType.DMA(())],
        ),
        compiler_params=_cparams("arbitrary"),
        name="dispatch",
    )(pos, pad_start, pad_len, n_used, h_slabs)


def _expert(xb, wgu_ref, wd_ref):
    gu = jnp.dot(xb, wgu_ref[0], preferred_element_type=F32)
    gate = gu[:, :D_EXPERT]
    he = (gate * jax.nn.sigmoid(gate)) * gu[:, D_EXPERT:]
    return jnp.dot(he.astype(BF16), wd_ref[0], preferred_element_type=F32)


def _moe_kernel(ea_ref, eb_ref, nu_ref, xs_ref, wgu_a_ref, wd_a_ref, wgu_b_ref, wd_b_ref, ys_ref, *, tile):
    t = pl.program_id(0)

    @pl.when(t < nu_ref[0])
    def _():
        lane_tile = lambda c: xs_ref[pl.ds(c, tile, stride=IN_SLAB), :]
        xb = jnp.concatenate([lane_tile(c) for c in range(D_MODEL // LANES)], axis=1).astype(BF16)
        w = lane_tile(D_MODEL // LANES)
        y = w[:, 0:1] * _expert(xb, wgu_a_ref, wd_a_ref) + w[:, 1:2] * _expert(xb, wgu_b_ref, wd_b_ref)
        for c in range(OUT_SLAB):
            ys_ref[pl.ds(c, tile, stride=OUT_SLAB), :] = y[:, c * LANES:(c + 1) * LANES]

    @pl.when(t >= nu_ref[0])
    def _():
        ys_ref[...] = jnp.zeros_like(ys_ref)


def _moe_grouped(xs, tile_ea, tile_eb, n_used, wgu, wd, tile):
    d = D_MODEL
    nt = xs.shape[0] // (tile * IN_SLAB)
    last = lambda nu: jnp.maximum(nu[0] - 1, 0)
    return pl.pallas_call(
        functools.partial(_moe_kernel, tile=tile),
        out_shape=jax.ShapeDtypeStruct((nt * tile * OUT_SLAB, LANES), F32),
        grid_spec=pltpu.PrefetchScalarGridSpec(
            num_scalar_prefetch=3,
            grid=(nt,),
            in_specs=[
                pl.BlockSpec((tile * IN_SLAB, LANES), lambda t, ea, eb, nu: (jnp.minimum(t, last(nu)), 0)),
                pl.BlockSpec((1, d, 2 * D_EXPERT), lambda t, ea, eb, nu: (ea[t], 0, 0)),
                pl.BlockSpec((1, D_EXPERT, d), lambda t, ea, eb, nu: (ea[t], 0, 0)),
                pl.BlockSpec((1, d, 2 * D_EXPERT), lambda t, ea, eb, nu: (eb[t], 0, 0)),
                pl.BlockSpec((1, D_EXPERT, d), lambda t, ea, eb, nu: (eb[t], 0, 0)),
            ],
            out_specs=pl.BlockSpec((tile * OUT_SLAB, LANES), lambda t, ea, eb, nu: (t, 0)),
        ),
        compiler_params=_cparams("arbitrary"),
        name="moe_grouped",
    )(tile_ea, tile_eb, n_used, xs, wgu, wd, wgu, wd)


_PAIR_LO, _PAIR_HI = np.triu_indices(EXPERTS_PER_GROUP, k=1)
_CLASS_LO = np.concatenate([g * EXPERTS_PER_GROUP + _PAIR_LO for g in range(N_GROUPS)]).astype(np.int32)
_CLASS_HI = np.concatenate([g * EXPERTS_PER_GROUP + _PAIR_HI for g in range(N_GROUPS)]).astype(np.int32)


def _moe_plan(counts, cls, rank, n, tile):
    p_max = -(-(n + N_CLASSES * (tile - 1)) // tile) * tile
    nt = p_max // tile
    counts = counts[:N_CLASSES].astype(jnp.int32)
    padded = ((counts + tile - 1) // tile) * tile
    pad_end = jnp.cumsum(padded)
    gstart = pad_end - padded
    class_ids = jnp.arange(N_CLASSES, dtype=jnp.int32)
    pos = rank + jnp.sum(jnp.where(cls[:, None] == class_ids[None, :], gstart[None, :], 0), axis=1)
    n_used = (pad_end[-1] // tile).astype(jnp.int32).reshape(1)
    tile_cls = jnp.sum((pad_end[None, :] <= (jnp.arange(nt, dtype=jnp.int32) * tile)[:, None]).astype(jnp.int32), axis=1)
    tile_cls = jnp.minimum(tile_cls, N_CLASSES - 1)
    tile_ea = jnp.asarray(_CLASS_LO)[tile_cls]
    tile_eb = jnp.asarray(_CLASS_HI)[tile_cls]
    return p_max, pos, gstart + counts, padded - counts, tile_ea, tile_eb, n_used


def _final_kernel(pos_ref, x1_ref, ys_ref, mod_ref, g_ref, o_ref, ybuf, sems, *, tile, n_steps):
    s = pl.program_id(0)
    slot = s % 2

    def slab_copy(src_row, dst_slot, dst_row):
        return pltpu.make_async_copy(ys_ref.at[pl.ds(src_row * OUT_SLAB, OUT_SLAB)],
                                     ybuf.at[dst_slot, pl.ds(dst_row * OUT_SLAB, OUT_SLAB)], sems.at[dst_slot])

    def issue(step, dst_slot):
        def one(r8, carry):
            for u in range(ISSUE_UNROLL):
                r = r8 * ISSUE_UNROLL + u
                slab_copy(pos_ref[step * tile + r], dst_slot, r).start()
            return carry
        lax.fori_loop(0, tile // ISSUE_UNROLL, one, 0)

    @pl.when(s == 0)
    def _():
        issue(0, 0)

    @pl.when(s + 1 < n_steps)
    def _():
        issue(s + 1, 1 - slot)

    def wait_one(r, carry):
        slab_copy(0, slot, 0).wait()
        return carry
    lax.fori_loop(0, tile, wait_one, 0)

    y = jnp.concatenate([ybuf[slot, pl.ds(c, tile, stride=OUT_SLAB), :] for c in range(OUT_SLAB)], axis=1)
    o_ref[...] = x1_ref[...] + mod_ref[0, 5:6, :] * _rms(y, g_ref[...])


def _final(x1, ys, pos, mods, g_post_ffn, tile):
    b, s, d = x1.shape
    n = b * s
    per_b = s // tile
    n_steps = n // tile
    tok = pl.BlockSpec((tile, d), lambda i, *_: (i, 0))
    out = pl.pallas_call(
        functools.partial(_final_kernel, tile=tile, n_steps=n_steps),
        out_shape=jax.ShapeDtypeStruct((n, d), F32),
        grid_spec=pltpu.PrefetchScalarGridSpec(
            num_scalar_prefetch=1,
            grid=(n_steps,),
            in_specs=[
                tok,
                pl.BlockSpec(memory_space=pl.ANY),
                pl.BlockSpec((1, 6, d), lambda i, *_: (i // per_b, 0, 0)),
                pl.BlockSpec((1, d), lambda i, *_: (0, 0)),
            ],
            out_specs=tok,
            scratch_shapes=[pltpu.VMEM((2, tile * OUT_SLAB, LANES), F32), pltpu.SemaphoreType.DMA((2,))],
        ),
        compiler_params=_cparams("arbitrary"),
        name="final",
    )(pos, x1.reshape(n, d), ys, mods, g_post_ffn)
    return out.reshape(b, s, d)


def _t5_bucket(rel):
    half = T5_BUCKETS // 2
    exact = half // 2
    ret = jnp.where(rel > 0, half, 0)
    n = jnp.abs(rel)
    nf = jnp.maximum(n, 1).astype(F32)
    large = exact + (jnp.log(nf / exact) / math.log(T5_MAX_DIST / exact) * (half - exact)).astype(jnp.int32)
    large = jnp.minimum(large, half - 1)
    return ret + jnp.where(n < exact, n, large)


def _toeplitz(u, n_rows, n_cols):
    return jnp.stack([u[..., n_rows - 1 - i:n_rows - 1 - i + n_cols] for i in range(n_rows)], axis=-2)


def _bias_tables(t5_table, rel_table):
    ja = jnp.arange(CHUNK - 1 + A_BAND)
    ua = t5_table[_t5_bucket(ja - (CHUNK - 1) - A_WINDOW)].T.astype(F32)
    jb = jnp.arange(CHUNK - 1 + B_BAND)
    ub = rel_table[:, jnp.clip((CHUNK - 1) - jb + B_REACH, -REL_CLIP, REL_CLIP) + REL_CLIP].astype(F32)
    return _toeplitz(ua, CHUNK, A_BAND), _toeplitz(ub, CHUNK, B_BAND)


def _two_chunk(bias):
    masked = jnp.full(bias.shape[:-1] + (CHUNK,), MASKED, F32)
    return jnp.concatenate([jnp.concatenate([bias, masked], axis=-1), jnp.concatenate([masked, bias], axis=-1)], axis=-2)


def _pair_tables(bias_a, bias_b, sinks):
    a2, b2 = _two_chunk(bias_a), _two_chunk(bias_b)
    stack2 = lambda x, h0, h1: jnp.concatenate([x[h0], x[h1]], axis=0)
    bias2_a = jnp.stack([jnp.concatenate([stack2(a2, 4 * kv, 4 * kv + 2), stack2(a2, 4 * kv + 1, 4 * kv + 3)], axis=1)
                         for kv in range(A_KV_HEADS)])
    bias2_b = jnp.concatenate([b2[0::2], b2[1::2]], axis=-1)
    sk = jnp.broadcast_to(sinks.astype(F32).reshape(A_Q_HEADS, 1, 1), (A_Q_HEADS, ATT_BLOCK, 1))
    sinks2 = jnp.stack([jnp.stack([stack2(sk, 4 * kv, 4 * kv + 2), stack2(sk, 4 * kv + 1, 4 * kv + 3)])
                        for kv in range(A_KV_HEADS)])
    return bias2_a, sinks2, bias2_b


def _moe_and_final(x1, h2e, cls, rank, counts, mods, g_post_ffn, wgu, wd, seq_tile, moe_tile):
    b, s, d = x1.shape
    n = b * s
    p_max, pos, pad_start, pad_len, tile_ea, tile_eb, n_used = _moe_plan(
        counts.reshape(CLASS_ROWS), cls.reshape(n), rank.reshape(n), n, moe_tile)
    xs = _dispatch(h2e, pos, pad_start, pad_len, n_used, p_max, seq_tile, moe_tile)
    ys = _moe_grouped(xs, tile_ea, tile_eb, n_used, wgu, wd, moe_tile)
    return _final(x1, ys, pos, mods, g_post_ffn, seq_tile)


def kernel(x_prompt, x_sample, c_prompt, c_sample, cache_a_k, cache_a_v, cache_b_k, cache_b_v, w_ada, b_ada, g_pre_mix, g_post_mix, g_pre_ffn, g_post_ffn, w_in, a_sinks, t5_table, b_rel_table, w_proj_a, w_proj_b, w_gate, b_gate, w_o, w_route_g, b_route_g, w_route_e, b_route_e, w_e_gate, w_e_up, w_e_down):
    depth = w_in.shape[0]
    assert depth == 1
    l = 0
    bp, sp, d = x_prompt.shape
    bs, ts, _ = x_sample.shape

    mods = _ada(jnp.concatenate([c_prompt, c_sample], axis=0), w_ada[l], b_ada[l]).reshape(bp + bs, 6, d)
    mods_p, mods_s = mods[:bp], mods[bp:]

    w_in_bf = w_in[l].astype(BF16)
    wg, wpa, wpb, wo = w_gate[l].astype(BF16), w_proj_a[l].astype(BF16), w_proj_b[l].astype(BF16), w_o[l].astype(BF16)
    bg = b_gate[l].reshape(1, 2 * d)
    pad_rows = ROUTE_ROWS - N_EXPERTS - N_GROUPS
    wr = jnp.concatenate([w_route_e[l].T, w_route_g[l].T, jnp.zeros((pad_rows, d), F32)], axis=0).astype(BF16)
    br = jnp.concatenate([b_route_e[l], b_route_g[l], jnp.zeros((pad_rows,), F32)]).reshape(ROUTE_ROWS, 1)
    wgu = jnp.concatenate([w_e_gate[l], w_e_up[l]], axis=-1).astype(BF16)
    wd = w_e_down[l].astype(BF16)
    g1, g1p, g2, g2p = (g[l].reshape(1, d) for g in (g_pre_mix, g_post_mix, g_pre_ffn, g_post_ffn))

    bias_a, bias_b = _bias_tables(t5_table, b_rel_table[l])
    bias2_a, sinks2, bias2_b = _pair_tables(bias_a, bias_b, a_sinks[l])
    bias_a = bias_a.reshape(A_KV_HEADS, A_GROUP * CHUNK, A_BAND)
    sinks = jnp.broadcast_to(a_sinks[l].reshape(A_KV_HEADS, A_GROUP, 1, 1), (A_KV_HEADS, A_GROUP, CHUNK, 1))
    sinks = sinks.reshape(A_KV_HEADS, A_GROUP * CHUNK, 1).astype(F32)

    qa, ka, va, qb, kb, vb, sak_p, sav_p, sbk_p, sbv_p = _pre_prompt(x_prompt, mods_p, g1, w_in_bf)
    oa, ob = _attn_prompt(qa, ka, va, qb, kb, vb, bias2_a, sinks2, bias2_b)
    x1, h2e, cls, rank, counts = _post(x_prompt, oa, ob, mods_p, g1, g1p, g2, wg, bg, wpa, wpb, wo, wr, br, SEQ_TILE)
    y_prompt = _moe_and_final(x1, h2e, cls, rank, counts, mods_p, g2p, wgu, wd, SEQ_TILE, 256)

    la, lb = cache_a_k.shape[2], cache_b_k.shape[2]
    cak = cache_a_k[l].reshape(bs, la, KVA_W)
    cav = cache_a_v[l].reshape(bs, la, KVA_W)
    cbk = cache_b_k[l].reshape(bs, lb, B_W)
    cbv = cache_b_v[l].reshape(bs, lb, B_W)
    qa, ka, va, qb, kb, vb, sak_s, sav_s, sbk_s, sbv_s = _pre_sample(x_sample, mods_s, g1, w_in_bf, cak, cav, cbk, cbv)
    oa, ob = _attn_sample(qa, ka, va, qb, kb, vb, bias_a, sinks, bias_b)
    x1, h2e, cls, rank, counts = _post(x_sample, oa, ob, mods_s, g1, g1p, g2, wg, bg, wpa, wpb, wo, wr, br, ts)
    y_sample = _moe_and_final(x1, h2e, cls, rank, counts, mods_s, g2p, wgu, wd, ts, 64)

    a_state = lambda v, b, r: v.reshape(1, b, r, A_KV_HEADS, HEAD_DIM)
    b_state = lambda v, b, r: v.reshape(1, b, r, B_HEADS, HEAD_DIM)
    return (y_prompt, y_sample,
            a_state(sak_p, bp, A_WINDOW), a_state(sav_p, bp, A_WINDOW),
            b_state(sbk_p, bp, B_REACH), b_state(sbv_p, bp, B_REACH),
            a_state(sak_s, bs, la), a_state(sav_s, bs, la),
            b_state(sbk_s, bs, lb), b_state(sbv_s, bs, lb))
```

```python
import functools
import math

import numpy as np
import jax
import jax.numpy as jnp
from jax import lax
from jax.experimental import pallas as pl
from jax.experimental.pallas import tpu as pltpu

D_MODEL = 1024
CHUNK = 64
HEAD_DIM = 64
A_Q_HEADS = 8
A_KV_HEADS = 2
A_GROUP = A_Q_HEADS // A_KV_HEADS
A_WINDOW = 128
A_BACK = A_WINDOW // CHUNK
B_HEADS = 8
B_BACK = 8
B_REACH = B_BACK * CHUNK
REL_CLIP = 256
T5_BUCKETS = 32
T5_MAX_DIST = 128
N_GROUPS = 4
EXPERTS_PER_GROUP = 8
N_EXPERTS = N_GROUPS * EXPERTS_PER_GROUP
D_EXPERT = D_MODEL // 4
EPS = 1e-6

QA_W = A_Q_HEADS * HEAD_DIM
KVA_W = A_KV_HEADS * HEAD_DIM
B_W = B_HEADS * HEAD_DIM
IN_W = QA_W + 2 * KVA_W + 3 * B_W
A_BAND = A_WINDOW + CHUNK
B_BAND = B_REACH + CHUNK

PAIRS_PER_GROUP = EXPERTS_PER_GROUP * (EXPERTS_PER_GROUP - 1) // 2
N_CLASSES = N_GROUPS * PAIRS_PER_GROUP
ROUTE_ROWS = 40
CLASS_ROWS = 128
LANES = 128
IN_SLAB = D_MODEL // LANES + 1
OUT_SLAB = D_MODEL // LANES

F32 = jnp.float32
BF16 = jnp.bfloat16

VMEM_LIMIT_BYTES = 56 * 1024 * 1024

SEQ_TILE = 512
ISSUE_UNROLL = 8


def _cparams(*sem):
    return pltpu.CompilerParams(dimension_semantics=sem, vmem_limit_bytes=VMEM_LIMIT_BYTES)


def _norm_mod(x, g, scale, shift):
    y = x * lax.rsqrt(jnp.mean(x * x, axis=-1, keepdims=True) + EPS)
    return (y * g) * (1.0 + scale) + shift


def _rms(x, g):
    return (x * lax.rsqrt(jnp.mean(x * x, axis=-1, keepdims=True) + EPS)) * g


def _ada_kernel(c_ref, w_ref, b_ref, o_ref):
    c = c_ref[...]
    s = (c * jax.nn.sigmoid(c)).astype(BF16)
    o_ref[...] = jnp.dot(s, w_ref[...].astype(BF16), preferred_element_type=F32) + b_ref[...]


def _ada(c, w_ada, b_ada):
    n, d = c.shape
    wn = w_ada.shape[1]
    tn = 512
    return pl.pallas_call(
        _ada_kernel,
        out_shape=jax.ShapeDtypeStruct((n, wn), F32),
        grid=(wn // tn,),
        in_specs=[
            pl.BlockSpec((n, d), lambda j: (0, 0)),
            pl.BlockSpec((d, tn), lambda j: (0, j)),
            pl.BlockSpec((1, tn), lambda j: (0, j)),
        ],
        out_specs=pl.BlockSpec((n, tn), lambda j: (0, j)),
        compiler_params=_cparams("arbitrary"),
        name="ada",
    )(c, w_ada, b_ada.reshape(1, wn))


_COL_QA = (0, QA_W)
_COL_KA = (QA_W, QA_W + KVA_W)
_COL_VA = (QA_W + KVA_W, QA_W + 2 * KVA_W)
_COL_QB = (QA_W + 2 * KVA_W, QA_W + 2 * KVA_W + B_W)
_COL_KB = (_COL_QB[1], _COL_QB[1] + B_W)
_COL_VB = (_COL_KB[1], _COL_KB[1] + B_W)
Q_SCALE = HEAD_DIM ** -0.5


def _project(x_ref, mod_ref, g_ref, w_ref):
    h = _norm_mod(x_ref[0], g_ref[...], mod_ref[0, 1:2, :], mod_ref[0, 0:1, :])
    return jnp.dot(h.astype(BF16), w_ref[...], preferred_element_type=F32)


def _cols(p, c):
    return p[:, c[0]:c[1]]


def _pre_prompt_kernel(x_ref, mod_ref, g_ref, w_ref,
                       qa_ref, ka_ref, va_ref, qb_ref, kb_ref, vb_ref,
                       sak_ref, sav_ref, sbk_ref, sbv_ref, *, n_tiles, tile):
    p = _project(x_ref, mod_ref, g_ref, w_ref)
    qa_ref[0] = (_cols(p, _COL_QA) * Q_SCALE).astype(BF16)
    ka_ref[0] = _cols(p, _COL_KA).astype(BF16)
    va_ref[0] = _cols(p, _COL_VA).astype(BF16)
    qb_ref[0] = (_cols(p, _COL_QB) * Q_SCALE).astype(BF16)
    kb_ref[0] = _cols(p, _COL_KB).astype(BF16)
    vb_ref[0] = _cols(p, _COL_VB).astype(BF16)

    @pl.when(pl.program_id(1) == n_tiles - 1)
    def _():
        sak_ref[0] = _cols(p, _COL_KA)[tile - A_WINDOW:, :]
        sav_ref[0] = _cols(p, _COL_VA)[tile - A_WINDOW:, :]
        sbk_ref[0] = _cols(p, _COL_KB)[tile - B_REACH:, :]
        sbv_ref[0] = _cols(p, _COL_VB)[tile - B_REACH:, :]


def _pre_prompt(x, mods, g_pre, w_in_bf):
    b, s, d = x.shape
    tile = SEQ_TILE
    assert s % tile == 0 and tile >= B_REACH and s >= B_REACH
    nt = s // tile
    tok = lambda w: pl.BlockSpec((1, tile, w), lambda i, j: (i, j, 0))
    state = lambda r, w: pl.BlockSpec((1, r, w), lambda i, j: (i, 0, 0))
    return pl.pallas_call(
        functools.partial(_pre_prompt_kernel, n_tiles=nt, tile=tile),
        out_shape=(
            jax.ShapeDtypeStruct((b, s, QA_W), BF16),
            jax.ShapeDtypeStruct((b, s, KVA_W), BF16),
            jax.ShapeDtypeStruct((b, s, KVA_W), BF16),
            jax.ShapeDtypeStruct((b, s, B_W), BF16),
            jax.ShapeDtypeStruct((b, s, B_W), BF16),
            jax.ShapeDtypeStruct((b, s, B_W), BF16),
            jax.ShapeDtypeStruct((b, A_WINDOW, KVA_W), F32),
            jax.ShapeDtypeStruct((b, A_WINDOW, KVA_W), F32),
            jax.ShapeDtypeStruct((b, B_REACH, B_W), F32),
            jax.ShapeDtypeStruct((b, B_REACH, B_W), F32),
        ),
        grid=(b, nt),
        in_specs=[
            tok(d),
            pl.BlockSpec((1, 6, d), lambda i, j: (i, 0, 0)),
            pl.BlockSpec((1, d), lambda i, j: (0, 0)),
            pl.BlockSpec((d, IN_W), lambda i, j: (0, 0)),
        ],
        out_specs=(
            tok(QA_W), tok(KVA_W), tok(KVA_W), tok(B_W), tok(B_W), tok(B_W),
            state(A_WINDOW, KVA_W), state(A_WINDOW, KVA_W), state(B_REACH, B_W), state(B_REACH, B_W),
        ),
        compiler_params=_cparams("parallel", "arbitrary"),
        name="pre_prompt",
    )(x, mods, g_pre, w_in_bf)


def _pre_sample_kernel(x_ref, mod_ref, g_ref, w_ref, cak_ref, cav_ref, cbk_ref, cbv_ref,
                       qa_ref, ka_ref, va_ref, qb_ref, kb_ref, vb_ref,
                       sak_ref, sav_ref, sbk_ref, sbv_ref, *, t, la, lb):
    p = _project(x_ref, mod_ref, g_ref, w_ref)
    qa_ref[0] = (_cols(p, _COL_QA) * Q_SCALE).astype(BF16)
    qb_ref[0] = (_cols(p, _COL_QB) * Q_SCALE).astype(BF16)
    for cache_ref, band_ref, state_ref, col, hist in (
            (cak_ref, ka_ref, sak_ref, _COL_KA, la), (cav_ref, va_ref, sav_ref, _COL_VA, la),
            (cbk_ref, kb_ref, sbk_ref, _COL_KB, lb), (cbv_ref, vb_ref, sbv_ref, _COL_VB, lb)):
        new = _cols(p, col)
        cache = cache_ref[0]
        band_ref[0, 0:hist, :] = cache.astype(BF16)
        band_ref[0, hist:hist + t, :] = new.astype(BF16)
        state_ref[0, 0:hist - t, :] = cache[t:, :]
        state_ref[0, hist - t:hist, :] = new


def _pre_sample(x, mods, g_pre, w_in_bf, cak, cav, cbk, cbv):
    b, t, d = x.shape
    la, lb = cak.shape[1], cbk.shape[1]
    assert t == CHUNK and la == A_WINDOW and lb == B_REACH
    per_b = lambda r, w: pl.BlockSpec((1, r, w), lambda i: (i, 0, 0))
    return pl.pallas_call(
        functools.partial(_pre_sample_kernel, t=t, la=la, lb=lb),
        out_shape=(
            jax.ShapeDtypeStruct((b, t, QA_W), BF16),
            jax.ShapeDtypeStruct((b, la + t, KVA_W), BF16),
            jax.ShapeDtypeStruct((b, la + t, KVA_W), BF16),
            jax.ShapeDtypeStruct((b, t, B_W), BF16),
            jax.ShapeDtypeStruct((b, lb + t, B_W), BF16),
            jax.ShapeDtypeStruct((b, lb + t, B_W), BF16),
            jax.ShapeDtypeStruct((b, la, KVA_W), F32),
            jax.ShapeDtypeStruct((b, la, KVA_W), F32),
            jax.ShapeDtypeStruct((b, lb, B_W), F32),
            jax.ShapeDtypeStruct((b, lb, B_W), F32),
        ),
        grid=(b,),
        in_specs=[
            per_b(t, d),
            per_b(6, d),
            pl.BlockSpec((1, d), lambda i: (0, 0)),
            pl.BlockSpec((d, IN_W), lambda i: (0, 0)),
            per_b(la, KVA_W), per_b(la, KVA_W), per_b(lb, B_W), per_b(lb, B_W),
        ],
        out_specs=(
            per_b(t, QA_W), per_b(la + t, KVA_W), per_b(la + t, KVA_W),
            per_b(t, B_W), per_b(lb + t, B_W), per_b(lb + t, B_W),
            per_b(la, KVA_W), per_b(la, KVA_W), per_b(lb, B_W), per_b(lb, B_W),
        ),
        compiler_params=_cparams("parallel"),
        name="pre_sample",
    )(x, mods, g_pre, w_in_bf, cak, cav, cbk, cbv)


def _attend_a(q, k, v, bias_ref, sink_ref, nk):
    boff = A_BAND - nk
    outs = []
    for kv in range(A_KV_HEADS):
        qs = jnp.concatenate(
            [q[:, (kv * A_GROUP + g) * HEAD_DIM:(kv * A_GROUP + g + 1) * HEAD_DIM] for g in range(A_GROUP)], axis=0)
        kh = k[:, kv * HEAD_DIM:(kv + 1) * HEAD_DIM]
        vh = v[:, kv * HEAD_DIM:(kv + 1) * HEAD_DIM]
        s = lax.dot_general(qs, kh, (((1,), (1,)), ((), ())), preferred_element_type=F32)
        s = s + bias_ref[kv, :, boff:boff + nk]
        sk = sink_ref[kv]
        mx = jnp.maximum(jnp.max(s, axis=-1, keepdims=True), sk)
        e = jnp.exp(s - mx)
        den = jnp.sum(e, axis=-1, keepdims=True) + jnp.exp(sk - mx)
        o = jnp.dot(e.astype(BF16), vh, preferred_element_type=F32) / den
        outs.extend(o[g * CHUNK:(g + 1) * CHUNK, :] for g in range(A_GROUP))
    return jnp.concatenate(outs, axis=1).astype(BF16)


def _attend_b(q, k, v, bias_ref, nk):
    boff = B_BAND - nk
    outs = []
    for h in range(B_HEADS):
        sl = slice(h * HEAD_DIM, (h + 1) * HEAD_DIM)
        s = lax.dot_general(q[:, sl], k[:, sl], (((1,), (1,)), ((), ())), preferred_element_type=F32)
        s = s + bias_ref[h, :, boff:boff + nk]
        mx = jnp.max(s, axis=-1, keepdims=True)
        e = jnp.exp(s - mx)
        den = jnp.sum(e, axis=-1, keepdims=True)
        outs.append(jnp.dot(e.astype(BF16), v[:, sl], preferred_element_type=F32) / den)
    return jnp.concatenate(outs, axis=1).astype(BF16)


MASKED = -1e30
PAIR_W = 2 * HEAD_DIM
ATT_BLOCK = 2 * CHUNK
A_COLS = A_WINDOW + ATT_BLOCK
B_COLS = B_REACH + ATT_BLOCK


def _pair_attend(q2, k2, v2, bias, sinks, first_valid):
    c = k2.shape[0] // 2
    s = lax.dot_general(q2, k2, (((1,), (1,)), ((), ())), preferred_element_type=F32) + bias
    es, dens = [], []
    for half in range(2):
        sh = s[:, half * c:(half + 1) * c]
        if first_valid > 0:
            sh = jnp.where(lax.broadcasted_iota(jnp.int32, sh.shape, 1) >= first_valid, sh, MASKED)
        mx = jnp.max(sh, axis=-1, keepdims=True)
        if sinks is not None:
            mx = jnp.maximum(mx, sinks[half])
        e = jnp.exp(sh - mx)
        den = jnp.sum(e, axis=-1, keepdims=True)
        if sinks is not None:
            den = den + jnp.exp(sinks[half] - mx)
        es.append(e.astype(BF16))
        dens.append(den)
    o = jnp.dot(jnp.concatenate(es, axis=1), v2, preferred_element_type=F32)
    lane = lax.broadcasted_iota(jnp.int32, o.shape, 1)
    return o / jnp.where(lane < HEAD_DIM, dens[0], dens[1])


def _attn_prompt_kernel(qa_ref, ka_ref, va_ref, qb_ref, kb_ref, vb_ref, ba_ref, sk_ref, bb_ref, lo_ref, hi_ref,
                        oa_ref, ob_ref, ak_ref, av_ref, bk_ref, bv_ref, *, tile):
    j = pl.program_id(1)

    @pl.when(j == 0)
    def _():
        lo, hi = lo_ref[:, 0:PAIR_W], hi_ref[:, 0:PAIR_W]
        for src, dst in ((ka_ref, ak_ref), (va_ref, av_ref)):
            dst[:, 0:A_WINDOW, :] = jnp.zeros((4, A_WINDOW, PAIR_W), BF16)
            x = src[0]
            swapped = jnp.concatenate([x[:, HEAD_DIM:], x[:, :HEAD_DIM]], axis=1)
            dst[0, A_WINDOW:, :] = x * lo
            dst[1, A_WINDOW:, :] = swapped * hi
            dst[2, A_WINDOW:, :] = swapped * lo
            dst[3, A_WINDOW:, :] = x * hi
        for src, dst in ((kb_ref, bk_ref), (vb_ref, bv_ref)):
            dst[:, 0:B_REACH, :] = jnp.zeros((2, B_REACH, B_W), BF16)
            x = src[0]
            dst[0, B_REACH:, :] = x * lo_ref[...]
            dst[1, B_REACH:, :] = x * hi_ref[...]

    def block(g, r0, first_a, first_b):
        rows = pl.ds(r0, ATT_BLOCK)
        band_a = pl.ds(g, A_COLS)
        band_b = pl.ds(g, B_COLS)
        for kv in range(A_KV_HEADS):
            c0, c1 = slice(2 * kv * PAIR_W, (2 * kv + 1) * PAIR_W), slice((2 * kv + 1) * PAIR_W, (2 * kv + 2) * PAIR_W)
            q2 = jnp.concatenate([qa_ref[0, rows, c0], qa_ref[0, rows, c1]], axis=0)
            k2 = jnp.concatenate([ak_ref[2 * kv, band_a, :], ak_ref[2 * kv + 1, band_a, :]], axis=0)
            v2 = jnp.concatenate([av_ref[2 * kv, band_a, :], av_ref[2 * kv + 1, band_a, :]], axis=0)
            o = _pair_attend(q2, k2, v2, ba_ref[kv], (sk_ref[kv, 0], sk_ref[kv, 1]), first_a)
            oa_ref[0, rows, c0] = o[0:ATT_BLOCK].astype(BF16)
            oa_ref[0, rows, c1] = o[ATT_BLOCK:].astype(BF16)
        for p in range(B_HEADS // 2):
            cols = slice(p * PAIR_W, (p + 1) * PAIR_W)
            k2 = jnp.concatenate([bk_ref[0, band_b, cols], bk_ref[1, band_b, cols]], axis=0)
            v2 = jnp.concatenate([bv_ref[0, band_b, cols], bv_ref[1, band_b, cols]], axis=0)
            o = _pair_attend(qb_ref[0, rows, cols], k2, v2, bb_ref[p], None, first_b)
            ob_ref[0, rows, cols] = o.astype(BF16)

    n_blocks = tile // ATT_BLOCK

    @pl.when(j == 0)
    def _():
        for blk in range(n_blocks):
            g = blk * ATT_BLOCK
            block(g, g, max(A_WINDOW - g, 0), max(B_REACH - g, 0))

    @pl.when(j > 0)
    def _():
        def body(blk, carry):
            r0 = pl.multiple_of(blk * ATT_BLOCK, ATT_BLOCK)
            block(pl.multiple_of(j * tile + r0, ATT_BLOCK), r0, 0, 0)
            return carry
        lax.fori_loop(0, n_blocks, body, 0)


def _attn_prompt(qa, ka, va, qb, kb, vb, bias2_a, sinks2, bias2_b):
    b, s, _ = qa.shape
    tile = SEQ_TILE
    assert s % tile == 0 and tile % ATT_BLOCK == 0
    lane = np.arange(B_W) % PAIR_W
    lo = jnp.asarray((lane < HEAD_DIM).astype(np.float32).reshape(1, B_W), BF16)
    hi = jnp.asarray((lane >= HEAD_DIM).astype(np.float32).reshape(1, B_W), BF16)
    tok = lambda w: pl.BlockSpec((1, tile, w), lambda i, j: (i, j, 0))
    seq = lambda w: pl.BlockSpec((1, s, w), lambda i, j: (i, 0, 0))
    const = lambda shp: pl.BlockSpec(shp, lambda i, j: (0,) * len(shp))
    return pl.pallas_call(
        functools.partial(_attn_prompt_kernel, tile=tile),
        out_shape=(jax.ShapeDtypeStruct((b, s, QA_W), BF16), jax.ShapeDtypeStruct((b, s, B_W), BF16)),
        grid=(b, s // tile),
        in_specs=[tok(QA_W), seq(KVA_W), seq(KVA_W), tok(B_W), seq(B_W), seq(B_W),
                  const(bias2_a.shape), const(sinks2.shape), const(bias2_b.shape), const(lo.shape), const(hi.shape)],
        out_specs=(tok(QA_W), tok(B_W)),
        scratch_shapes=[pltpu.VMEM((4, A_WINDOW + s, PAIR_W), BF16), pltpu.VMEM((4, A_WINDOW + s, PAIR_W), BF16),
                        pltpu.VMEM((2, B_REACH + s, B_W), BF16), pltpu.VMEM((2, B_REACH + s, B_W), BF16)],
        compiler_params=_cparams("arbitrary", "arbitrary"),
        name="attn_prompt",
    )(qa, ka, va, qb, kb, vb, bias2_a, sinks2, bias2_b, lo, hi)


def _attn_sample_kernel(qa_ref, ka_ref, va_ref, qb_ref, kb_ref, vb_ref, ba_ref, sk_ref, bb_ref, oa_ref, ob_ref):
    oa_ref[0] = _attend_a(qa_ref[0], ka_ref[0], va_ref[0], ba_ref, sk_ref, A_BAND)
    ob_ref[0] = _attend_b(qb_ref[0], kb_ref[0], vb_ref[0], bb_ref, B_BAND)


def _attn_sample(qa, ka, va, qb, kb, vb, bias_a, sinks, bias_b):
    b, t, _ = qa.shape
    per_b = lambda r, w: pl.BlockSpec((1, r, w), lambda i: (i, 0, 0))
    const = lambda shp: pl.BlockSpec(shp, lambda i: (0,) * len(shp))
    return pl.pallas_call(
        _attn_sample_kernel,
        out_shape=(jax.ShapeDtypeStruct((b, t, QA_W), BF16), jax.ShapeDtypeStruct((b, t, B_W), BF16)),
        grid=(b,),
        in_specs=[per_b(t, QA_W), per_b(A_BAND, KVA_W), per_b(A_BAND, KVA_W),
                  per_b(t, B_W), per_b(B_BAND, B_W), per_b(B_BAND, B_W),
                  const(bias_a.shape), const(sinks.shape), const(bias_b.shape)],
        out_specs=(per_b(t, QA_W), per_b(t, B_W)),
        compiler_params=_cparams("parallel"),
        name="attn_sample",
    )(qa, ka, va, qb, kb, vb, bias_a, sinks, bias_b)


def _route(lt):
    t = lt.shape[1]
    el = lt[0:N_EXPERTS]
    gl = lt[N_EXPERTS:N_EXPERTS + N_GROUPS]
    gmax = jnp.max(gl, axis=0, keepdims=True)
    gi = lax.broadcasted_iota(jnp.int32, (N_GROUPS, t), 0)
    gidx = jnp.min(jnp.where(gl == gmax, gi, N_GROUPS), axis=0, keepdims=True)
    g_w = 1.0 / jnp.sum(jnp.exp(gl - gmax), axis=0, keepdims=True)
    e_sel = el[(N_GROUPS - 1) * EXPERTS_PER_GROUP:]
    for g in range(N_GROUPS - 2, -1, -1):
        e_sel = jnp.where(gidx == g, el[g * EXPERTS_PER_GROUP:(g + 1) * EXPERTS_PER_GROUP], e_sel)
    ei = lax.broadcasted_iota(jnp.int32, (EXPERTS_PER_GROUP, t), 0)
    m1 = jnp.max(e_sel, axis=0, keepdims=True)
    i1 = jnp.min(jnp.where(e_sel == m1, ei, EXPERTS_PER_GROUP), axis=0, keepdims=True)
    rest = jnp.where(ei == i1, -jnp.inf, e_sel)
    m2 = jnp.max(rest, axis=0, keepdims=True)
    i2 = jnp.min(jnp.where(rest == m2, ei, EXPERTS_PER_GROUP), axis=0, keepdims=True)
    ex = jnp.exp(m2 - m1)
    den = 1.0 + ex
    w1 = g_w * (1.0 / den)
    w2 = g_w * (ex / den)
    lo = jnp.minimum(i1, i2)
    hi = jnp.maximum(i1, i2)
    first_is_lo = i1 < i2
    w_lo = jnp.where(first_is_lo, w1, w2)
    w_hi = jnp.where(first_is_lo, w2, w1)
    pair = ((lo * (2 * EXPERTS_PER_GROUP - 1 - lo)) >> 1) + (hi - lo - 1)
    return gidx * PAIRS_PER_GROUP + pair, w_lo, w_hi


def _col_from_row(w_row, n):
    ri = lax.broadcasted_iota(jnp.int32, (n, n), 0)
    ci = lax.broadcasted_iota(jnp.int32, (n, n), 1)
    return jnp.sum(jnp.where(ri == ci, jnp.broadcast_to(w_row, (n, n)), 0.0), axis=1, keepdims=True)


def _rank_in_class(cls, run_ref, tri_ref):
    t = cls.shape[1]
    onehot = lax.broadcasted_iota(jnp.int32, (CLASS_ROWS, t), 0) == cls
    ones = jnp.where(onehot, 1.0, 0.0)
    before = jnp.dot(ones.astype(BF16), tri_ref[...], preferred_element_type=F32)
    run = run_ref[...]
    rank = jnp.sum(jnp.where(onehot, before + run, 0.0), axis=0, keepdims=True)
    run_ref[...] = run + jnp.sum(ones, axis=1, keepdims=True)
    return rank.astype(jnp.int32)


def _post_kernel(x_ref, oa_ref, ob_ref, mod_ref, g1_ref, g1p_ref, g2_ref,
                 wg_ref, bg_ref, wpa_ref, wpb_ref, wo_ref, wr_ref, br_ref, tri_ref,
                 x1_ref, h2_ref, cls_ref, rank_ref, cnt_ref, run_ref):
    @pl.when((pl.program_id(0) == 0) & (pl.program_id(1) == 0))
    def _():
        run_ref[...] = jnp.zeros_like(run_ref)

    x = x_ref[0]
    h = _norm_mod(x, g1_ref[...], mod_ref[0, 1:2, :], mod_ref[0, 0:1, :]).astype(BF16)
    gates = jax.nn.sigmoid(jnp.dot(h, wg_ref[...], preferred_element_type=F32) + bg_ref[...])
    pa = jnp.dot(oa_ref[0], wpa_ref[...], preferred_element_type=F32)
    pb = jnp.dot(ob_ref[0], wpb_ref[...], preferred_element_type=F32)
    mixed = gates[:, :D_MODEL] * pa + gates[:, D_MODEL:] * pb
    y = jnp.dot(mixed.astype(BF16), wo_ref[...], preferred_element_type=F32)
    x1 = x + mod_ref[0, 2:3, :] * _rms(y, g1p_ref[...])
    x1_ref[0] = x1
    h2 = _norm_mod(x1, g2_ref[...], mod_ref[0, 4:5, :], mod_ref[0, 3:4, :]).astype(BF16)
    lt = lax.dot_general(wr_ref[...], h2, (((1,), (1,)), ((), ())), preferred_element_type=F32) + br_ref[...]
    cls, w_lo, w_hi = _route(lt)
    cls_ref[0, 0] = cls
    rank_ref[0, 0] = _rank_in_class(cls, run_ref, tri_ref)
    cnt_ref[...] = run_ref[...]
    t = x.shape[0]
    lane = lax.broadcasted_iota(jnp.int32, (t, LANES), 1)
    h2f = h2.astype(F32)
    for c in range(D_MODEL // LANES):
        h2_ref[pl.ds(c, t, stride=IN_SLAB), :] = h2f[:, c * LANES:(c + 1) * LANES]
    h2_ref[pl.ds(D_MODEL // LANES, t, stride=IN_SLAB), :] = jnp.where(
        lane == 0, _col_from_row(w_lo, t), jnp.where(lane == 1, _col_from_row(w_hi, t), 0.0))


def _post(x, oa, ob, mods, g_pre_mix, g_post_mix, g_pre_ffn, wg, bg, wpa, wpb, wo, wr, br, tile):
    b, s, d = x.shape
    nt = s // tile
    tri = (jnp.arange(tile)[:, None] < jnp.arange(tile)[None, :]).astype(BF16)
    tok = lambda w: pl.BlockSpec((1, tile, w), lambda i, j: (i, j, 0))
    const = lambda shp: pl.BlockSpec(shp, lambda i, j: (0,) * len(shp))
    per_tile = pl.BlockSpec((1, 1, 1, tile), lambda i, j: (i, j, 0, 0))
    return pl.pallas_call(
        _post_kernel,
        out_shape=(
            jax.ShapeDtypeStruct((b, s, d), F32),
            jax.ShapeDtypeStruct((b * s * IN_SLAB, LANES), F32),
            jax.ShapeDtypeStruct((b, nt, 1, tile), jnp.int32),
            jax.ShapeDtypeStruct((b, nt, 1, tile), jnp.int32),
            jax.ShapeDtypeStruct((CLASS_ROWS, 1), F32),
        ),
        grid=(b, nt),
        in_specs=[
            tok(d), tok(QA_W), tok(B_W),
            pl.BlockSpec((1, 6, d), lambda i, j: (i, 0, 0)),
            const((1, d)), const((1, d)), const((1, d)),
            const(wg.shape), const(bg.shape), const(wpa.shape), const(wpb.shape), const(wo.shape),
            const(wr.shape), const(br.shape), const(tri.shape),
        ],
        out_specs=(tok(d), pl.BlockSpec((tile * IN_SLAB, LANES), lambda i, j: (i * nt + j, 0)),
                   per_tile, per_tile, const((CLASS_ROWS, 1))),
        scratch_shapes=[pltpu.VMEM((CLASS_ROWS, 1), F32)],
        compiler_params=_cparams("arbitrary", "arbitrary"),
        name="post",
    )(x, oa, ob, mods, g_pre_mix, g_post_mix, g_pre_ffn, wg, bg, wpa, wpb, wo, wr, br, tri)


def _invert_kernel(upos_ref, tok_ref, *, n):
    def body(i, carry):
        for u in range(ISSUE_UNROLL):
            t = i * ISSUE_UNROLL + u
            tok_ref[upos_ref[t]] = t
        return carry
    lax.fori_loop(0, n // ISSUE_UNROLL, body, 0)


def _invert(upos):
    n = upos.shape[0]
    return pl.pallas_call(
        functools.partial(_invert_kernel, n=n),
        out_shape=jax.ShapeDtypeStruct((n,), jnp.int32),
        in_specs=[pl.BlockSpec(memory_space=pltpu.SMEM)],
        out_specs=pl.BlockSpec(memory_space=pltpu.SMEM),
        name="invert",
    )(upos)


def _expert(xb, wgu_ref, wd_ref):
    gu = jnp.dot(xb, wgu_ref[0], preferred_element_type=F32)
    gate = gu[:, :D_EXPERT]
    he = (gate * jax.nn.sigmoid(gate)) * gu[:, D_EXPERT:]
    return jnp.dot(he.astype(BF16), wd_ref[0], preferred_element_type=F32)


def _moe_kernel(tok_ref, base_ref, ea_ref, eb_ref, nu_ref, h_ref, wgu_a_ref, wd_a_ref, wgu_b_ref, wd_b_ref,
                ys_ref, xbuf, sems, *, tile, n_tok):
    t = pl.program_id(0)
    nu = nu_ref[0]
    slot = t % 2

    def slab_copy(tok, dst_slot, r):
        return pltpu.make_async_copy(h_ref.at[pl.ds(tok * IN_SLAB, IN_SLAB)],
                                     xbuf.at[dst_slot, pl.ds(r * IN_SLAB, IN_SLAB)], sems.at[dst_slot])

    def fetch(step, dst_slot):
        base = base_ref[step]
        for r in range(tile):
            slab_copy(tok_ref[jnp.minimum(base + r, n_tok - 1)], dst_slot, r).start()

    def wait(dst_slot):
        for r in range(tile):
            slab_copy(0, dst_slot, 0).wait()

    @pl.when(t == 0)
    def _():
        fetch(0, 0)

    @pl.when(t < nu)
    def _():
        wait(slot)
        fetch(jnp.minimum(t + 1, nu - 1), 1 - slot)
        lane_tile = lambda c: xbuf[slot, pl.ds(c, tile, stride=IN_SLAB), :]
        xb = jnp.concatenate([lane_tile(c) for c in range(D_MODEL // LANES)], axis=1).astype(BF16)
        w = lane_tile(D_MODEL // LANES)
        y = w[:, 0:1] * _expert(xb, wgu_a_ref, wd_a_ref) + w[:, 1:2] * _expert(xb, wgu_b_ref, wd_b_ref)
        for c in range(OUT_SLAB):
            ys_ref[pl.ds(c, tile, stride=OUT_SLAB), :] = y[:, c * LANES:(c + 1) * LANES]

    @pl.when(t == nu - 1)
    def _():
        wait(1 - slot)

    @pl.when(t >= nu)
    def _():
        ys_ref[...] = jnp.zeros_like(ys_ref)


def _moe_grouped(h_slabs, sorted_tok, tile_base, tile_ea, tile_eb, n_used, wgu, wd, tile, n_tiles):
    d = D_MODEL
    n_tok = sorted_tok.shape[0]
    expert = lambda which, shp: pl.BlockSpec(shp, lambda t, tok, base, ea, eb, nu: ((ea, eb)[which][t], 0, 0))
    return pl.pallas_call(
        functools.partial(_moe_kernel, tile=tile, n_tok=n_tok),
        out_shape=jax.ShapeDtypeStruct((n_tiles * tile * OUT_SLAB, LANES), F32),
        grid_spec=pltpu.PrefetchScalarGridSpec(
            num_scalar_prefetch=5,
            grid=(n_tiles,),
            in_specs=[
                pl.BlockSpec(memory_space=pl.ANY),
                expert(0, (1, d, 2 * D_EXPERT)), expert(0, (1, D_EXPERT, d)),
                expert(1, (1, d, 2 * D_EXPERT)), expert(1, (1, D_EXPERT, d)),
            ],
            out_specs=pl.BlockSpec((tile * OUT_SLAB, LANES), lambda t, *_: (t, 0)),
            scratch_shapes=[pltpu.VMEM((2, tile * IN_SLAB, LANES), F32), pltpu.SemaphoreType.DMA((2,))],
        ),
        compiler_params=_cparams("arbitrary"),
        name="moe_grouped",
    )(sorted_tok, tile_base, tile_ea, tile_eb, n_used, h_slabs, wgu, wd, wgu, wd)


_PAIR_LO, _PAIR_HI = np.triu_indices(EXPERTS_PER_GROUP, k=1)
_CLASS_LO = np.concatenate([g * EXPERTS_PER_GROUP + _PAIR_LO for g in range(N_GROUPS)]).astype(np.int32)
_CLASS_HI = np.concatenate([g * EXPERTS_PER_GROUP + _PAIR_HI for g in range(N_GROUPS)]).astype(np.int32)


def _moe_plan(counts, cls, rank, n, tile):
    p_max = -(-(n + N_CLASSES * (tile - 1)) // tile) * tile
    nt = p_max // tile
    counts = counts[:N_CLASSES].astype(jnp.int32)
    padded = ((counts + tile - 1) // tile) * tile
    pad_end = jnp.cumsum(padded)
    gstart = pad_end - padded
    ustart = jnp.cumsum(counts) - counts
    onehot = cls[:, None] == jnp.arange(N_CLASSES, dtype=jnp.int32)[None, :]
    pos = rank + jnp.sum(jnp.where(onehot, gstart[None, :], 0), axis=1)
    upos = rank + jnp.sum(jnp.where(onehot, ustart[None, :], 0), axis=1)
    n_used = (pad_end[-1] // tile).astype(jnp.int32).reshape(1)
    tiles = jnp.arange(nt, dtype=jnp.int32)
    tile_cls = jnp.minimum(jnp.sum((pad_end[None, :] <= (tiles * tile)[:, None]).astype(jnp.int32), axis=1), N_CLASSES - 1)
    tile_ea = jnp.asarray(_CLASS_LO)[tile_cls]
    tile_eb = jnp.asarray(_CLASS_HI)[tile_cls]
    tile_base = jnp.clip(ustart[tile_cls] + tiles * tile - gstart[tile_cls], 0, n - 1)
    return nt, pos, upos, tile_base, tile_ea, tile_eb, n_used


def _final_kernel(pos_ref, x1_ref, ys_ref, mod_ref, g_ref, o_ref, ybuf, sems, *, tile, n_steps):
    s = pl.program_id(0)
    slot = s % 2

    def slab_copy(src_row, dst_slot, dst_row):
        return pltpu.make_async_copy(ys_ref.at[pl.ds(src_row * OUT_SLAB, OUT_SLAB)],
                                     ybuf.at[dst_slot, pl.ds(dst_row * OUT_SLAB, OUT_SLAB)], sems.at[dst_slot])

    def fetch(step, dst_slot):
        for r in range(tile):
            slab_copy(pos_ref[step * tile + r], dst_slot, r).start()

    def wait(dst_slot):
        for r in range(tile):
            slab_copy(0, dst_slot, 0).wait()

    @pl.when(s == 0)
    def _():
        fetch(0, 0)

    wait(slot)
    fetch(jnp.minimum(s + 1, n_steps - 1), 1 - slot)
    y = jnp.concatenate([ybuf[slot, pl.ds(c, tile, stride=OUT_SLAB), :] for c in range(OUT_SLAB)], axis=1)
    o_ref[...] = x1_ref[...] + mod_ref[0, 5:6, :] * _rms(y, g_ref[...])

    @pl.when(s == n_steps - 1)
    def _():
        wait(1 - slot)


def _final(x1, ys, pos, mods, g_post_ffn, tile):
    b, s, d = x1.shape
    n = b * s
    per_b = s // tile
    n_steps = n // tile
    tok = pl.BlockSpec((tile, d), lambda i, *_: (i, 0))
    out = pl.pallas_call(
        functools.partial(_final_kernel, tile=tile, n_steps=n_steps),
        out_shape=jax.ShapeDtypeStruct((n, d), F32),
        grid_spec=pltpu.PrefetchScalarGridSpec(
            num_scalar_prefetch=1,
            grid=(n_steps,),
            in_specs=[
                tok,
                pl.BlockSpec(memory_space=pl.ANY),
                pl.BlockSpec((1, 6, d), lambda i, *_: (i // per_b, 0, 0)),
                pl.BlockSpec((1, d), lambda i, *_: (0, 0)),
            ],
            out_specs=tok,
            scratch_shapes=[pltpu.VMEM((2, tile * OUT_SLAB, LANES), F32), pltpu.SemaphoreType.DMA((2,))],
        ),
        compiler_params=_cparams("arbitrary"),
        name="final",
    )(pos, x1.reshape(n, d), ys, mods, g_post_ffn)
    return out.reshape(b, s, d)


def _t5_bucket(rel):
    half = T5_BUCKETS // 2
    exact = half // 2
    ret = jnp.where(rel > 0, half, 0)
    n = jnp.abs(rel)
    nf = jnp.maximum(n, 1).astype(F32)
    large = exact + (jnp.log(nf / exact) / math.log(T5_MAX_DIST / exact) * (half - exact)).astype(jnp.int32)
    large = jnp.minimum(large, half - 1)
    return ret + jnp.where(n < exact, n, large)


def _toeplitz(u, n_rows, n_cols):
    return jnp.stack([u[..., n_rows - 1 - i:n_rows - 1 - i + n_cols] for i in range(n_rows)], axis=-2)


def _bias_tables(t5_table, rel_table):
    ja = jnp.arange(CHUNK - 1 + A_BAND)
    ua = t5_table[_t5_bucket(ja - (CHUNK - 1) - A_WINDOW)].T.astype(F32)
    jb = jnp.arange(CHUNK - 1 + B_BAND)
    ub = rel_table[:, jnp.clip((CHUNK - 1) - jb + B_REACH, -REL_CLIP, REL_CLIP) + REL_CLIP].astype(F32)
    return _toeplitz(ua, CHUNK, A_BAND), _toeplitz(ub, CHUNK, B_BAND)


def _two_chunk(bias):
    masked = jnp.full(bias.shape[:-1] + (CHUNK,), MASKED, F32)
    return jnp.concatenate([jnp.concatenate([bias, masked], axis=-1), jnp.concatenate([masked, bias], axis=-1)], axis=-2)


def _pair_tables(bias_a, bias_b, sinks):
    a2, b2 = _two_chunk(bias_a), _two_chunk(bias_b)
    stack2 = lambda x, h0, h1: jnp.concatenate([x[h0], x[h1]], axis=0)
    bias2_a = jnp.stack([jnp.concatenate([stack2(a2, 4 * kv, 4 * kv + 2), stack2(a2, 4 * kv + 1, 4 * kv + 3)], axis=1)
                         for kv in range(A_KV_HEADS)])
    bias2_b = jnp.concatenate([b2[0::2], b2[1::2]], axis=-1)
    sk = jnp.broadcast_to(sinks.astype(F32).reshape(A_Q_HEADS, 1, 1), (A_Q_HEADS, ATT_BLOCK, 1))
    sinks2 = jnp.stack([jnp.stack([stack2(sk, 4 * kv, 4 * kv + 2), stack2(sk, 4 * kv + 1, 4 * kv + 3)])
                        for kv in range(A_KV_HEADS)])
    return bias2_a, sinks2, bias2_b


def _moe_and_final(x1, h2e, cls, rank, counts, mods, g_post_ffn, wgu, wd, seq_tile, moe_tile):
    b, s, d = x1.shape
    n = b * s
    n_tiles, pos, upos, tile_base, tile_ea, tile_eb, n_used = _moe_plan(
        counts.reshape(CLASS_ROWS), cls.reshape(n), rank.reshape(n), n, moe_tile)
    ys = _moe_grouped(h2e, _invert(upos), tile_base, tile_ea, tile_eb, n_used, wgu, wd, moe_tile, n_tiles)
    return _final(x1, ys, pos, mods, g_post_ffn, seq_tile)


def kernel(x_prompt, x_sample, c_prompt, c_sample, cache_a_k, cache_a_v, cache_b_k, cache_b_v, w_ada, b_ada, g_pre_mix, g_post_mix, g_pre_ffn, g_post_ffn, w_in, a_sinks, t5_table, b_rel_table, w_proj_a, w_proj_b, w_gate, b_gate, w_o, w_route_g, b_route_g, w_route_e, b_route_e, w_e_gate, w_e_up, w_e_down):
    depth = w_in.shape[0]
    assert depth == 1
    l = 0
    bp, sp, d = x_prompt.shape
    bs, ts, _ = x_sample.shape

    mods = _ada(jnp.concatenate([c_prompt, c_sample], axis=0), w_ada[l], b_ada[l]).reshape(bp + bs, 6, d)
    mods_p, mods_s = mods[:bp], mods[bp:]

    w_in_bf = w_in[l].astype(BF16)
    wg, wpa, wpb, wo = w_gate[l].astype(BF16), w_proj_a[l].astype(BF16), w_proj_b[l].astype(BF16), w_o[l].astype(BF16)
    bg = b_gate[l].reshape(1, 2 * d)
    pad_rows = ROUTE_ROWS - N_EXPERTS - N_GROUPS
    wr = jnp.concatenate([w_route_e[l].T, w_route_g[l].T, jnp.zeros((pad_rows, d), F32)], axis=0).astype(BF16)
    br = jnp.concatenate([b_route_e[l], b_route_g[l], jnp.zeros((pad_rows,), F32)]).reshape(ROUTE_ROWS, 1)
    wgu = jnp.concatenate([w_e_gate[l], w_e_up[l]], axis=-1).astype(BF16)
    wd = w_e_down[l].astype(BF16)
    g1, g1p, g2, g2p = (g[l].reshape(1, d) for g in (g_pre_mix, g_post_mix, g_pre_ffn, g_post_ffn))

    bias_a, bias_b = _bias_tables(t5_table, b_rel_table[l])
    bias2_a, sinks2, bias2_b = _pair_tables(bias_a, bias_b, a_sinks[l])
    bias_a = bias_a.reshape(A_KV_HEADS, A_GROUP * CHUNK, A_BAND)
    sinks = jnp.broadcast_to(a_sinks[l].reshape(A_KV_HEADS, A_GROUP, 1, 1), (A_KV_HEADS, A_GROUP, CHUNK, 1))
    sinks = sinks.reshape(A_KV_HEADS, A_GROUP * CHUNK, 1).astype(F32)

    qa, ka, va, qb, kb, vb, sak_p, sav_p, sbk_p, sbv_p = _pre_prompt(x_prompt, mods_p, g1, w_in_bf)
    oa, ob = _attn_prompt(qa, ka, va, qb, kb, vb, bias2_a, sinks2, bias2_b)
    x1, h2e, cls, rank, counts = _post(x_prompt, oa, ob, mods_p, g1, g1p, g2, wg, bg, wpa, wpb, wo, wr, br, SEQ_TILE)
    y_prompt = _moe_and_final(x1, h2e, cls, rank, counts, mods_p, g2p, wgu, wd, SEQ_TILE, 256)

    la, lb = cache_a_k.shape[2], cache_b_k.shape[2]
    cak = cache_a_k[l].reshape(bs, la, KVA_W)
    cav = cache_a_v[l].reshape(bs, la, KVA_W)
    cbk = cache_b_k[l].reshape(bs, lb, B_W)
    cbv = cache_b_v[l].reshape(bs, lb, B_W)
    qa, ka, va, qb, kb, vb, sak_s, sav_s, sbk_s, sbv_s = _pre_sample(x_sample, mods_s, g1, w_in_bf, cak, cav, cbk, cbv)
    oa, ob = _attn_sample(qa, ka, va, qb, kb, vb, bias_a, sinks, bias_b)
    x1, h2e, cls, rank, counts = _post(x_sample, oa, ob, mods_s, g1, g1p, g2, wg, bg, wpa, wpb, wo, wr, br, ts)
    y_sample = _moe_and_final(x1, h2e, cls, rank, counts, mods_s, g2p, wgu, wd, ts, 64)

    a_state = lambda v, b, r: v.reshape(1, b, r, A_KV_HEADS, HEAD_DIM)
    b_state = lambda v, b, r: v.reshape(1, b, r, B_HEADS, HEAD_DIM)
    return (y_prompt, y_sample,
            a_state(sak_p, bp, A_WINDOW), a_state(sav_p, bp, A_WINDOW),
            b_state(sbk_p, bp, B_REACH), b_state(sbv_p, bp, B_REACH),
            a_state(sak_s, bs, la), a_state(sav_s, bs, la),
            b_state(sbk_s, bs, lb), b_state(sbv_s, bs, lb))
```

```python
import functools
import math

import numpy as np
import jax
import jax.numpy as jnp
from jax import lax
from jax.experimental import pallas as pl
from jax.experimental.pallas import tpu as pltpu

D_MODEL = 1024
CHUNK = 64
HEAD_DIM = 64
A_Q_HEADS = 8
A_KV_HEADS = 2
A_GROUP = A_Q_HEADS // A_KV_HEADS
A_WINDOW = 128
A_BACK = A_WINDOW // CHUNK
B_HEADS = 8
B_BACK = 8
B_REACH = B_BACK * CHUNK
REL_CLIP = 256
T5_BUCKETS = 32
T5_MAX_DIST = 128
N_GROUPS = 4
EXPERTS_PER_GROUP = 8
N_EXPERTS = N_GROUPS * EXPERTS_PER_GROUP
D_EXPERT = D_MODEL // 4
EPS = 1e-6

QA_W = A_Q_HEADS * HEAD_DIM
KVA_W = A_KV_HEADS * HEAD_DIM
B_W = B_HEADS * HEAD_DIM
IN_W = QA_W + 2 * KVA_W + 3 * B_W
A_BAND = A_WINDOW + CHUNK
B_BAND = B_REACH + CHUNK

PAIRS_PER_GROUP = EXPERTS_PER_GROUP * (EXPERTS_PER_GROUP - 1) // 2
N_CLASSES = N_GROUPS * PAIRS_PER_GROUP
ROUTE_ROWS = 40
CLASS_ROWS = 128
LANES = 128
IN_SLAB = D_MODEL // LANES
OUT_SLAB = D_MODEL // LANES

F32 = jnp.float32
BF16 = jnp.bfloat16

VMEM_LIMIT_BYTES = 56 * 1024 * 1024

SEQ_TILE = 512
ISSUE_UNROLL = 8


def _cparams(*sem):
    return pltpu.CompilerParams(dimension_semantics=sem, vmem_limit_bytes=VMEM_LIMIT_BYTES)


def _norm_mod(x, g, scale, shift):
    y = x * lax.rsqrt(jnp.mean(x * x, axis=-1, keepdims=True) + EPS)
    return (y * g) * (1.0 + scale) + shift


def _rms(x, g):
    return (x * lax.rsqrt(jnp.mean(x * x, axis=-1, keepdims=True) + EPS)) * g


def _ada_kernel(c_ref, w_ref, b_ref, o_ref):
    c = c_ref[...]
    s = (c * jax.nn.sigmoid(c)).astype(BF16)
    o_ref[...] = jnp.dot(s, w_ref[...].astype(BF16), preferred_element_type=F32) + b_ref[...]


def _ada(c, w_ada, b_ada):
    n, d = c.shape
    wn = w_ada.shape[1]
    tn = 512
    return pl.pallas_call(
        _ada_kernel,
        out_shape=jax.ShapeDtypeStruct((n, wn), F32),
        grid=(wn // tn,),
        in_specs=[
            pl.BlockSpec((n, d), lambda j: (0, 0)),
            pl.BlockSpec((d, tn), lambda j: (0, j)),
            pl.BlockSpec((1, tn), lambda j: (0, j)),
        ],
        out_specs=pl.BlockSpec((n, tn), lambda j: (0, j)),
        compiler_params=_cparams("arbitrary"),
        name="ada",
    )(c, w_ada, b_ada.reshape(1, wn))


_COL_QA = (0, QA_W)
_COL_KA = (QA_W, QA_W + KVA_W)
_COL_VA = (QA_W + KVA_W, QA_W + 2 * KVA_W)
_COL_QB = (QA_W + 2 * KVA_W, QA_W + 2 * KVA_W + B_W)
_COL_KB = (_COL_QB[1], _COL_QB[1] + B_W)
_COL_VB = (_COL_KB[1], _COL_KB[1] + B_W)
Q_SCALE = HEAD_DIM ** -0.5


def _project(x_ref, mod_ref, g_ref, w_ref):
    h = _norm_mod(x_ref[0], g_ref[...], mod_ref[0, 1:2, :], mod_ref[0, 0:1, :])
    return jnp.dot(h.astype(BF16), w_ref[...], preferred_element_type=F32)


def _cols(p, c):
    return p[:, c[0]:c[1]]


def _pre_prompt_kernel(x_ref, mod_ref, g_ref, w_ref,
                       qa_ref, ka_ref, va_ref, qb_ref, kb_ref, vb_ref,
                       sak_ref, sav_ref, sbk_ref, sbv_ref, *, n_tiles, tile):
    p = _project(x_ref, mod_ref, g_ref, w_ref)
    qa_ref[0] = (_cols(p, _COL_QA) * Q_SCALE).astype(BF16)
    ka_ref[0] = _cols(p, _COL_KA).astype(BF16)
    va_ref[0] = _cols(p, _COL_VA).astype(BF16)
    qb_ref[0] = (_cols(p, _COL_QB) * Q_SCALE).astype(BF16)
    kb_ref[0] = _cols(p, _COL_KB).astype(BF16)
    vb_ref[0] = _cols(p, _COL_VB).astype(BF16)

    @pl.when(pl.program_id(1) == n_tiles - 1)
    def _():
        sak_ref[0] = _cols(p, _COL_KA)[tile - A_WINDOW:, :]
        sav_ref[0] = _cols(p, _COL_VA)[tile - A_WINDOW:, :]
        sbk_ref[0] = _cols(p, _COL_KB)[tile - B_REACH:, :]
        sbv_ref[0] = _cols(p, _COL_VB)[tile - B_REACH:, :]


def _pre_prompt(x, mods, g_pre, w_in_bf):
    b, s, d = x.shape
    tile = SEQ_TILE
    assert s % tile == 0 and tile >= B_REACH and s >= B_REACH
    nt = s // tile
    tok = lambda w: pl.BlockSpec((1, tile, w), lambda i, j: (i, j, 0))
    state = lambda r, w: pl.BlockSpec((1, r, w), lambda i, j: (i, 0, 0))
    return pl.pallas_call(
        functools.partial(_pre_prompt_kernel, n_tiles=nt, tile=tile),
        out_shape=(
            jax.ShapeDtypeStruct((b, s, QA_W), BF16),
            jax.ShapeDtypeStruct((b, s, KVA_W), BF16),
            jax.ShapeDtypeStruct((b, s, KVA_W), BF16),
            jax.ShapeDtypeStruct((b, s, B_W), BF16),
            jax.ShapeDtypeStruct((b, s, B_W), BF16),
            jax.ShapeDtypeStruct((b, s, B_W), BF16),
            jax.ShapeDtypeStruct((b, A_WINDOW, KVA_W), F32),
            jax.ShapeDtypeStruct((b, A_WINDOW, KVA_W), F32),
            jax.ShapeDtypeStruct((b, B_REACH, B_W), F32),
            jax.ShapeDtypeStruct((b, B_REACH, B_W), F32),
        ),
        grid=(b, nt),
        in_specs=[
            tok(d),
            pl.BlockSpec((1, 6, d), lambda i, j: (i, 0, 0)),
            pl.BlockSpec((1, d), lambda i, j: (0, 0)),
            pl.BlockSpec((d, IN_W), lambda i, j: (0, 0)),
        ],
        out_specs=(
            tok(QA_W), tok(KVA_W), tok(KVA_W), tok(B_W), tok(B_W), tok(B_W),
            state(A_WINDOW, KVA_W), state(A_WINDOW, KVA_W), state(B_REACH, B_W), state(B_REACH, B_W),
        ),
        compiler_params=_cparams("parallel", "arbitrary"),
        name="pre_prompt",
    )(x, mods, g_pre, w_in_bf)


def _pre_sample_kernel(x_ref, mod_ref, g_ref, w_ref, cak_ref, cav_ref, cbk_ref, cbv_ref,
                       qa_ref, ka_ref, va_ref, qb_ref, kb_ref, vb_ref,
                       sak_ref, sav_ref, sbk_ref, sbv_ref, *, t, la, lb):
    p = _project(x_ref, mod_ref, g_ref, w_ref)
    qa_ref[0] = (_cols(p, _COL_QA) * Q_SCALE).astype(BF16)
    qb_ref[0] = (_cols(p, _COL_QB) * Q_SCALE).astype(BF16)
    for cache_ref, band_ref, state_ref, col, hist in (
            (cak_ref, ka_ref, sak_ref, _COL_KA, la), (cav_ref, va_ref, sav_ref, _COL_VA, la),
            (cbk_ref, kb_ref, sbk_ref, _COL_KB, lb), (cbv_ref, vb_ref, sbv_ref, _COL_VB, lb)):
        new = _cols(p, col)
        cache = cache_ref[0]
        band_ref[0, 0:hist, :] = cache.astype(BF16)
        band_ref[0, hist:hist + t, :] = new.astype(BF16)
        state_ref[0, 0:hist - t, :] = cache[t:, :]
        state_ref[0, hist - t:hist, :] = new


def _pre_sample(x, mods, g_pre, w_in_bf, cak, cav, cbk, cbv):
    b, t, d = x.shape
    la, lb = cak.shape[1], cbk.shape[1]
    assert t == CHUNK and la == A_WINDOW and lb == B_REACH
    per_b = lambda r, w: pl.BlockSpec((1, r, w), lambda i: (i, 0, 0))
    return pl.pallas_call(
        functools.partial(_pre_sample_kernel, t=t, la=la, lb=lb),
        out_shape=(
            jax.ShapeDtypeStruct((b, t, QA_W), BF16),
            jax.ShapeDtypeStruct((b, la + t, KVA_W), BF16),
            jax.ShapeDtypeStruct((b, la + t, KVA_W), BF16),
            jax.ShapeDtypeStruct((b, t, B_W), BF16),
            jax.ShapeDtypeStruct((b, lb + t, B_W), BF16),
            jax.ShapeDtypeStruct((b, lb + t, B_W), BF16),
            jax.ShapeDtypeStruct((b, la, KVA_W), F32),
            jax.ShapeDtypeStruct((b, la, KVA_W), F32),
            jax.ShapeDtypeStruct((b, lb, B_W), F32),
            jax.ShapeDtypeStruct((b, lb, B_W), F32),
        ),
        grid=(b,),
        in_specs=[
            per_b(t, d),
            per_b(6, d),
            pl.BlockSpec((1, d), lambda i: (0, 0)),
            pl.BlockSpec((d, IN_W), lambda i: (0, 0)),
            per_b(la, KVA_W), per_b(la, KVA_W), per_b(lb, B_W), per_b(lb, B_W),
        ],
        out_specs=(
            per_b(t, QA_W), per_b(la + t, KVA_W), per_b(la + t, KVA_W),
            per_b(t, B_W), per_b(lb + t, B_W), per_b(lb + t, B_W),
            per_b(la, KVA_W), per_b(la, KVA_W), per_b(lb, B_W), per_b(lb, B_W),
        ),
        compiler_params=_cparams("parallel"),
        name="pre_sample",
    )(x, mods, g_pre, w_in_bf, cak, cav, cbk, cbv)


def _attend_a(q, k, v, bias_ref, sink_ref, nk):
    boff = A_BAND - nk
    outs = []
    for kv in range(A_KV_HEADS):
        qs = jnp.concatenate(
            [q[:, (kv * A_GROUP + g) * HEAD_DIM:(kv * A_GROUP + g + 1) * HEAD_DIM] for g in range(A_GROUP)], axis=0)
        kh = k[:, kv * HEAD_DIM:(kv + 1) * HEAD_DIM]
        vh = v[:, kv * HEAD_DIM:(kv + 1) * HEAD_DIM]
        s = lax.dot_general(qs, kh, (((1,), (1,)), ((), ())), preferred_element_type=F32)
        s = s + bias_ref[kv, :, boff:boff + nk]
        sk = sink_ref[kv]
        mx = jnp.maximum(jnp.max(s, axis=-1, keepdims=True), sk)
        e = jnp.exp(s - mx)
        den = jnp.sum(e, axis=-1, keepdims=True) + jnp.exp(sk - mx)
        o = jnp.dot(e.astype(BF16), vh, preferred_element_type=F32) / den
        outs.extend(o[g * CHUNK:(g + 1) * CHUNK, :] for g in range(A_GROUP))
    return jnp.concatenate(outs, axis=1).astype(BF16)


def _attend_b(q, k, v, bias_ref, nk):
    boff = B_BAND - nk
    outs = []
    for h in range(B_HEADS):
        sl = slice(h * HEAD_DIM, (h + 1) * HEAD_DIM)
        s = lax.dot_general(q[:, sl], k[:, sl], (((1,), (1,)), ((), ())), preferred_element_type=F32)
        s = s + bias_ref[h, :, boff:boff + nk]
        mx = jnp.max(s, axis=-1, keepdims=True)
        e = jnp.exp(s - mx)
        den = jnp.sum(e, axis=-1, keepdims=True)
        outs.append(jnp.dot(e.astype(BF16), v[:, sl], preferred_element_type=F32) / den)
    return jnp.concatenate(outs, axis=1).astype(BF16)


MASKED = -1e30
PAIR_W = 2 * HEAD_DIM
ATT_BLOCK = 2 * CHUNK
A_COLS = A_WINDOW + ATT_BLOCK
B_COLS = B_REACH + ATT_BLOCK


def _pair_attend(q2, k2, v2, ind, bias, sinks):
    c = k2.shape[0] // 2
    s = lax.dot_general(q2, k2, (((1,), (1,)), ((), ())), preferred_element_type=F32) + bias
    es, mxs = [], []
    for half in range(2):
        sh = s[:, half * c:(half + 1) * c]
        mx = jnp.max(sh, axis=-1, keepdims=True)
        if sinks is not None:
            mx = jnp.maximum(mx, sinks[half])
        es.append(jnp.exp(sh - mx).astype(BF16))
        mxs.append(mx)
    o = jnp.dot(jnp.concatenate(es, axis=1), jnp.concatenate([v2, ind], axis=1), preferred_element_type=F32)
    dens = [o[:, PAIR_W + half:PAIR_W + half + 1] for half in range(2)]
    if sinks is not None:
        dens = [dens[half] + jnp.exp(sinks[half] - mxs[half]) for half in range(2)]
    lane = lax.broadcasted_iota(jnp.int32, (o.shape[0], PAIR_W), 1)
    return o[:, :PAIR_W] / jnp.where(lane < HEAD_DIM, dens[0], dens[1])


def _last_cols(table, full, n):
    if n == full:
        return table
    return jnp.concatenate([table[:, full - n:full], table[:, 2 * full - n:2 * full]], axis=1)


def _attn_prompt_kernel(qa_ref, ka_ref, va_ref, qb_ref, kb_ref, vb_ref, ba_ref, sk_ref, bb_ref, lo_ref, hi_ref,
                        ia_ref, ib_ref, oa_ref, ob_ref, ak_ref, av_ref, bk_ref, bv_ref, *, tile):
    j = pl.program_id(1)

    @pl.when(j == 0)
    def _():
        lo, hi = lo_ref[:, 0:PAIR_W], hi_ref[:, 0:PAIR_W]
        for src, dst in ((ka_ref, ak_ref), (va_ref, av_ref)):
            x = src[0]
            swapped = jnp.concatenate([x[:, HEAD_DIM:], x[:, :HEAD_DIM]], axis=1)
            dst[0] = x * lo
            dst[1] = swapped * hi
            dst[2] = swapped * lo
            dst[3] = x * hi
        for src, dst in ((kb_ref, bk_ref), (vb_ref, bv_ref)):
            x = src[0]
            dst[0] = x * lo_ref[...]
            dst[1] = x * hi_ref[...]

    def block(g, r0, n_a, n_b):
        rows = pl.ds(r0, ATT_BLOCK)
        band_a = pl.ds(g + ATT_BLOCK - n_a, n_a)
        band_b = pl.ds(g + ATT_BLOCK - n_b, n_b)
        ind_a = jnp.concatenate([ia_ref[0:n_a, :], ia_ref[A_COLS:A_COLS + n_a, :]], axis=0)
        ind_b = jnp.concatenate([ib_ref[0:n_b, :], ib_ref[B_COLS:B_COLS + n_b, :]], axis=0)
        for kv in range(A_KV_HEADS):
            c0, c1 = slice(2 * kv * PAIR_W, (2 * kv + 1) * PAIR_W), slice((2 * kv + 1) * PAIR_W, (2 * kv + 2) * PAIR_W)
            q2 = jnp.concatenate([qa_ref[0, rows, c0], qa_ref[0, rows, c1]], axis=0)
            k2 = jnp.concatenate([ak_ref[2 * kv, band_a, :], ak_ref[2 * kv + 1, band_a, :]], axis=0)
            v2 = jnp.concatenate([av_ref[2 * kv, band_a, :], av_ref[2 * kv + 1, band_a, :]], axis=0)
            o = _pair_attend(q2, k2, v2, ind_a, _last_cols(ba_ref[kv], A_COLS, n_a), (sk_ref[kv, 0], sk_ref[kv, 1]))
            oa_ref[0, rows, c0] = o[0:ATT_BLOCK].astype(BF16)
            oa_ref[0, rows, c1] = o[ATT_BLOCK:].astype(BF16)
        for p in range(B_HEADS // 2):
            cols = slice(p * PAIR_W, (p + 1) * PAIR_W)
            k2 = jnp.concatenate([bk_ref[0, band_b, cols], bk_ref[1, band_b, cols]], axis=0)
            v2 = jnp.concatenate([bv_ref[0, band_b, cols], bv_ref[1, band_b, cols]], axis=0)
            o = _pair_attend(qb_ref[0, rows, cols], k2, v2, ind_b, _last_cols(bb_ref[p], B_COLS, n_b), None)
            ob_ref[0, rows, cols] = o.astype(BF16)

    n_blocks = tile // ATT_BLOCK

    @pl.when(j == 0)
    def _():
        for blk in range(n_blocks):
            g = blk * ATT_BLOCK
            block(g, g, min(g + ATT_BLOCK, A_COLS), min(g + ATT_BLOCK, B_COLS))

    @pl.when(j > 0)
    def _():
        def body(blk, carry):
            r0 = pl.multiple_of(blk * ATT_BLOCK, ATT_BLOCK)
            block(pl.multiple_of(j * tile + r0, ATT_BLOCK), r0, A_COLS, B_COLS)
            return carry
        lax.fori_loop(0, n_blocks, body, 0)


def _head_indicator(c):
    ind = np.zeros((2 * c, PAIR_W), np.float32)
    ind[:c, 0] = 1.0
    ind[c:, 1] = 1.0
    return ind


def _attn_prompt(qa, ka, va, qb, kb, vb, bias2_a, sinks2, bias2_b):
    b, s, _ = qa.shape
    tile = SEQ_TILE
    assert s % tile == 0 and tile % ATT_BLOCK == 0
    lane = np.arange(B_W) % PAIR_W
    lo = jnp.asarray((lane < HEAD_DIM).astype(np.float32).reshape(1, B_W), BF16)
    hi = jnp.asarray((lane >= HEAD_DIM).astype(np.float32).reshape(1, B_W), BF16)
    ind_a, ind_b = (jnp.asarray(_head_indicator(c), BF16) for c in (A_COLS, B_COLS))
    tok = lambda w: pl.BlockSpec((1, tile, w), lambda i, j: (i, j, 0))
    seq = lambda w: pl.BlockSpec((1, s, w), lambda i, j: (i, 0, 0))
    const = lambda shp: pl.BlockSpec(shp, lambda i, j: (0,) * len(shp))
    return pl.pallas_call(
        functools.partial(_attn_prompt_kernel, tile=tile),
        out_shape=(jax.ShapeDtypeStruct((b, s, QA_W), BF16), jax.ShapeDtypeStruct((b, s, B_W), BF16)),
        grid=(b, s // tile),
        in_specs=[tok(QA_W), seq(KVA_W), seq(KVA_W), tok(B_W), seq(B_W), seq(B_W),
                  const(bias2_a.shape), const(sinks2.shape), const(bias2_b.shape), const(lo.shape), const(hi.shape),
                  const(ind_a.shape), const(ind_b.shape)],
        out_specs=(tok(QA_W), tok(B_W)),
        scratch_shapes=[pltpu.VMEM((4, s, PAIR_W), BF16), pltpu.VMEM((4, s, PAIR_W), BF16),
                        pltpu.VMEM((2, s, B_W), BF16), pltpu.VMEM((2, s, B_W), BF16)],
        compiler_params=_cparams("arbitrary", "arbitrary"),
        name="attn_prompt",
    )(qa, ka, va, qb, kb, vb, bias2_a, sinks2, bias2_b, lo, hi, ind_a, ind_b)


def _attn_sample_kernel(qa_ref, ka_ref, va_ref, qb_ref, kb_ref, vb_ref, ba_ref, sk_ref, bb_ref, oa_ref, ob_ref):
    oa_ref[0] = _attend_a(qa_ref[0], ka_ref[0], va_ref[0], ba_ref, sk_ref, A_BAND)
    ob_ref[0] = _attend_b(qb_ref[0], kb_ref[0], vb_ref[0], bb_ref, B_BAND)


def _attn_sample(qa, ka, va, qb, kb, vb, bias_a, sinks, bias_b):
    b, t, _ = qa.shape
    per_b = lambda r, w: pl.BlockSpec((1, r, w), lambda i: (i, 0, 0))
    const = lambda shp: pl.BlockSpec(shp, lambda i: (0,) * len(shp))
    return pl.pallas_call(
        _attn_sample_kernel,
        out_shape=(jax.ShapeDtypeStruct((b, t, QA_W), BF16), jax.ShapeDtypeStruct((b, t, B_W), BF16)),
        grid=(b,),
        in_specs=[per_b(t, QA_W), per_b(A_BAND, KVA_W), per_b(A_BAND, KVA_W),
                  per_b(t, B_W), per_b(B_BAND, B_W), per_b(B_BAND, B_W),
                  const(bias_a.shape), const(sinks.shape), const(bias_b.shape)],
        out_specs=(per_b(t, QA_W), per_b(t, B_W)),
        compiler_params=_cparams("parallel"),
        name="attn_sample",
    )(qa, ka, va, qb, kb, vb, bias_a, sinks, bias_b)


def _route(lt):
    t = lt.shape[1]
    el = lt[0:N_EXPERTS]
    gl = lt[N_EXPERTS:N_EXPERTS + N_GROUPS]
    gmax = jnp.max(gl, axis=0, keepdims=True)
    gi = lax.broadcasted_iota(jnp.int32, (N_GROUPS, t), 0)
    gidx = jnp.min(jnp.where(gl == gmax, gi, N_GROUPS), axis=0, keepdims=True)
    g_w = 1.0 / jnp.sum(jnp.exp(gl - gmax), axis=0, keepdims=True)
    e_sel = el[(N_GROUPS - 1) * EXPERTS_PER_GROUP:]
    for g in range(N_GROUPS - 2, -1, -1):
        e_sel = jnp.where(gidx == g, el[g * EXPERTS_PER_GROUP:(g + 1) * EXPERTS_PER_GROUP], e_sel)
    ei = lax.broadcasted_iota(jnp.int32, (EXPERTS_PER_GROUP, t), 0)
    m1 = jnp.max(e_sel, axis=0, keepdims=True)
    i1 = jnp.min(jnp.where(e_sel == m1, ei, EXPERTS_PER_GROUP), axis=0, keepdims=True)
    rest = jnp.where(ei == i1, -jnp.inf, e_sel)
    m2 = jnp.max(rest, axis=0, keepdims=True)
    i2 = jnp.min(jnp.where(rest == m2, ei, EXPERTS_PER_GROUP), axis=0, keepdims=True)
    ex = jnp.exp(m2 - m1)
    den = 1.0 + ex
    w1 = g_w * (1.0 / den)
    w2 = g_w * (ex / den)
    lo = jnp.minimum(i1, i2)
    hi = jnp.maximum(i1, i2)
    first_is_lo = i1 < i2
    w_lo = jnp.where(first_is_lo, w1, w2)
    w_hi = jnp.where(first_is_lo, w2, w1)
    pair = ((lo * (2 * EXPERTS_PER_GROUP - 1 - lo)) >> 1) + (hi - lo - 1)
    return gidx * PAIRS_PER_GROUP + pair, w_lo, w_hi


def _route_logits(wr_ref, br_ref, h2):
    return lax.dot_general(wr_ref[...], h2, (((1,), (1,)), ((), ())), preferred_element_type=F32) + br_ref[...]


def _col_from_row(w_row, n):
    ri = lax.broadcasted_iota(jnp.int32, (n, n), 0)
    ci = lax.broadcasted_iota(jnp.int32, (n, n), 1)
    return jnp.sum(jnp.where(ri == ci, jnp.broadcast_to(w_row, (n, n)), 0.0), axis=1, keepdims=True)


def _rank_in_class(cls, run_ref, tri_ref):
    t = cls.shape[1]
    onehot = lax.broadcasted_iota(jnp.int32, (CLASS_ROWS, t), 0) == cls
    ones = jnp.where(onehot, 1.0, 0.0)
    before = jnp.dot(ones.astype(BF16), tri_ref[...], preferred_element_type=F32)
    run = run_ref[...]
    rank = jnp.sum(jnp.where(onehot, before + run, 0.0), axis=0, keepdims=True)
    run_ref[...] = run + jnp.sum(ones, axis=1, keepdims=True)
    return rank.astype(jnp.int32)


def _post_kernel(x_ref, oa_ref, ob_ref, mod_ref, g1_ref, g1p_ref, g2_ref,
                 wg_ref, bg_ref, wpa_ref, wpb_ref, wo_ref, wr_ref, br_ref, tri_ref,
                 x1_ref, h2_ref, cls_ref, rank_ref, cnt_ref, run_ref):
    @pl.when((pl.program_id(0) == 0) & (pl.program_id(1) == 0))
    def _():
        run_ref[...] = jnp.zeros_like(run_ref)

    x = x_ref[0]
    h = _norm_mod(x, g1_ref[...], mod_ref[0, 1:2, :], mod_ref[0, 0:1, :]).astype(BF16)
    gates = jax.nn.sigmoid(jnp.dot(h, wg_ref[...], preferred_element_type=F32) + bg_ref[...])
    pa = jnp.dot(oa_ref[0], wpa_ref[...], preferred_element_type=F32)
    pb = jnp.dot(ob_ref[0], wpb_ref[...], preferred_element_type=F32)
    mixed = gates[:, :D_MODEL] * pa + gates[:, D_MODEL:] * pb
    y = jnp.dot(mixed.astype(BF16), wo_ref[...], preferred_element_type=F32)
    x1 = x + mod_ref[0, 2:3, :] * _rms(y, g1p_ref[...])
    x1_ref[0] = x1
    h2 = _norm_mod(x1, g2_ref[...], mod_ref[0, 4:5, :], mod_ref[0, 3:4, :]).astype(BF16)
    cls, _, _ = _route(_route_logits(wr_ref, br_ref, h2))
    cls_ref[0, 0] = cls
    rank_ref[0, 0] = _rank_in_class(cls, run_ref, tri_ref)
    cnt_ref[...] = run_ref[...]
    t = x.shape[0]
    h2f = h2.astype(F32)
    for c in range(IN_SLAB):
        h2_ref[pl.ds(c, t, stride=IN_SLAB), :] = h2f[:, c * LANES:(c + 1) * LANES]


def _post(x, oa, ob, mods, g_pre_mix, g_post_mix, g_pre_ffn, wg, bg, wpa, wpb, wo, wr, br, tile):
    b, s, d = x.shape
    nt = s // tile
    tri = (jnp.arange(tile)[:, None] < jnp.arange(tile)[None, :]).astype(BF16)
    tok = lambda w: pl.BlockSpec((1, tile, w), lambda i, j: (i, j, 0))
    const = lambda shp: pl.BlockSpec(shp, lambda i, j: (0,) * len(shp))
    per_tile = pl.BlockSpec((1, 1, 1, tile), lambda i, j: (i, j, 0, 0))
    return pl.pallas_call(
        _post_kernel,
        out_shape=(
            jax.ShapeDtypeStruct((b, s, d), F32),
            jax.ShapeDtypeStruct((b * s * IN_SLAB, LANES), F32),
            jax.ShapeDtypeStruct((b, nt, 1, tile), jnp.int32),
            jax.ShapeDtypeStruct((b, nt, 1, tile), jnp.int32),
            jax.ShapeDtypeStruct((CLASS_ROWS, 1), F32),
        ),
        grid=(b, nt),
        in_specs=[
            tok(d), tok(QA_W), tok(B_W),
            pl.BlockSpec((1, 6, d), lambda i, j: (i, 0, 0)),
            const((1, d)), const((1, d)), const((1, d)),
            const(wg.shape), const(bg.shape), const(wpa.shape), const(wpb.shape), const(wo.shape),
            const(wr.shape), const(br.shape), const(tri.shape),
        ],
        out_specs=(tok(d), pl.BlockSpec((tile * IN_SLAB, LANES), lambda i, j: (i * nt + j, 0)),
                   per_tile, per_tile, const((CLASS_ROWS, 1))),
        scratch_shapes=[pltpu.VMEM((CLASS_ROWS, 1), F32)],
        compiler_params=_cparams("arbitrary", "arbitrary"),
        name="post",
    )(x, oa, ob, mods, g_pre_mix, g_post_mix, g_pre_ffn, wg, bg, wpa, wpb, wo, wr, br, tri)


def _invert_kernel(upos_ref, tok_ref, *, n):
    def body(i, carry):
        for u in range(ISSUE_UNROLL):
            t = i * ISSUE_UNROLL + u
            tok_ref[upos_ref[t]] = t
        return carry
    lax.fori_loop(0, n // ISSUE_UNROLL, body, 0)


def _invert(upos):
    n = upos.shape[0]
    return pl.pallas_call(
        functools.partial(_invert_kernel, n=n),
        out_shape=jax.ShapeDtypeStruct((n,), jnp.int32),
        in_specs=[pl.BlockSpec(memory_space=pltpu.SMEM)],
        out_specs=pl.BlockSpec(memory_space=pltpu.SMEM),
        name="invert",
    )(upos)


def _expert(xb, wgu_ref, wd_ref):
    gu = jnp.dot(xb, wgu_ref[0], preferred_element_type=F32)
    gate = gu[:, :D_EXPERT]
    he = (gate * jax.nn.sigmoid(gate)) * gu[:, D_EXPERT:]
    return jnp.dot(he.astype(BF16), wd_ref[0], preferred_element_type=F32)


def _moe_kernel(tok_ref, base_ref, ea_ref, eb_ref, nu_ref, h_ref, wr_ref, br_ref, wgu_a_ref, wd_a_ref,
                wgu_b_ref, wd_b_ref, ys_ref, xbuf, sems, *, tile, n_tok):
    t = pl.program_id(0)
    nu = nu_ref[0]
    slot = t % 2

    def slab_copy(tok, dst_slot, r):
        return pltpu.make_async_copy(h_ref.at[pl.ds(tok * IN_SLAB, IN_SLAB)],
                                     xbuf.at[dst_slot, pl.ds(r * IN_SLAB, IN_SLAB)], sems.at[dst_slot])

    def fetch(step, dst_slot):
        base = base_ref[step]
        for r in range(tile):
            slab_copy(tok_ref[jnp.minimum(base + r, n_tok - 1)], dst_slot, r).start(priority=r % 2)

    def wait(dst_slot):
        for r in range(tile):
            slab_copy(0, dst_slot, 0).wait()

    @pl.when(t == 0)
    def _():
        fetch(0, 0)

    @pl.when(t < nu)
    def _():
        wait(slot)
        fetch(jnp.minimum(t + 1, nu - 1), 1 - slot)
        xb = jnp.concatenate([xbuf[slot, pl.ds(c, tile, stride=IN_SLAB), :] for c in range(IN_SLAB)],
                             axis=1).astype(BF16)
        _, w_lo, w_hi = _route(_route_logits(wr_ref, br_ref, xb))
        y = (_col_from_row(w_lo, tile) * _expert(xb, wgu_a_ref, wd_a_ref)
             + _col_from_row(w_hi, tile) * _expert(xb, wgu_b_ref, wd_b_ref))
        for c in range(OUT_SLAB):
            ys_ref[pl.ds(c, tile, stride=OUT_SLAB), :] = y[:, c * LANES:(c + 1) * LANES]

    @pl.when(t == nu - 1)
    def _():
        wait(1 - slot)

    @pl.when(t >= nu)
    def _():
        ys_ref[...] = jnp.zeros_like(ys_ref)


def _moe_grouped(h_slabs, sorted_tok, tile_base, tile_ea, tile_eb, n_used, wr, br, wgu, wd, tile, n_tiles):
    d = D_MODEL
    n_tok = sorted_tok.shape[0]
    expert = lambda which, shp: pl.BlockSpec(shp, lambda t, tok, base, ea, eb, nu: ((ea, eb)[which][t], 0, 0))
    const = lambda shp: pl.BlockSpec(shp, lambda t, *_: (0,) * len(shp))
    return pl.pallas_call(
        functools.partial(_moe_kernel, tile=tile, n_tok=n_tok),
        out_shape=jax.ShapeDtypeStruct((n_tiles * tile * OUT_SLAB, LANES), F32),
        grid_spec=pltpu.PrefetchScalarGridSpec(
            num_scalar_prefetch=5,
            grid=(n_tiles,),
            in_specs=[
                pl.BlockSpec(memory_space=pl.ANY), const(wr.shape), const(br.shape),
                expert(0, (1, d, 2 * D_EXPERT)), expert(0, (1, D_EXPERT, d)),
                expert(1, (1, d, 2 * D_EXPERT)), expert(1, (1, D_EXPERT, d)),
            ],
            out_specs=pl.BlockSpec((tile * OUT_SLAB, LANES), lambda t, *_: (t, 0)),
            scratch_shapes=[pltpu.VMEM((2, tile * IN_SLAB, LANES), F32), pltpu.SemaphoreType.DMA((2,))],
        ),
        compiler_params=_cparams("arbitrary"),
        name="moe_grouped",
    )(sorted_tok, tile_base, tile_ea, tile_eb, n_used, h_slabs, wr, br, wgu, wd, wgu, wd)


_PAIR_LO, _PAIR_HI = np.triu_indices(EXPERTS_PER_GROUP, k=1)
_CLASS_LO = np.concatenate([g * EXPERTS_PER_GROUP + _PAIR_LO for g in range(N_GROUPS)]).astype(np.int32)
_CLASS_HI = np.concatenate([g * EXPERTS_PER_GROUP + _PAIR_HI for g in range(N_GROUPS)]).astype(np.int32)


def _moe_plan(counts, cls, rank, n, tile):
    p_max = -(-(n + N_CLASSES * (tile - 1)) // tile) * tile
    nt = p_max // tile
    counts = counts[:N_CLASSES].astype(jnp.int32)
    padded = ((counts + tile - 1) // tile) * tile
    pad_end = jnp.cumsum(padded)
    gstart = pad_end - padded
    ustart = jnp.cumsum(counts) - counts
    onehot = cls[:, None] == jnp.arange(N_CLASSES, dtype=jnp.int32)[None, :]
    pos = rank + jnp.sum(jnp.where(onehot, gstart[None, :], 0), axis=1)
    upos = rank + jnp.sum(jnp.where(onehot, ustart[None, :], 0), axis=1)
    n_used = (pad_end[-1] // tile).astype(jnp.int32).reshape(1)
    tiles = jnp.arange(nt, dtype=jnp.int32)
    tile_cls = jnp.minimum(jnp.sum((pad_end[None, :] <= (tiles * tile)[:, None]).astype(jnp.int32), axis=1), N_CLASSES - 1)
    tile_ea = jnp.asarray(_CLASS_LO)[tile_cls]
    tile_eb = jnp.asarray(_CLASS_HI)[tile_cls]
    tile_base = jnp.clip(ustart[tile_cls] + tiles * tile - gstart[tile_cls], 0, n - 1)
    return nt, pos, upos, tile_base, tile_ea, tile_eb, n_used


def _final_kernel(pos_ref, x1_ref, ys_ref, mod_ref, g_ref, o_ref, ybuf, sems, *, tile, n_steps):
    s = pl.program_id(0)
    slot = s % 2

    def slab_copy(src_row, dst_slot, dst_row):
        return pltpu.make_async_copy(ys_ref.at[pl.ds(src_row * OUT_SLAB, OUT_SLAB)],
                                     ybuf.at[dst_slot, pl.ds(dst_row * OUT_SLAB, OUT_SLAB)], sems.at[dst_slot])

    def fetch(step, dst_slot):
        for r in range(tile):
            slab_copy(pos_ref[step * tile + r], dst_slot, r).start(priority=r % 2)

    def wait(dst_slot):
        for r in range(tile):
            slab_copy(0, dst_slot, 0).wait()

    @pl.when(s == 0)
    def _():
        fetch(0, 0)

    wait(slot)
    fetch(jnp.minimum(s + 1, n_steps - 1), 1 - slot)
    y = jnp.concatenate([ybuf[slot, pl.ds(c, tile, stride=OUT_SLAB), :] for c in range(OUT_SLAB)], axis=1)
    o_ref[...] = x1_ref[...] + mod_ref[0, 5:6, :] * _rms(y, g_ref[...])

    @pl.when(s == n_steps - 1)
    def _():
        wait(1 - slot)


def _final(x1, ys, pos, mods, g_post_ffn, tile):
    b, s, d = x1.shape
    n = b * s
    per_b = s // tile
    n_steps = n // tile
    tok = pl.BlockSpec((tile, d), lambda i, *_: (i, 0))
    out = pl.pallas_call(
        functools.partial(_final_kernel, tile=tile, n_steps=n_steps),
        out_shape=jax.ShapeDtypeStruct((n, d), F32),
        grid_spec=pltpu.PrefetchScalarGridSpec(
            num_scalar_prefetch=1,
            grid=(n_steps,),
            in_specs=[
                tok,
                pl.BlockSpec(memory_space=pl.ANY),
                pl.BlockSpec((1, 6, d), lambda i, *_: (i // per_b, 0, 0)),
                pl.BlockSpec((1, d), lambda i, *_: (0, 0)),
            ],
            out_specs=tok,
            scratch_shapes=[pltpu.VMEM((2, tile * OUT_SLAB, LANES), F32), pltpu.SemaphoreType.DMA((2,))],
        ),
        compiler_params=_cparams("arbitrary"),
        name="final",
    )(pos, x1.reshape(n, d), ys, mods, g_post_ffn)
    return out.reshape(b, s, d)


def _t5_bucket(rel):
    half = T5_BUCKETS // 2
    exact = half // 2
    ret = jnp.where(rel > 0, half, 0)
    n = jnp.abs(rel)
    nf = jnp.maximum(n, 1).astype(F32)
    large = exact + (jnp.log(nf / exact) / math.log(T5_MAX_DIST / exact) * (half - exact)).astype(jnp.int32)
    large = jnp.minimum(large, half - 1)
    return ret + jnp.where(n < exact, n, large)


def _toeplitz(u, n_rows, n_cols):
    return jnp.stack([u[..., n_rows - 1 - i:n_rows - 1 - i + n_cols] for i in range(n_rows)], axis=-2)


def _bias_tables(t5_table, rel_table):
    ja = jnp.arange(CHUNK - 1 + A_BAND)
    ua = t5_table[_t5_bucket(ja - (CHUNK - 1) - A_WINDOW)].T.astype(F32)
    jb = jnp.arange(CHUNK - 1 + B_BAND)
    ub = rel_table[:, jnp.clip((CHUNK - 1) - jb + B_REACH, -REL_CLIP, REL_CLIP) + REL_CLIP].astype(F32)
    return _toeplitz(ua, CHUNK, A_BAND), _toeplitz(ub, CHUNK, B_BAND)


def _two_chunk(bias):
    masked = jnp.full(bias.shape[:-1] + (CHUNK,), MASKED, F32)
    return jnp.concatenate([jnp.concatenate([bias, masked], axis=-1), jnp.concatenate([masked, bias], axis=-1)], axis=-2)


def _pair_tables(bias_a, bias_b, sinks):
    a2, b2 = _two_chunk(bias_a), _two_chunk(bias_b)
    stack2 = lambda x, h0, h1: jnp.concatenate([x[h0], x[h1]], axis=0)
    bias2_a = jnp.stack([jnp.concatenate([stack2(a2, 4 * kv, 4 * kv + 2), stack2(a2, 4 * kv + 1, 4 * kv + 3)], axis=1)
                         for kv in range(A_KV_HEADS)])
    bias2_b = jnp.concatenate([b2[0::2], b2[1::2]], axis=-1)
    sk = jnp.broadcast_to(sinks.astype(F32).reshape(A_Q_HEADS, 1, 1), (A_Q_HEADS, ATT_BLOCK, 1))
    sinks2 = jnp.stack([jnp.stack([stack2(sk, 4 * kv, 4 * kv + 2), stack2(sk, 4 * kv + 1, 4 * kv + 3)])
                        for kv in range(A_KV_HEADS)])
    return bias2_a, sinks2, bias2_b


def _moe_and_final(x1, h2e, cls, rank, counts, mods, g_post_ffn, wr, br, wgu, wd, seq_tile, moe_tile):
    b, s, d = x1.shape
    n = b * s
    n_tiles, pos, upos, tile_base, tile_ea, tile_eb, n_used = _moe_plan(
        counts.reshape(CLASS_ROWS), cls.reshape(n), rank.reshape(n), n, moe_tile)
    ys = _moe_grouped(h2e, _invert(upos), tile_base, tile_ea, tile_eb, n_used, wr, br, wgu, wd, moe_tile, n_tiles)
    return _final(x1, ys, pos, mods, g_post_ffn, seq_tile)


def kernel(x_prompt, x_sample, c_prompt, c_sample, cache_a_k, cache_a_v, cache_b_k, cache_b_v, w_ada, b_ada, g_pre_mix, g_post_mix, g_pre_ffn, g_post_ffn, w_in, a_sinks, t5_table, b_rel_table, w_proj_a, w_proj_b, w_gate, b_gate, w_o, w_route_g, b_route_g, w_route_e, b_route_e, w_e_gate, w_e_up, w_e_down):
    depth = w_in.shape[0]
    assert depth == 1
    l = 0
    bp, sp, d = x_prompt.shape
    bs, ts, _ = x_sample.shape

    mods = _ada(jnp.concatenate([c_prompt, c_sample], axis=0), w_ada[l], b_ada[l]).reshape(bp + bs, 6, d)
    mods_p, mods_s = mods[:bp], mods[bp:]

    w_in_bf = w_in[l].astype(BF16)
    wg, wpa, wpb, wo = w_gate[l].astype(BF16), w_proj_a[l].astype(BF16), w_proj_b[l].astype(BF16), w_o[l].astype(BF16)
    bg = b_gate[l].reshape(1, 2 * d)
    pad_rows = ROUTE_ROWS - N_EXPERTS - N_GROUPS
    wr = jnp.concatenate([w_route_e[l].T, w_route_g[l].T, jnp.zeros((pad_rows, d), F32)], axis=0).astype(BF16)
    br = jnp.concatenate([b_route_e[l], b_route_g[l], jnp.zeros((pad_rows,), F32)]).reshape(ROUTE_ROWS, 1)
    wgu = jnp.concatenate([w_e_gate[l], w_e_up[l]], axis=-1).astype(BF16)
    wd = w_e_down[l].astype(BF16)
    g1, g1p, g2, g2p = (g[l].reshape(1, d) for g in (g_pre_mix, g_post_mix, g_pre_ffn, g_post_ffn))

    bias_a, bias_b = _bias_tables(t5_table, b_rel_table[l])
    bias2_a, sinks2, bias2_b = _pair_tables(bias_a, bias_b, a_sinks[l])
    bias_a = bias_a.reshape(A_KV_HEADS, A_GROUP * CHUNK, A_BAND)
    sinks = jnp.broadcast_to(a_sinks[l].reshape(A_KV_HEADS, A_GROUP, 1, 1), (A_KV_HEADS, A_GROUP, CHUNK, 1))
    sinks = sinks.reshape(A_KV_HEADS, A_GROUP * CHUNK, 1).astype(F32)

    qa, ka, va, qb, kb, vb, sak_p, sav_p, sbk_p, sbv_p = _pre_prompt(x_prompt, mods_p, g1, w_in_bf)
    oa, ob = _attn_prompt(qa, ka, va, qb, kb, vb, bias2_a, sinks2, bias2_b)
    x1, h2e, cls, rank, counts = _post(x_prompt, oa, ob, mods_p, g1, g1p, g2, wg, bg, wpa, wpb, wo, wr, br, SEQ_TILE)
    y_prompt = _moe_and_final(x1, h2e, cls, rank, counts, mods_p, g2p, wr, br, wgu, wd, SEQ_TILE, 256)

    la, lb = cache_a_k.shape[2], cache_b_k.shape[2]
    cak = cache_a_k[l].reshape(bs, la, KVA_W)
    cav = cache_a_v[l].reshape(bs, la, KVA_W)
    cbk = cache_b_k[l].reshape(bs, lb, B_W)
    cbv = cache_b_v[l].reshape(bs, lb, B_W)
    qa, ka, va, qb, kb, vb, sak_s, sav_s, sbk_s, sbv_s = _pre_sample(x_sample, mods_s, g1, w_in_bf, cak, cav, cbk, cbv)
    oa, ob = _attn_sample(qa, ka, va, qb, kb, vb, bias_a, sinks, bias_b)
    x1, h2e, cls, rank, counts = _post(x_sample, oa, ob, mods_s, g1, g1p, g2, wg, bg, wpa, wpb, wo, wr, br, ts)
    y_sample = _moe_and_final(x1, h2e, cls, rank, counts, mods_s, g2p, wr, br, wgu, wd, ts, 64)

    a_state = lambda v, b, r: v.reshape(1, b, r, A_KV_HEADS, HEAD_DIM)
    b_state = lambda v, b, r: v.reshape(1, b, r, B_HEADS, HEAD_DIM)
    return (y_prompt, y_sample,
            a_state(sak_p, bp, A_WINDOW), a_state(sav_p, bp, A_WINDOW),
            b_state(sbk_p, bp, B_REACH), b_state(sbv_p, bp, B_REACH),
            a_state(sak_s, bs, la), a_state(sav_s, bs, la),
            b_state(sbk_s, bs, lb), b_state(sbv_s, bs, lb))
```

```python
import functools
import math

import numpy as np
import jax
import jax.numpy as jnp
from jax import lax
from jax.experimental import pallas as pl
from jax.experimental.pallas import tpu as pltpu

D_MODEL = 1024
CHUNK = 64
HEAD_DIM = 64
A_Q_HEADS = 8
A_KV_HEADS = 2
A_GROUP = A_Q_HEADS // A_KV_HEADS
A_WINDOW = 128
A_BACK = A_WINDOW // CHUNK
B_HEADS = 8
B_BACK = 8
B_REACH = B_BACK * CHUNK
REL_CLIP = 256
T5_BUCKETS = 32
T5_MAX_DIST = 128
N_GROUPS = 4
EXPERTS_PER_GROUP = 8
N_EXPERTS = N_GROUPS * EXPERTS_PER_GROUP
D_EXPERT = D_MODEL // 4
EPS = 1e-6

QA_W = A_Q_HEADS * HEAD_DIM
KVA_W = A_KV_HEADS * HEAD_DIM
B_W = B_HEADS * HEAD_DIM
IN_W = QA_W + 2 * KVA_W + 3 * B_W
A_BAND = A_WINDOW + CHUNK
B_BAND = B_REACH + CHUNK

PAIRS_PER_GROUP = EXPERTS_PER_GROUP * (EXPERTS_PER_GROUP - 1) // 2
N_CLASSES = N_GROUPS * PAIRS_PER_GROUP
ROUTE_ROWS = 40
CLASS_ROWS = 128
LANES = 128
IN_SLAB = D_MODEL // LANES + 1
OUT_SLAB = D_MODEL // LANES

F32 = jnp.float32
BF16 = jnp.bfloat16

VMEM_LIMIT_BYTES = 56 * 1024 * 1024

SEQ_TILE = 512
ISSUE_UNROLL = 8


def _cparams(*sem):
    return pltpu.CompilerParams(dimension_semantics=sem, vmem_limit_bytes=VMEM_LIMIT_BYTES)


def _norm_mod(x, g, scale, shift):
    y = x * lax.rsqrt(jnp.mean(x * x, axis=-1, keepdims=True) + EPS)
    return (y * g) * (1.0 + scale) + shift


def _rms(x, g):
    return (x * lax.rsqrt(jnp.mean(x * x, axis=-1, keepdims=True) + EPS)) * g


def _ada_kernel(c_ref, w_ref, b_ref, o_ref):
    c = c_ref[...]
    s = (c * jax.nn.sigmoid(c)).astype(BF16)
    o_ref[...] = jnp.dot(s, w_ref[...].astype(BF16), preferred_element_type=F32) + b_ref[...]


def _ada(c, w_ada, b_ada):
    n, d = c.shape
    wn = w_ada.shape[1]
    tn = 512
    return pl.pallas_call(
        _ada_kernel,
        out_shape=jax.ShapeDtypeStruct((n, wn), F32),
        grid=(wn // tn,),
        in_specs=[
            pl.BlockSpec((n, d), lambda j: (0, 0)),
            pl.BlockSpec((d, tn), lambda j: (0, j)),
            pl.BlockSpec((1, tn), lambda j: (0, j)),
        ],
        out_specs=pl.BlockSpec((n, tn), lambda j: (0, j)),
        compiler_params=_cparams("arbitrary"),
        name="ada",
    )(c, w_ada, b_ada.reshape(1, wn))


_COL_QA = (0, QA_W)
_COL_KA = (QA_W, QA_W + KVA_W)
_COL_VA = (QA_W + KVA_W, QA_W + 2 * KVA_W)
_COL_QB = (QA_W + 2 * KVA_W, QA_W + 2 * KVA_W + B_W)
_COL_KB = (_COL_QB[1], _COL_QB[1] + B_W)
_COL_VB = (_COL_KB[1], _COL_KB[1] + B_W)
Q_SCALE = HEAD_DIM ** -0.5


def _project(x_ref, mod_ref, g_ref, w_ref):
    h = _norm_mod(x_ref[0], g_ref[...], mod_ref[0, 1:2, :], mod_ref[0, 0:1, :])
    return jnp.dot(h.astype(BF16), w_ref[...], preferred_element_type=F32)


def _cols(p, c):
    return p[:, c[0]:c[1]]


def _pre_prompt_kernel(x_ref, mod_ref, g_ref, w_ref,
                       qa_ref, ka_ref, va_ref, qb_ref, kb_ref, vb_ref,
                       sak_ref, sav_ref, sbk_ref, sbv_ref, *, n_tiles, tile):
    p = _project(x_ref, mod_ref, g_ref, w_ref)
    qa_ref[0] = (_cols(p, _COL_QA) * Q_SCALE).astype(BF16)
    ka_ref[0] = _cols(p, _COL_KA).astype(BF16)
    va_ref[0] = _cols(p, _COL_VA).astype(BF16)
    qb_ref[0] = (_cols(p, _COL_QB) * Q_SCALE).astype(BF16)
    kb_ref[0] = _cols(p, _COL_KB).astype(BF16)
    vb_ref[0] = _cols(p, _COL_VB).astype(BF16)

    @pl.when(pl.program_id(1) == n_tiles - 1)
    def _():
        sak_ref[0] = _cols(p, _COL_KA)[tile - A_WINDOW:, :]
        sav_ref[0] = _cols(p, _COL_VA)[tile - A_WINDOW:, :]
        sbk_ref[0] = _cols(p, _COL_KB)[tile - B_REACH:, :]
        sbv_ref[0] = _cols(p, _COL_VB)[tile - B_REACH:, :]


def _pre_prompt(x, mods, g_pre, w_in_bf):
    b, s, d = x.shape
    tile = SEQ_TILE
    assert s % tile == 0 and tile >= B_REACH and s >= B_REACH
    nt = s // tile
    tok = lambda w: pl.BlockSpec((1, tile, w), lambda i, j: (i, j, 0))
    state = lambda r, w: pl.BlockSpec((1, r, w), lambda i, j: (i, 0, 0))
    return pl.pallas_call(
        functools.partial(_pre_prompt_kernel, n_tiles=nt, tile=tile),
        out_shape=(
            jax.ShapeDtypeStruct((b, s, QA_W), BF16),
            jax.ShapeDtypeStruct((b, s, KVA_W), BF16),
            jax.ShapeDtypeStruct((b, s, KVA_W), BF16),
            jax.ShapeDtypeStruct((b, s, B_W), BF16),
            jax.ShapeDtypeStruct((b, s, B_W), BF16),
            jax.ShapeDtypeStruct((b, s, B_W), BF16),
            jax.ShapeDtypeStruct((b, A_WINDOW, KVA_W), F32),
            jax.ShapeDtypeStruct((b, A_WINDOW, KVA_W), F32),
            jax.ShapeDtypeStruct((b, B_REACH, B_W), F32),
            jax.ShapeDtypeStruct((b, B_REACH, B_W), F32),
        ),
        grid=(b, nt),
        in_specs=[
            tok(d),
            pl.BlockSpec((1, 6, d), lambda i, j: (i, 0, 0)),
            pl.BlockSpec((1, d), lambda i, j: (0, 0)),
            pl.BlockSpec((d, IN_W), lambda i, j: (0, 0)),
        ],
        out_specs=(
            tok(QA_W), tok(KVA_W), tok(KVA_W), tok(B_W), tok(B_W), tok(B_W),
            state(A_WINDOW, KVA_W), state(A_WINDOW, KVA_W), state(B_REACH, B_W), state(B_REACH, B_W),
        ),
        compiler_params=_cparams("parallel", "arbitrary"),
        name="pre_prompt",
    )(x, mods, g_pre, w_in_bf)


def _pre_sample_kernel(x_ref, mod_ref, g_ref, w_ref, cak_ref, cav_ref, cbk_ref, cbv_ref,
                       qa_ref, ka_ref, va_ref, qb_ref, kb_ref, vb_ref,
                       sak_ref, sav_ref, sbk_ref, sbv_ref, *, t, la, lb):
    p = _project(x_ref, mod_ref, g_ref, w_ref)
    qa_ref[0] = (_cols(p, _COL_QA) * Q_SCALE).astype(BF16)
    qb_ref[0] = (_cols(p, _COL_QB) * Q_SCALE).astype(BF16)
    for cache_ref, band_ref, state_ref, col, hist in (
            (cak_ref, ka_ref, sak_ref, _COL_KA, la), (cav_ref, va_ref, sav_ref, _COL_VA, la),
            (cbk_ref, kb_ref, sbk_ref, _COL_KB, lb), (cbv_ref, vb_ref, sbv_ref, _COL_VB, lb)):
        new = _cols(p, col)
        cache = cache_ref[0]
        band_ref[0, 0:hist, :] = cache.astype(BF16)
        band_ref[0, hist:hist + t, :] = new.astype(BF16)
        state_ref[0, 0:hist - t, :] = cache[t:, :]
        state_ref[0, hist - t:hist, :] = new


def _pre_sample(x, mods, g_pre, w_in_bf, cak, cav, cbk, cbv):
    b, t, d = x.shape
    la, lb = cak.shape[1], cbk.shape[1]
    assert t == CHUNK and la == A_WINDOW and lb == B_REACH
    per_b = lambda r, w: pl.BlockSpec((1, r, w), lambda i: (i, 0, 0))
    return pl.pallas_call(
        functools.partial(_pre_sample_kernel, t=t, la=la, lb=lb),
        out_shape=(
            jax.ShapeDtypeStruct((b, t, QA_W), BF16),
            jax.ShapeDtypeStruct((b, la + t, KVA_W), BF16),
            jax.ShapeDtypeStruct((b, la + t, KVA_W), BF16),
            jax.ShapeDtypeStruct((b, t, B_W), BF16),
            jax.ShapeDtypeStruct((b, lb + t, B_W), BF16),
            jax.ShapeDtypeStruct((b, lb + t, B_W), BF16),
            jax.ShapeDtypeStruct((b, la, KVA_W), F32),
            jax.ShapeDtypeStruct((b, la, KVA_W), F32),
            jax.ShapeDtypeStruct((b, lb, B_W), F32),
            jax.ShapeDtypeStruct((b, lb, B_W), F32),
        ),
        grid=(b,),
        in_specs=[
            per_b(t, d),
            per_b(6, d),
            pl.BlockSpec((1, d), lambda i: (0, 0)),
            pl.BlockSpec((d, IN_W), lambda i: (0, 0)),
            per_b(la, KVA_W), per_b(la, KVA_W), per_b(lb, B_W), per_b(lb, B_W),
        ],
        out_specs=(
            per_b(t, QA_W), per_b(la + t, KVA_W), per_b(la + t, KVA_W),
            per_b(t, B_W), per_b(lb + t, B_W), per_b(lb + t, B_W),
            per_b(la, KVA_W), per_b(la, KVA_W), per_b(lb, B_W), per_b(lb, B_W),
        ),
        compiler_params=_cparams("parallel"),
        name="pre_sample",
    )(x, mods, g_pre, w_in_bf, cak, cav, cbk, cbv)


def _attend_a(q, k, v, bias_ref, sink_ref, nk):
    boff = A_BAND - nk
    outs = []
    for kv in range(A_KV_HEADS):
        qs = jnp.concatenate(
            [q[:, (kv * A_GROUP + g) * HEAD_DIM:(kv * A_GROUP + g + 1) * HEAD_DIM] for g in range(A_GROUP)], axis=0)
        kh = k[:, kv * HEAD_DIM:(kv + 1) * HEAD_DIM]
        vh = v[:, kv * HEAD_DIM:(kv + 1) * HEAD_DIM]
        s = lax.dot_general(qs, kh, (((1,), (1,)), ((), ())), preferred_element_type=F32)
        s = s + bias_ref[kv, :, boff:boff + nk]
        sk = sink_ref[kv]
        mx = jnp.maximum(jnp.max(s, axis=-1, keepdims=True), sk)
        e = jnp.exp(s - mx)
        den = jnp.sum(e, axis=-1, keepdims=True) + jnp.exp(sk - mx)
        o = jnp.dot(e.astype(BF16), vh, preferred_element_type=F32) / den
        outs.extend(o[g * CHUNK:(g + 1) * CHUNK, :] for g in range(A_GROUP))
    return jnp.concatenate(outs, axis=1).astype(BF16)


def _attend_b(q, k, v, bias_ref, nk):
    boff = B_BAND - nk
    outs = []
    for h in range(B_HEADS):
        sl = slice(h * HEAD_DIM, (h + 1) * HEAD_DIM)
        s = lax.dot_general(q[:, sl], k[:, sl], (((1,), (1,)), ((), ())), preferred_element_type=F32)
        s = s + bias_ref[h, :, boff:boff + nk]
        mx = jnp.max(s, axis=-1, keepdims=True)
        e = jnp.exp(s - mx)
        den = jnp.sum(e, axis=-1, keepdims=True)
        outs.append(jnp.dot(e.astype(BF16), v[:, sl], preferred_element_type=F32) / den)
    return jnp.concatenate(outs, axis=1).astype(BF16)


MASKED = -1e30
PAIR_W = 2 * HEAD_DIM
ATT_BLOCK = 2 * CHUNK
A_COLS = A_WINDOW + ATT_BLOCK
B_COLS = B_REACH + ATT_BLOCK


def _pair_attend(q2, k2, v2, ind, bias, sinks):
    c = k2.shape[0] // 2
    s = lax.dot_general(q2, k2, (((1,), (1,)), ((), ())), preferred_element_type=F32) + bias
    es, mxs = [], []
    for half in range(2):
        sh = s[:, half * c:(half + 1) * c]
        mx = jnp.max(sh, axis=-1, keepdims=True)
        if sinks is not None:
            mx = jnp.maximum(mx, sinks[half])
        es.append(jnp.exp(sh - mx).astype(BF16))
        mxs.append(mx)
    o = jnp.dot(jnp.concatenate(es, axis=1), jnp.concatenate([v2, ind], axis=1), preferred_element_type=F32)
    dens = [o[:, PAIR_W + half:PAIR_W + half + 1] for half in range(2)]
    if sinks is not None:
        dens = [dens[half] + jnp.exp(sinks[half] - mxs[half]) for half in range(2)]
    lane = lax.broadcasted_iota(jnp.int32, (o.shape[0], PAIR_W), 1)
    return o[:, :PAIR_W] / jnp.where(lane < HEAD_DIM, dens[0], dens[1])


def _last_cols(table, full, n):
    if n == full:
        return table
    return jnp.concatenate([table[:, full - n:full], table[:, 2 * full - n:2 * full]], axis=1)


def _attn_prompt_kernel(qa_ref, ka_ref, va_ref, qb_ref, kb_ref, vb_ref, ba_ref, sk_ref, bb_ref, lo_ref, hi_ref,
                        ia_ref, ib_ref, oa_ref, ob_ref, ak_ref, av_ref, bk_ref, bv_ref, *, tile):
    j = pl.program_id(1)

    @pl.when(j == 0)
    def _():
        lo, hi = lo_ref[:, 0:PAIR_W], hi_ref[:, 0:PAIR_W]
        for src, dst in ((ka_ref, ak_ref), (va_ref, av_ref)):
            x = src[0]
            swapped = jnp.concatenate([x[:, HEAD_DIM:], x[:, :HEAD_DIM]], axis=1)
            dst[0] = x * lo
            dst[1] = swapped * hi
            dst[2] = swapped * lo
            dst[3] = x * hi
        for src, dst in ((kb_ref, bk_ref), (vb_ref, bv_ref)):
            x = src[0]
            dst[0] = x * lo_ref[...]
            dst[1] = x * hi_ref[...]

    def block(g, r0, n_a, n_b):
        rows = pl.ds(r0, ATT_BLOCK)
        band_a = pl.ds(g + ATT_BLOCK - n_a, n_a)
        band_b = pl.ds(g + ATT_BLOCK - n_b, n_b)
        ind_a = jnp.concatenate([ia_ref[0:n_a, :], ia_ref[A_COLS:A_COLS + n_a, :]], axis=0)
        ind_b = jnp.concatenate([ib_ref[0:n_b, :], ib_ref[B_COLS:B_COLS + n_b, :]], axis=0)
        for kv in range(A_KV_HEADS):
            c0, c1 = slice(2 * kv * PAIR_W, (2 * kv + 1) * PAIR_W), slice((2 * kv + 1) * PAIR_W, (2 * kv + 2) * PAIR_W)
            q2 = jnp.concatenate([qa_ref[0, rows, c0], qa_ref[0, rows, c1]], axis=0)
            k2 = jnp.concatenate([ak_ref[2 * kv, band_a, :], ak_ref[2 * kv + 1, band_a, :]], axis=0)
            v2 = jnp.concatenate([av_ref[2 * kv, band_a, :], av_ref[2 * kv + 1, band_a, :]], axis=0)
            o = _pair_attend(q2, k2, v2, ind_a, _last_cols(ba_ref[kv], A_COLS, n_a), (sk_ref[kv, 0], sk_ref[kv, 1]))
            oa_ref[0, rows, c0] = o[0:ATT_BLOCK].astype(BF16)
            oa_ref[0, rows, c1] = o[ATT_BLOCK:].astype(BF16)
        for p in range(B_HEADS // 2):
            cols = slice(p * PAIR_W, (p + 1) * PAIR_W)
            k2 = jnp.concatenate([bk_ref[0, band_b, cols], bk_ref[1, band_b, cols]], axis=0)
            v2 = jnp.concatenate([bv_ref[0, band_b, cols], bv_ref[1, band_b, cols]], axis=0)
            o = _pair_attend(qb_ref[0, rows, cols], k2, v2, ind_b, _last_cols(bb_ref[p], B_COLS, n_b), None)
            ob_ref[0, rows, cols] = o.astype(BF16)

    n_blocks = tile // ATT_BLOCK

    @pl.when(j == 0)
    def _():
        for blk in range(n_blocks):
            g = blk * ATT_BLOCK
            block(g, g, min(g + ATT_BLOCK, A_COLS), min(g + ATT_BLOCK, B_COLS))

    @pl.when(j > 0)
    def _():
        def body(blk, carry):
            r0 = pl.multiple_of(blk * ATT_BLOCK, ATT_BLOCK)
            block(pl.multiple_of(j * tile + r0, ATT_BLOCK), r0, A_COLS, B_COLS)
            return carry
        lax.fori_loop(0, n_blocks, body, 0)


def _head_indicator(c):
    ind = np.zeros((2 * c, PAIR_W), np.float32)
    ind[:c, 0] = 1.0
    ind[c:, 1] = 1.0
    return ind


def _attn_prompt(qa, ka, va, qb, kb, vb, bias2_a, sinks2, bias2_b):
    b, s, _ = qa.shape
    tile = SEQ_TILE
    assert s % tile == 0 and tile % ATT_BLOCK == 0
    lane = np.arange(B_W) % PAIR_W
    lo = jnp.asarray((lane < HEAD_DIM).astype(np.float32).reshape(1, B_W), BF16)
    hi = jnp.asarray((lane >= HEAD_DIM).astype(np.float32).reshape(1, B_W), BF16)
    ind_a, ind_b = (jnp.asarray(_head_indicator(c), BF16) for c in (A_COLS, B_COLS))
    tok = lambda w: pl.BlockSpec((1, tile, w), lambda i, j: (i, j, 0))
    seq = lambda w: pl.BlockSpec((1, s, w), lambda i, j: (i, 0, 0))
    const = lambda shp: pl.BlockSpec(shp, lambda i, j: (0,) * len(shp))
    return pl.pallas_call(
        functools.partial(_attn_prompt_kernel, tile=tile),
        out_shape=(jax.ShapeDtypeStruct((b, s, QA_W), BF16), jax.ShapeDtypeStruct((b, s, B_W), BF16)),
        grid=(b, s // tile),
        in_specs=[tok(QA_W), seq(KVA_W), seq(KVA_W), tok(B_W), seq(B_W), seq(B_W),
                  const(bias2_a.shape), const(sinks2.shape), const(bias2_b.shape), const(lo.shape), const(hi.shape),
                  const(ind_a.shape), const(ind_b.shape)],
        out_specs=(tok(QA_W), tok(B_W)),
        scratch_shapes=[pltpu.VMEM((4, s, PAIR_W), BF16), pltpu.VMEM((4, s, PAIR_W), BF16),
                        pltpu.VMEM((2, s, B_W), BF16), pltpu.VMEM((2, s, B_W), BF16)],
        compiler_params=_cparams("arbitrary", "arbitrary"),
        name="attn_prompt",
    )(qa, ka, va, qb, kb, vb, bias2_a, sinks2, bias2_b, lo, hi, ind_a, ind_b)


def _attn_sample_kernel(qa_ref, ka_ref, va_ref, qb_ref, kb_ref, vb_ref, ba_ref, sk_ref, bb_ref, oa_ref, ob_ref):
    oa_ref[0] = _attend_a(qa_ref[0], ka_ref[0], va_ref[0], ba_ref, sk_ref, A_BAND)
    ob_ref[0] = _attend_b(qb_ref[0], kb_ref[0], vb_ref[0], bb_ref, B_BAND)


def _attn_sample(qa, ka, va, qb, kb, vb, bias_a, sinks, bias_b):
    b, t, _ = qa.shape
    per_b = lambda r, w: pl.BlockSpec((1, r, w), lambda i: (i, 0, 0))
    const = lambda shp: pl.BlockSpec(shp, lambda i: (0,) * len(shp))
    return pl.pallas_call(
        _attn_sample_kernel,
        out_shape=(jax.ShapeDtypeStruct((b, t, QA_W), BF16), jax.ShapeDtypeStruct((b, t, B_W), BF16)),
        grid=(b,),
        in_specs=[per_b(t, QA_W), per_b(A_BAND, KVA_W), per_b(A_BAND, KVA_W),
                  per_b(t, B_W), per_b(B_BAND, B_W), per_b(B_BAND, B_W),
                  const(bias_a.shape), const(sinks.shape), const(bias_b.shape)],
        out_specs=(per_b(t, QA_W), per_b(t, B_W)),
        compiler_params=_cparams("parallel"),
        name="attn_sample",
    )(qa, ka, va, qb, kb, vb, bias_a, sinks, bias_b)


def _route(lt):
    t = lt.shape[1]
    el = lt[0:N_EXPERTS]
    gl = lt[N_EXPERTS:N_EXPERTS + N_GROUPS]
    gmax = jnp.max(gl, axis=0, keepdims=True)
    gi = lax.broadcasted_iota(jnp.int32, (N_GROUPS, t), 0)
    gidx = jnp.min(jnp.where(gl == gmax, gi, N_GROUPS), axis=0, keepdims=True)
    g_w = 1.0 / jnp.sum(jnp.exp(gl - gmax), axis=0, keepdims=True)
    e_sel = el[(N_GROUPS - 1) * EXPERTS_PER_GROUP:]
    for g in range(N_GROUPS - 2, -1, -1):
        e_sel = jnp.where(gidx == g, el[g * EXPERTS_PER_GROUP:(g + 1) * EXPERTS_PER_GROUP], e_sel)
    ei = lax.broadcasted_iota(jnp.int32, (EXPERTS_PER_GROUP, t), 0)
    m1 = jnp.max(e_sel, axis=0, keepdims=True)
    i1 = jnp.min(jnp.where(e_sel == m1, ei, EXPERTS_PER_GROUP), axis=0, keepdims=True)
    rest = jnp.where(ei == i1, -jnp.inf, e_sel)
    m2 = jnp.max(rest, axis=0, keepdims=True)
    i2 = jnp.min(jnp.where(rest == m2, ei, EXPERTS_PER_GROUP), axis=0, keepdims=True)
    ex = jnp.exp(m2 - m1)
    den = 1.0 + ex
    w1 = g_w * (1.0 / den)
    w2 = g_w * (ex / den)
    lo = jnp.minimum(i1, i2)
    hi = jnp.maximum(i1, i2)
    first_is_lo = i1 < i2
    w_lo = jnp.where(first_is_lo, w1, w2)
    w_hi = jnp.where(first_is_lo, w2, w1)
    pair = ((lo * (2 * EXPERTS_PER_GROUP - 1 - lo)) >> 1) + (hi - lo - 1)
    return gidx * PAIRS_PER_GROUP + pair, w_lo, w_hi


def _route_logits(wr_ref, br_ref, h2):
    return lax.dot_general(wr_ref[...], h2, (((1,), (1,)), ((), ())), preferred_element_type=F32) + br_ref[...]


def _col_from_row(w_row, n):
    ri = lax.broadcasted_iota(jnp.int32, (n, n), 0)
    ci = lax.broadcasted_iota(jnp.int32, (n, n), 1)
    return jnp.sum(jnp.where(ri == ci, jnp.broadcast_to(w_row, (n, n)), 0.0), axis=1, keepdims=True)


def _rank_in_class(cls, run_ref, tri_ref):
    t = cls.shape[1]
    onehot = lax.broadcasted_iota(jnp.int32, (CLASS_ROWS, t), 0) == cls
    ones = jnp.where(onehot, 1.0, 0.0)
    before = jnp.dot(ones.astype(BF16), tri_ref[...], preferred_element_type=F32)
    run = run_ref[...]
    rank = jnp.sum(jnp.where(onehot, before + run, 0.0), axis=0, keepdims=True)
    run_ref[...] = run + jnp.sum(ones, axis=1, keepdims=True)
    return rank.astype(jnp.int32)


def _post_kernel(x_ref, oa_ref, ob_ref, mod_ref, g1_ref, g1p_ref, g2_ref,
                 wg_ref, bg_ref, wpa_ref, wpb_ref, wo_ref, wr_ref, br_ref, tri_ref,
                 x1_ref, h2_ref, cls_ref, rank_ref, cnt_ref, run_ref):
    @pl.when((pl.program_id(0) == 0) & (pl.program_id(1) == 0))
    def _():
        run_ref[...] = jnp.zeros_like(run_ref)

    nb, rows, _ = x_ref.shape
    stack = lambda parts: parts[0] if nb == 1 else jnp.concatenate(parts, axis=0)
    xs = [x_ref[g] for g in range(nb)]
    h = stack([_norm_mod(xs[g], g1_ref[...], mod_ref[g, 1:2, :], mod_ref[g, 0:1, :]) for g in range(nb)]).astype(BF16)
    gates = jax.nn.sigmoid(jnp.dot(h, wg_ref[...], preferred_element_type=F32) + bg_ref[...])
    pa = jnp.dot(stack([oa_ref[g] for g in range(nb)]), wpa_ref[...], preferred_element_type=F32)
    pb = jnp.dot(stack([ob_ref[g] for g in range(nb)]), wpb_ref[...], preferred_element_type=F32)
    mixed = gates[:, :D_MODEL] * pa + gates[:, D_MODEL:] * pb
    y = jnp.dot(mixed.astype(BF16), wo_ref[...], preferred_element_type=F32)
    h2_parts = []
    for g in range(nb):
        x1 = xs[g] + mod_ref[g, 2:3, :] * _rms(y[g * rows:(g + 1) * rows], g1p_ref[...])
        x1_ref[g] = x1
        h2_parts.append(_norm_mod(x1, g2_ref[...], mod_ref[g, 4:5, :], mod_ref[g, 3:4, :]))
    h2 = stack(h2_parts).astype(BF16)
    cls, w_lo, w_hi = _route(_route_logits(wr_ref, br_ref, h2))
    cls_ref[0] = cls
    rank_ref[0] = _rank_in_class(cls, run_ref, tri_ref)
    cnt_ref[...] = run_ref[...]
    t = nb * rows
    lane = lax.broadcasted_iota(jnp.int32, (t, LANES), 1)
    h2f = h2.astype(F32)
    for c in range(D_MODEL // LANES):
        h2_ref[pl.ds(c, t, stride=IN_SLAB), :] = h2f[:, c * LANES:(c + 1) * LANES]
    h2_ref[pl.ds(D_MODEL // LANES, t, stride=IN_SLAB), :] = jnp.where(
        lane == 0, _col_from_row(w_lo, t), jnp.where(lane == 1, _col_from_row(w_hi, t), 0.0))


def _post(x, oa, ob, mods, g_pre_mix, g_post_mix, g_pre_ffn, wg, bg, wpa, wpb, wo, wr, br, tile):
    b, s, d = x.shape
    rows = min(s, tile)
    nb = math.gcd(tile // rows, b)
    tile = nb * rows
    assert s % rows == 0
    nt = s // rows
    n_steps = (b // nb) * nt
    tri = (jnp.arange(tile)[:, None] < jnp.arange(tile)[None, :]).astype(BF16)
    tok = lambda w: pl.BlockSpec((nb, rows, w), lambda i, j: (i, j, 0))
    const = lambda shp: pl.BlockSpec(shp, lambda i, j: (0,) * len(shp))
    per_tile = pl.BlockSpec((1, 1, tile), lambda i, j: (i * nt + j, 0, 0))
    return pl.pallas_call(
        _post_kernel,
        out_shape=(
            jax.ShapeDtypeStruct((b, s, d), F32),
            jax.ShapeDtypeStruct((b * s * IN_SLAB, LANES), F32),
            jax.ShapeDtypeStruct((n_steps, 1, tile), jnp.int32),
            jax.ShapeDtypeStruct((n_steps, 1, tile), jnp.int32),
            jax.ShapeDtypeStruct((CLASS_ROWS, 1), F32),
        ),
        grid=(b // nb, nt),
        in_specs=[
            tok(d), tok(QA_W), tok(B_W),
            pl.BlockSpec((nb, 6, d), lambda i, j: (i, 0, 0)),
            const((1, d)), const((1, d)), const((1, d)),
            const(wg.shape), const(bg.shape), const(wpa.shape), const(wpb.shape), const(wo.shape),
            const(wr.shape), const(br.shape), const(tri.shape),
        ],
        out_specs=(tok(d), pl.BlockSpec((tile * IN_SLAB, LANES), lambda i, j: (i * nt + j, 0)),
                   per_tile, per_tile, const((CLASS_ROWS, 1))),
        scratch_shapes=[pltpu.VMEM((CLASS_ROWS, 1), F32)],
        compiler_params=_cparams("arbitrary", "arbitrary"),
        name="post",
    )(x, oa, ob, mods, g_pre_mix, g_post_mix, g_pre_ffn, wg, bg, wpa, wpb, wo, wr, br, tri)


def _invert_kernel(upos_ref, row_ref, *, n, scale):
    def body(i, carry):
        for u in range(ISSUE_UNROLL):
            t = i * ISSUE_UNROLL + u
            row_ref[upos_ref[t]] = t * scale
        return carry
    lax.fori_loop(0, n // ISSUE_UNROLL, body, 0)


def _invert(upos, scale):
    n = upos.shape[0]
    return pl.pallas_call(
        functools.partial(_invert_kernel, n=n, scale=scale),
        out_shape=jax.ShapeDtypeStruct((n,), jnp.int32),
        in_specs=[pl.BlockSpec(memory_space=pltpu.SMEM)],
        out_specs=pl.BlockSpec(memory_space=pltpu.SMEM),
        name="invert",
    )(upos)


def _expert(xb, wgu_ref, wd_ref):
    gu = jnp.dot(xb, wgu_ref[0], preferred_element_type=F32)
    gate = gu[:, :D_EXPERT]
    he = (gate * jax.nn.sigmoid(gate)) * gu[:, D_EXPERT:]
    return jnp.dot(he.astype(BF16), wd_ref[0], preferred_element_type=F32)


def _moe_kernel(tok_ref, base_ref, ea_ref, eb_ref, nu_ref, h_ref, wgu_a_ref, wd_a_ref, wgu_b_ref, wd_b_ref,
                ys_ref, xbuf, sems, *, tile, n_tok):
    t = pl.program_id(0)
    nu = nu_ref[0]
    slot = t % 2

    def slab_copy(src_row0, dst_slot, r):
        return pltpu.make_async_copy(h_ref.at[pl.ds(src_row0, IN_SLAB)],
                                     xbuf.at[dst_slot, pl.ds(r * IN_SLAB, IN_SLAB)], sems.at[dst_slot])

    def fetch(step, dst_slot):
        base = base_ref[step]
        for r in range(tile):
            slab_copy(tok_ref[jnp.minimum(base + r, n_tok - 1)], dst_slot, r).start(priority=r % 2)

    def wait(dst_slot):
        for r in range(tile):
            slab_copy(0, dst_slot, 0).wait()

    @pl.when(t == 0)
    def _():
        fetch(0, 0)

    @pl.when(t < nu)
    def _():
        wait(slot)
        fetch(jnp.minimum(t + 1, nu - 1), 1 - slot)
        lane_tile = lambda c: xbuf[slot, pl.ds(c, tile, stride=IN_SLAB), :]
        xb = jnp.concatenate([lane_tile(c) for c in range(D_MODEL // LANES)], axis=1).astype(BF16)
        w = lane_tile(D_MODEL // LANES)
        y = w[:, 0:1] * _expert(xb, wgu_a_ref, wd_a_ref) + w[:, 1:2] * _expert(xb, wgu_b_ref, wd_b_ref)
        for c in range(OUT_SLAB):
            ys_ref[pl.ds(c, tile, stride=OUT_SLAB), :] = y[:, c * LANES:(c + 1) * LANES]

    @pl.when(t == nu - 1)
    def _():
        wait(1 - slot)

    @pl.when(t >= nu)
    def _():
        ys_ref[...] = jnp.zeros_like(ys_ref)


def _moe_grouped(h_slabs, sorted_tok, tile_base, tile_ea, tile_eb, n_used, wgu, wd, tile, n_tiles):
    d = D_MODEL
    n_tok = sorted_tok.shape[0]
    expert = lambda which, shp: pl.BlockSpec(shp, lambda t, tok, base, ea, eb, nu: ((ea, eb)[which][t], 0, 0))
    return pl.pallas_call(
        functools.partial(_moe_kernel, tile=tile, n_tok=n_tok),
        out_shape=jax.ShapeDtypeStruct((n_tiles * tile * OUT_SLAB, LANES), F32),
        grid_spec=pltpu.PrefetchScalarGridSpec(
            num_scalar_prefetch=5,
            grid=(n_tiles,),
            in_specs=[
                pl.BlockSpec(memory_space=pl.ANY),
                expert(0, (1, d, 2 * D_EXPERT)), expert(0, (1, D_EXPERT, d)),
                expert(1, (1, d, 2 * D_EXPERT)), expert(1, (1, D_EXPERT, d)),
            ],
            out_specs=pl.BlockSpec((tile * OUT_SLAB, LANES), lambda t, *_: (t, 0)),
            scratch_shapes=[pltpu.VMEM((2, tile * IN_SLAB, LANES), F32), pltpu.SemaphoreType.DMA((2,))],
        ),
        compiler_params=_cparams("arbitrary"),
        name="moe_grouped",
    )(sorted_tok, tile_base, tile_ea, tile_eb, n_used, h_slabs, wgu, wd, wgu, wd)


_PAIR_LO, _PAIR_HI = np.triu_indices(EXPERTS_PER_GROUP, k=1)
_CLASS_LO = np.concatenate([g * EXPERTS_PER_GROUP + _PAIR_LO for g in range(N_GROUPS)]).astype(np.int32)
_CLASS_HI = np.concatenate([g * EXPERTS_PER_GROUP + _PAIR_HI for g in range(N_GROUPS)]).astype(np.int32)


def _moe_plan(counts, cls, rank, n, tile):
    p_max = -(-(n + N_CLASSES * (tile - 1)) // tile) * tile
    nt = p_max // tile
    counts = counts[:N_CLASSES].astype(jnp.int32)
    padded = ((counts + tile - 1) // tile) * tile
    pad_end = jnp.cumsum(padded)
    gstart = pad_end - padded
    ustart = jnp.cumsum(counts) - counts
    onehot = cls[:, None] == jnp.arange(N_CLASSES, dtype=jnp.int32)[None, :]
    pos = rank + jnp.sum(jnp.where(onehot, gstart[None, :], 0), axis=1)
    upos = rank + jnp.sum(jnp.where(onehot, ustart[None, :], 0), axis=1)
    n_used = (pad_end[-1] // tile).astype(jnp.int32).reshape(1)
    tiles = jnp.arange(nt, dtype=jnp.int32)
    tile_cls = jnp.minimum(jnp.sum((pad_end[None, :] <= (tiles * tile)[:, None]).astype(jnp.int32), axis=1), N_CLASSES - 1)
    tile_ea = jnp.asarray(_CLASS_LO)[tile_cls]
    tile_eb = jnp.asarray(_CLASS_HI)[tile_cls]
    tile_base = jnp.clip(ustart[tile_cls] + tiles * tile - gstart[tile_cls], 0, n - 1)
    return nt, pos, upos, tile_base, tile_ea, tile_eb, n_used


def _final_kernel(pos_ref, x1_ref, ys_ref, mod_ref, g_ref, o_ref, ybuf, sems, *, tile, n_steps):
    s = pl.program_id(0)
    slot = s % 2

    def slab_copy(src_row0, dst_slot, dst_row):
        return pltpu.make_async_copy(ys_ref.at[pl.ds(src_row0, OUT_SLAB)],
                                     ybuf.at[dst_slot, pl.ds(dst_row * OUT_SLAB, OUT_SLAB)], sems.at[dst_slot])

    def fetch(step, dst_slot):
        for r in range(tile):
            slab_copy(pos_ref[step * tile + r], dst_slot, r).start(priority=r % 2)

    def wait(dst_slot):
        for r in range(tile):
            slab_copy(0, dst_slot, 0).wait()

    @pl.when(s == 0)
    def _():
        fetch(0, 0)

    wait(slot)
    fetch(jnp.minimum(s + 1, n_steps - 1), 1 - slot)
    y = jnp.concatenate([ybuf[slot, pl.ds(c, tile, stride=OUT_SLAB), :] for c in range(OUT_SLAB)], axis=1)
    o_ref[...] = x1_ref[...] + mod_ref[0, 5:6, :] * _rms(y, g_ref[...])

    @pl.when(s == n_steps - 1)
    def _():
        wait(1 - slot)


def _final(x1, ys, pos, mods, g_post_ffn, tile):
    b, s, d = x1.shape
    n = b * s
    per_b = s // tile
    n_steps = n // tile
    tok = pl.BlockSpec((tile, d), lambda i, *_: (i, 0))
    out = pl.pallas_call(
        functools.partial(_final_kernel, tile=tile, n_steps=n_steps),
        out_shape=jax.ShapeDtypeStruct((n, d), F32),
        grid_spec=pltpu.PrefetchScalarGridSpec(
            num_scalar_prefetch=1,
            grid=(n_steps,),
            in_specs=[
                tok,
                pl.BlockSpec(memory_space=pl.ANY),
                pl.BlockSpec((1, 6, d), lambda i, *_: (i // per_b, 0, 0)),
                pl.BlockSpec((1, d), lambda i, *_: (0, 0)),
            ],
            out_specs=tok,
            scratch_shapes=[pltpu.VMEM((2, tile * OUT_SLAB, LANES), F32), pltpu.SemaphoreType.DMA((2,))],
        ),
        compiler_params=_cparams("arbitrary"),
        name="final",
    )(pos, x1.reshape(n, d), ys, mods, g_post_ffn)
    return out.reshape(b, s, d)


def _t5_bucket(rel):
    half = T5_BUCKETS // 2
    exact = half // 2
    ret = jnp.where(rel > 0, half, 0)
    n = jnp.abs(rel)
    nf = jnp.maximum(n, 1).astype(F32)
    large = exact + (jnp.log(nf / exact) / math.log(T5_MAX_DIST / exact) * (half - exact)).astype(jnp.int32)
    large = jnp.minimum(large, half - 1)
    return ret + jnp.where(n < exact, n, large)


def _toeplitz(u, n_rows, n_cols):
    return jnp.stack([u[..., n_rows - 1 - i:n_rows - 1 - i + n_cols] for i in range(n_rows)], axis=-2)


def _bias_tables(t5_table, rel_table):
    ja = jnp.arange(CHUNK - 1 + A_BAND)
    ua = t5_table[_t5_bucket(ja - (CHUNK - 1) - A_WINDOW)].T.astype(F32)
    jb = jnp.arange(CHUNK - 1 + B_BAND)
    ub = rel_table[:, jnp.clip((CHUNK - 1) - jb + B_REACH, -REL_CLIP, REL_CLIP) + REL_CLIP].astype(F32)
    return _toeplitz(ua, CHUNK, A_BAND), _toeplitz(ub, CHUNK, B_BAND)


def _two_chunk(bias):
    masked = jnp.full(bias.shape[:-1] + (CHUNK,), MASKED, F32)
    return jnp.concatenate([jnp.concatenate([bias, masked], axis=-1), jnp.concatenate([masked, bias], axis=-1)], axis=-2)


def _pair_tables(bias_a, bias_b, sinks):
    a2, b2 = _two_chunk(bias_a), _two_chunk(bias_b)
    stack2 = lambda x, h0, h1: jnp.concatenate([x[h0], x[h1]], axis=0)
    bias2_a = jnp.stack([jnp.concatenate([stack2(a2, 4 * kv, 4 * kv + 2), stack2(a2, 4 * kv + 1, 4 * kv + 3)], axis=1)
                         for kv in range(A_KV_HEADS)])
    bias2_b = jnp.concatenate([b2[0::2], b2[1::2]], axis=-1)
    sk = jnp.broadcast_to(sinks.astype(F32).reshape(A_Q_HEADS, 1, 1), (A_Q_HEADS, ATT_BLOCK, 1))
    sinks2 = jnp.stack([jnp.stack([stack2(sk, 4 * kv, 4 * kv + 2), stack2(sk, 4 * kv + 1, 4 * kv + 3)])
                        for kv in range(A_KV_HEADS)])
    return bias2_a, sinks2, bias2_b


def _moe_and_final(x1, h2e, cls, rank, counts, mods, g_post_ffn, wgu, wd, seq_tile, moe_tile):
    b, s, d = x1.shape
    n = b * s
    n_tiles, pos, upos, tile_base, tile_ea, tile_eb, n_used = _moe_plan(
        counts.reshape(CLASS_ROWS), cls.reshape(n), rank.reshape(n), n, moe_tile)
    ys = _moe_grouped(h2e, _invert(upos, IN_SLAB), tile_base, tile_ea, tile_eb, n_used, wgu, wd, moe_tile, n_tiles)
    return _final(x1, ys, pos * OUT_SLAB, mods, g_post_ffn, seq_tile)


def kernel(x_prompt, x_sample, c_prompt, c_sample, cache_a_k, cache_a_v, cache_b_k, cache_b_v, w_ada, b_ada, g_pre_mix, g_post_mix, g_pre_ffn, g_post_ffn, w_in, a_sinks, t5_table, b_rel_table, w_proj_a, w_proj_b, w_gate, b_gate, w_o, w_route_g, b_route_g, w_route_e, b_route_e, w_e_gate, w_e_up, w_e_down):
    depth = w_in.shape[0]
    assert depth == 1
    l = 0
    bp, sp, d = x_prompt.shape
    bs, ts, _ = x_sample.shape

    mods = _ada(jnp.concatenate([c_prompt, c_sample], axis=0), w_ada[l], b_ada[l]).reshape(bp + bs, 6, d)
    mods_p, mods_s = mods[:bp], mods[bp:]

    w_in_bf = w_in[l].astype(BF16)
    wg, wpa, wpb, wo = w_gate[l].astype(BF16), w_proj_a[l].astype(BF16), w_proj_b[l].astype(BF16), w_o[l].astype(BF16)
    bg = b_gate[l].reshape(1, 2 * d)
    pad_rows = ROUTE_ROWS - N_EXPERTS - N_GROUPS
    wr = jnp.concatenate([w_route_e[l].T, w_route_g[l].T, jnp.zeros((pad_rows, d), F32)], axis=0).astype(BF16)
    br = jnp.concatenate([b_route_e[l], b_route_g[l], jnp.zeros((pad_rows,), F32)]).reshape(ROUTE_ROWS, 1)
    wgu = jnp.concatenate([w_e_gate[l], w_e_up[l]], axis=-1).astype(BF16)
    wd = w_e_down[l].astype(BF16)
    g1, g1p, g2, g2p = (g[l].reshape(1, d) for g in (g_pre_mix, g_post_mix, g_pre_ffn, g_post_ffn))

    bias_a, bias_b = _bias_tables(t5_table, b_rel_table[l])
    bias2_a, sinks2, bias2_b = _pair_tables(bias_a, bias_b, a_sinks[l])
    bias_a = bias_a.reshape(A_KV_HEADS, A_GROUP * CHUNK, A_BAND)
    sinks = jnp.broadcast_to(a_sinks[l].reshape(A_KV_HEADS, A_GROUP, 1, 1), (A_KV_HEADS, A_GROUP, CHUNK, 1))
    sinks = sinks.reshape(A_KV_HEADS, A_GROUP * CHUNK, 1).astype(F32)

    qa, ka, va, qb, kb, vb, sak_p, sav_p, sbk_p, sbv_p = _pre_prompt(x_prompt, mods_p, g1, w_in_bf)
    oa, ob = _attn_prompt(qa, ka, va, qb, kb, vb, bias2_a, sinks2, bias2_b)
    x1, h2e, cls, rank, counts = _post(x_prompt, oa, ob, mods_p, g1, g1p, g2, wg, bg, wpa, wpb, wo, wr, br, SEQ_TILE)
    y_prompt = _moe_and_final(x1, h2e, cls, rank, counts, mods_p, g2p, wgu, wd, SEQ_TILE, 256)

    la, lb = cache_a_k.shape[2], cache_b_k.shape[2]
    cak = cache_a_k[l].reshape(bs, la, KVA_W)
    cav = cache_a_v[l].reshape(bs, la, KVA_W)
    cbk = cache_b_k[l].reshape(bs, lb, B_W)
    cbv = cache_b_v[l].reshape(bs, lb, B_W)
    qa, ka, va, qb, kb, vb, sak_s, sav_s, sbk_s, sbv_s = _pre_sample(x_sample, mods_s, g1, w_in_bf, cak, cav, cbk, cbv)
    oa, ob = _attn_sample(qa, ka, va, qb, kb, vb, bias_a, sinks, bias_b)
    x1, h2e, cls, rank, counts = _post(x_sample, oa, ob, mods_s, g1, g1p, g2, wg, bg, wpa, wpb, wo, wr, br, SEQ_TILE)
    y_sample = _moe_and_final(x1, h2e, cls, rank, counts, mods_s, g2p, wgu, wd, ts, 64)

    a_state = lambda v, b, r: v.reshape(1, b, r, A_KV_HEADS, HEAD_DIM)
    b_state = lambda v, b, r: v.reshape(1, b, r, B_HEADS, HEAD_DIM)
    return (y_prompt, y_sample,
            a_state(sak_p, bp, A_WINDOW), a_state(sav_p, bp, A_WINDOW),
            b_state(sbk_p, bp, B_REACH), b_state(sbv_p, bp, B_REACH),
            a_state(sak_s, bs, la), a_state(sav_s, bs, la),
            b_state(sbk_s, bs, lb), b_state(sbv_s, bs, lb))
```

```python
import functools
import math

import numpy as np
import jax
import jax.numpy as jnp
from jax import lax
from jax.experimental import pallas as pl
from jax.experimental.pallas import tpu as pltpu

D_MODEL = 1024
CHUNK = 64
HEAD_DIM = 64
A_Q_HEADS = 8
A_KV_HEADS = 2
A_GROUP = A_Q_HEADS // A_KV_HEADS
A_WINDOW = 128
A_BACK = A_WINDOW // CHUNK
B_HEADS = 8
B_BACK = 8
B_REACH = B_BACK * CHUNK
REL_CLIP = 256
T5_BUCKETS = 32
T5_MAX_DIST = 128
N_GROUPS = 4
EXPERTS_PER_GROUP = 8
N_EXPERTS = N_GROUPS * EXPERTS_PER_GROUP
D_EXPERT = D_MODEL // 4
EPS = 1e-6

QA_W = A_Q_HEADS * HEAD_DIM
KVA_W = A_KV_HEADS * HEAD_DIM
B_W = B_HEADS * HEAD_DIM
IN_W = QA_W + 2 * KVA_W + 3 * B_W
A_BAND = A_WINDOW + CHUNK
B_BAND = B_REACH + CHUNK

PAIRS_PER_GROUP = EXPERTS_PER_GROUP * (EXPERTS_PER_GROUP - 1) // 2
N_CLASSES = N_GROUPS * PAIRS_PER_GROUP
ROUTE_ROWS = 40
CLASS_ROWS = 128
LANES = 128
IN_SLAB = D_MODEL // LANES + 1
OUT_SLAB = D_MODEL // LANES

F32 = jnp.float32
BF16 = jnp.bfloat16

VMEM_LIMIT_BYTES = 56 * 1024 * 1024

SEQ_TILE = 512
ISSUE_UNROLL = 32


def _cparams(*sem):
    return pltpu.CompilerParams(dimension_semantics=sem, vmem_limit_bytes=VMEM_LIMIT_BYTES)


def _norm_mod(x, g, scale, shift):
    y = x * lax.rsqrt(jnp.mean(x * x, axis=-1, keepdims=True) + EPS)
    return (y * g) * (1.0 + scale) + shift


def _rms(x, g):
    return (x * lax.rsqrt(jnp.mean(x * x, axis=-1, keepdims=True) + EPS)) * g


def _ada_kernel(c_ref, w_ref, b_ref, o_ref):
    c = c_ref[...]
    s = (c * jax.nn.sigmoid(c)).astype(BF16)
    o_ref[...] = jnp.dot(s, w_ref[...].astype(BF16), preferred_element_type=F32) + b_ref[...]


def _ada(c, w_ada, b_ada):
    n, d = c.shape
    wn = w_ada.shape[1]
    tn = 512
    return pl.pallas_call(
        _ada_kernel,
        out_shape=jax.ShapeDtypeStruct((n, wn), F32),
        grid=(wn // tn,),
        in_specs=[
            pl.BlockSpec((n, d), lambda j: (0, 0)),
            pl.BlockSpec((d, tn), lambda j: (0, j)),
            pl.BlockSpec((1, tn), lambda j: (0, j)),
        ],
        out_specs=pl.BlockSpec((n, tn), lambda j: (0, j)),
        compiler_params=_cparams("arbitrary"),
        name="ada",
    )(c, w_ada, b_ada.reshape(1, wn))


_COL_QA = (0, QA_W)
_COL_KA = (QA_W, QA_W + KVA_W)
_COL_VA = (QA_W + KVA_W, QA_W + 2 * KVA_W)
_COL_QB = (QA_W + 2 * KVA_W, QA_W + 2 * KVA_W + B_W)
_COL_KB = (_COL_QB[1], _COL_QB[1] + B_W)
_COL_VB = (_COL_KB[1], _COL_KB[1] + B_W)
Q_SCALE = HEAD_DIM ** -0.5


def _project(x_ref, mod_ref, g_ref, w_ref):
    h = _norm_mod(x_ref[0], g_ref[...], mod_ref[0, 1:2, :], mod_ref[0, 0:1, :])
    return jnp.dot(h.astype(BF16), w_ref[...], preferred_element_type=F32)


def _cols(p, c):
    return p[:, c[0]:c[1]]


def _pre_prompt_kernel(x_ref, mod_ref, g_ref, w_ref,
                       qa_ref, ka_ref, va_ref, qb_ref, kb_ref, vb_ref,
                       sak_ref, sav_ref, sbk_ref, sbv_ref, *, n_tiles, tile):
    p = _project(x_ref, mod_ref, g_ref, w_ref)
    qa_ref[0] = (_cols(p, _COL_QA) * Q_SCALE).astype(BF16)
    ka_ref[0] = _cols(p, _COL_KA).astype(BF16)
    va_ref[0] = _cols(p, _COL_VA).astype(BF16)
    qb_ref[0] = (_cols(p, _COL_QB) * Q_SCALE).astype(BF16)
    kb_ref[0] = _cols(p, _COL_KB).astype(BF16)
    vb_ref[0] = _cols(p, _COL_VB).astype(BF16)

    @pl.when(pl.program_id(1) == n_tiles - 1)
    def _():
        sak_ref[0] = _cols(p, _COL_KA)[tile - A_WINDOW:, :]
        sav_ref[0] = _cols(p, _COL_VA)[tile - A_WINDOW:, :]
        sbk_ref[0] = _cols(p, _COL_KB)[tile - B_REACH:, :]
        sbv_ref[0] = _cols(p, _COL_VB)[tile - B_REACH:, :]


def _pre_prompt(x, mods, g_pre, w_in_bf):
    b, s, d = x.shape
    tile = SEQ_TILE
    assert s % tile == 0 and tile >= B_REACH and s >= B_REACH
    nt = s // tile
    tok = lambda w: pl.BlockSpec((1, tile, w), lambda i, j: (i, j, 0))
    state = lambda r, w: pl.BlockSpec((1, r, w), lambda i, j: (i, 0, 0))
    return pl.pallas_call(
        functools.partial(_pre_prompt_kernel, n_tiles=nt, tile=tile),
        out_shape=(
            jax.ShapeDtypeStruct((b, s, QA_W), BF16),
            jax.ShapeDtypeStruct((b, s, KVA_W), BF16),
            jax.ShapeDtypeStruct((b, s, KVA_W), BF16),
            jax.ShapeDtypeStruct((b, s, B_W), BF16),
            jax.ShapeDtypeStruct((b, s, B_W), BF16),
            jax.ShapeDtypeStruct((b, s, B_W), BF16),
            jax.ShapeDtypeStruct((b, A_WINDOW, KVA_W), F32),
            jax.ShapeDtypeStruct((b, A_WINDOW, KVA_W), F32),
            jax.ShapeDtypeStruct((b, B_REACH, B_W), F32),
            jax.ShapeDtypeStruct((b, B_REACH, B_W), F32),
        ),
        grid=(b, nt),
        in_specs=[
            tok(d),
            pl.BlockSpec((1, 6, d), lambda i, j: (i, 0, 0)),
            pl.BlockSpec((1, d), lambda i, j: (0, 0)),
            pl.BlockSpec((d, IN_W), lambda i, j: (0, 0)),
        ],
        out_specs=(
            tok(QA_W), tok(KVA_W), tok(KVA_W), tok(B_W), tok(B_W), tok(B_W),
            state(A_WINDOW, KVA_W), state(A_WINDOW, KVA_W), state(B_REACH, B_W), state(B_REACH, B_W),
        ),
        compiler_params=_cparams("parallel", "arbitrary"),
        name="pre_prompt",
    )(x, mods, g_pre, w_in_bf)


def _pre_sample_kernel(x_ref, mod_ref, g_ref, w_ref, cak_ref, cav_ref, cbk_ref, cbv_ref,
                       qa_ref, ka_ref, va_ref, qb_ref, kb_ref, vb_ref,
                       sak_ref, sav_ref, sbk_ref, sbv_ref, *, t):
    p = _project(x_ref, mod_ref, g_ref, w_ref)
    qa_ref[0] = (_cols(p, _COL_QA) * Q_SCALE).astype(BF16)
    qb_ref[0] = (_cols(p, _COL_QB) * Q_SCALE).astype(BF16)
    for cache_ref, new_ref, state_ref, col, heads in (
            (cak_ref, ka_ref, sak_ref, _COL_KA, A_KV_HEADS), (cav_ref, va_ref, sav_ref, _COL_VA, A_KV_HEADS),
            (cbk_ref, kb_ref, sbk_ref, _COL_KB, B_HEADS), (cbv_ref, vb_ref, sbv_ref, _COL_VB, B_HEADS)):
        new = _cols(p, col)
        new_ref[0] = new.astype(BF16)
        n_rows = cache_ref.shape[1]
        kept = n_rows - t * heads
        state_ref[0, 0:kept, :] = cache_ref[0, t * heads:n_rows, :]
        for h in range(heads):
            state_ref[0, pl.ds(kept + h, t, stride=heads), :] = new[:, h * HEAD_DIM:(h + 1) * HEAD_DIM]


def _pre_sample(x, mods, g_pre, w_in_bf, cak, cav, cbk, cbv):
    b, t, d = x.shape
    assert t == CHUNK and cak.shape[1] == A_WINDOW * A_KV_HEADS and cbk.shape[1] == B_REACH * B_HEADS
    per_b = lambda shp: pl.BlockSpec((1,) + tuple(shp[1:]), lambda i: (i, 0, 0))
    new_rows = [(b, t, QA_W), (b, t, KVA_W), (b, t, KVA_W), (b, t, B_W), (b, t, B_W), (b, t, B_W)]
    caches = (cak, cav, cbk, cbv)
    return pl.pallas_call(
        functools.partial(_pre_sample_kernel, t=t),
        out_shape=tuple(jax.ShapeDtypeStruct(shp, BF16) for shp in new_rows)
        + tuple(jax.ShapeDtypeStruct(c.shape, F32) for c in caches),
        grid=(b,),
        in_specs=[
            per_b(x.shape),
            per_b(mods.shape),
            pl.BlockSpec((1, d), lambda i: (0, 0)),
            pl.BlockSpec((d, IN_W), lambda i: (0, 0)),
        ] + [per_b(c.shape) for c in caches],
        out_specs=tuple(per_b(shp) for shp in new_rows) + tuple(per_b(c.shape) for c in caches),
        compiler_params=_cparams("parallel"),
        name="pre_sample",
    )(x, mods, g_pre, w_in_bf, *caches)


def _attend_a(q, ks, vs, bias_ref, sink_ref):
    outs = []
    for kv in range(A_KV_HEADS):
        qs = jnp.concatenate(
            [q[:, (kv * A_GROUP + g) * HEAD_DIM:(kv * A_GROUP + g + 1) * HEAD_DIM] for g in range(A_GROUP)], axis=0)
        s = lax.dot_general(qs, ks[kv], (((1,), (1,)), ((), ())), preferred_element_type=F32) + bias_ref[kv]
        sk = sink_ref[kv]
        mx = jnp.maximum(jnp.max(s, axis=-1, keepdims=True), sk)
        e = jnp.exp(s - mx)
        den = jnp.sum(e, axis=-1, keepdims=True) + jnp.exp(sk - mx)
        o = jnp.dot(e.astype(BF16), vs[kv], preferred_element_type=F32) / den
        outs.extend(o[g * CHUNK:(g + 1) * CHUNK, :] for g in range(A_GROUP))
    return jnp.concatenate(outs, axis=1).astype(BF16)


def _attend_b(q, ks, vs, bias_ref):
    outs = []
    for h in range(B_HEADS):
        s = lax.dot_general(q[:, h * HEAD_DIM:(h + 1) * HEAD_DIM], ks[h], (((1,), (1,)), ((), ())),
                            preferred_element_type=F32) + bias_ref[h]
        mx = jnp.max(s, axis=-1, keepdims=True)
        e = jnp.exp(s - mx)
        den = jnp.sum(e, axis=-1, keepdims=True)
        outs.append(jnp.dot(e.astype(BF16), vs[h], preferred_element_type=F32) / den)
    return jnp.concatenate(outs, axis=1).astype(BF16)


MASKED = -1e30
PAIR_W = 2 * HEAD_DIM
ATT_BLOCK = 2 * CHUNK
A_COLS = A_WINDOW + ATT_BLOCK
B_COLS = B_REACH + ATT_BLOCK


def _pair_attend(q2, k2, v2, ind, bias, sinks):
    c = k2.shape[0] // 2
    s = lax.dot_general(q2, k2, (((1,), (1,)), ((), ())), preferred_element_type=F32) + bias
    es, mxs = [], []
    for half in range(2):
        sh = s[:, half * c:(half + 1) * c]
        mx = jnp.max(sh, axis=-1, keepdims=True)
        if sinks is not None:
            mx = jnp.maximum(mx, sinks[half])
        es.append(jnp.exp(sh - mx).astype(BF16))
        mxs.append(mx)
    o = jnp.dot(jnp.concatenate(es, axis=1), jnp.concatenate([v2, ind], axis=1), preferred_element_type=F32)
    dens = [o[:, PAIR_W + half:PAIR_W + half + 1] for half in range(2)]
    if sinks is not None:
        dens = [dens[half] + jnp.exp(sinks[half] - mxs[half]) for half in range(2)]
    lane = lax.broadcasted_iota(jnp.int32, (o.shape[0], PAIR_W), 1)
    return o[:, :PAIR_W] / jnp.where(lane < HEAD_DIM, dens[0], dens[1])


def _last_cols(table, full, n):
    if n == full:
        return table
    return jnp.concatenate([table[:, full - n:full], table[:, 2 * full - n:2 * full]], axis=1)


def _attn_prompt_kernel(qa_ref, ka_ref, va_ref, qb_ref, kb_ref, vb_ref, ba_ref, sk_ref, bb_ref, lo_ref, hi_ref,
                        ia_ref, ib_ref, oa_ref, ob_ref, ak_ref, av_ref, bk_ref, bv_ref, *, tile):
    j = pl.program_id(1)

    @pl.when(j == 0)
    def _():
        lo, hi = lo_ref[:, 0:PAIR_W], hi_ref[:, 0:PAIR_W]
        for src, dst in ((ka_ref, ak_ref), (va_ref, av_ref)):
            x = src[0]
            swapped = jnp.concatenate([x[:, HEAD_DIM:], x[:, :HEAD_DIM]], axis=1)
            dst[0] = x * lo
            dst[1] = swapped * hi
            dst[2] = swapped * lo
            dst[3] = x * hi
        for src, dst in ((kb_ref, bk_ref), (vb_ref, bv_ref)):
            x = src[0]
            dst[0] = x * lo_ref[...]
            dst[1] = x * hi_ref[...]

    def block(g, r0, n_a, n_b):
        rows = pl.ds(r0, ATT_BLOCK)
        band_a = pl.ds(g + ATT_BLOCK - n_a, n_a)
        band_b = pl.ds(g + ATT_BLOCK - n_b, n_b)
        ind_a = jnp.concatenate([ia_ref[0:n_a, :], ia_ref[A_COLS:A_COLS + n_a, :]], axis=0)
        ind_b = jnp.concatenate([ib_ref[0:n_b, :], ib_ref[B_COLS:B_COLS + n_b, :]], axis=0)
        for kv in range(A_KV_HEADS):
            c0, c1 = slice(2 * kv * PAIR_W, (2 * kv + 1) * PAIR_W), slice((2 * kv + 1) * PAIR_W, (2 * kv + 2) * PAIR_W)
            q2 = jnp.concatenate([qa_ref[0, rows, c0], qa_ref[0, rows, c1]], axis=0)
            k2 = jnp.concatenate([ak_ref[2 * kv, band_a, :], ak_ref[2 * kv + 1, band_a, :]], axis=0)
            v2 = jnp.concatenate([av_ref[2 * kv, band_a, :], av_ref[2 * kv + 1, band_a, :]], axis=0)
            o = _pair_attend(q2, k2, v2, ind_a, _last_cols(ba_ref[kv], A_COLS, n_a), (sk_ref[kv, 0], sk_ref[kv, 1]))
            oa_ref[0, rows, c0] = o[0:ATT_BLOCK].astype(BF16)
            oa_ref[0, rows, c1] = o[ATT_BLOCK:].astype(BF16)
        for p in range(B_HEADS // 2):
            cols = slice(p * PAIR_W, (p + 1) * PAIR_W)
            k2 = jnp.concatenate([bk_ref[0, band_b, cols], bk_ref[1, band_b, cols]], axis=0)
            v2 = jnp.concatenate([bv_ref[0, band_b, cols], bv_ref[1, band_b, cols]], axis=0)
            o = _pair_attend(qb_ref[0, rows, cols], k2, v2, ind_b, _last_cols(bb_ref[p], B_COLS, n_b), None)
            ob_ref[0, rows, cols] = o.astype(BF16)

    n_blocks = tile // ATT_BLOCK

    @pl.when(j == 0)
    def _():
        for blk in range(n_blocks):
            g = blk * ATT_BLOCK
            block(g, g, min(g + ATT_BLOCK, A_COLS), min(g + ATT_BLOCK, B_COLS))

    @pl.when(j > 0)
    def _():
        def body(blk, carry):
            r0 = pl.multiple_of(blk * ATT_BLOCK, ATT_BLOCK)
            block(pl.multiple_of(j * tile + r0, ATT_BLOCK), r0, A_COLS, B_COLS)
            return carry
        lax.fori_loop(0, n_blocks, body, 0)


def _head_indicator(c):
    ind = np.zeros((2 * c, PAIR_W), np.float32)
    ind[:c, 0] = 1.0
    ind[c:, 1] = 1.0
    return ind


def _attn_prompt(qa, ka, va, qb, kb, vb, bias2_a, sinks2, bias2_b):
    b, s, _ = qa.shape
    tile = SEQ_TILE
    assert s % tile == 0 and tile % ATT_BLOCK == 0
    lane = np.arange(B_W) % PAIR_W
    lo = jnp.asarray((lane < HEAD_DIM).astype(np.float32).reshape(1, B_W), BF16)
    hi = jnp.asarray((lane >= HEAD_DIM).astype(np.float32).reshape(1, B_W), BF16)
    ind_a, ind_b = (jnp.asarray(_head_indicator(c), BF16) for c in (A_COLS, B_COLS))
    tok = lambda w: pl.BlockSpec((1, tile, w), lambda i, j: (i, j, 0))
    seq = lambda w: pl.BlockSpec((1, s, w), lambda i, j: (i, 0, 0))
    const = lambda shp: pl.BlockSpec(shp, lambda i, j: (0,) * len(shp))
    return pl.pallas_call(
        functools.partial(_attn_prompt_kernel, tile=tile),
        out_shape=(jax.ShapeDtypeStruct((b, s, QA_W), BF16), jax.ShapeDtypeStruct((b, s, B_W), BF16)),
        grid=(b, s // tile),
        in_specs=[tok(QA_W), seq(KVA_W), seq(KVA_W), tok(B_W), seq(B_W), seq(B_W),
                  const(bias2_a.shape), const(sinks2.shape), const(bias2_b.shape), const(lo.shape), const(hi.shape),
                  const(ind_a.shape), const(ind_b.shape)],
        out_specs=(tok(QA_W), tok(B_W)),
        scratch_shapes=[pltpu.VMEM((4, s, PAIR_W), BF16), pltpu.VMEM((4, s, PAIR_W), BF16),
                        pltpu.VMEM((2, s, B_W), BF16), pltpu.VMEM((2, s, B_W), BF16)],
        compiler_params=_cparams("arbitrary", "arbitrary"),
        name="attn_prompt",
    )(qa, ka, va, qb, kb, vb, bias2_a, sinks2, bias2_b, lo, hi, ind_a, ind_b)


def _band_heads(cache_ref, new_ref, heads):
    hist = cache_ref.shape[1] // heads
    new = new_ref[0]
    return [jnp.concatenate([cache_ref[0, pl.ds(h, hist, stride=heads), :].astype(BF16),
                             new[:, h * HEAD_DIM:(h + 1) * HEAD_DIM]], axis=0) for h in range(heads)]


def _attn_sample_kernel(qa_ref, ka_ref, va_ref, qb_ref, kb_ref, vb_ref, cak_ref, cav_ref, cbk_ref, cbv_ref,
                        ba_ref, sk_ref, bb_ref, oa_ref, ob_ref):
    oa_ref[0] = _attend_a(qa_ref[0], _band_heads(cak_ref, ka_ref, A_KV_HEADS), _band_heads(cav_ref, va_ref, A_KV_HEADS),
                          ba_ref, sk_ref)
    ob_ref[0] = _attend_b(qb_ref[0], _band_heads(cbk_ref, kb_ref, B_HEADS), _band_heads(cbv_ref, vb_ref, B_HEADS), bb_ref)


def _attn_sample(qa, ka, va, qb, kb, vb, cak, cav, cbk, cbv, bias_a, sinks, bias_b):
    b, t, _ = qa.shape
    per_b = lambda arr: pl.BlockSpec((1,) + arr.shape[1:], lambda i: (i, 0, 0))
    const = lambda shp: pl.BlockSpec(shp, lambda i: (0,) * len(shp))
    operands = (qa, ka, va, qb, kb, vb, cak, cav, cbk, cbv)
    return pl.pallas_call(
        _attn_sample_kernel,
        out_shape=(jax.ShapeDtypeStruct((b, t, QA_W), BF16), jax.ShapeDtypeStruct((b, t, B_W), BF16)),
        grid=(b,),
        in_specs=[per_b(a) for a in operands] + [const(bias_a.shape), const(sinks.shape), const(bias_b.shape)],
        out_specs=(per_b(qa), per_b(qb)),
        compiler_params=_cparams("parallel"),
        name="attn_sample",
    )(*operands, bias_a, sinks, bias_b)


def _route(lt):
    t = lt.shape[1]
    el = lt[0:N_EXPERTS]
    gl = lt[N_EXPERTS:N_EXPERTS + N_GROUPS]
    gmax = jnp.max(gl, axis=0, keepdims=True)
    gi = lax.broadcasted_iota(jnp.int32, (N_GROUPS, t), 0)
    gidx = jnp.min(jnp.where(gl == gmax, gi, N_GROUPS), axis=0, keepdims=True)
    g_w = 1.0 / jnp.sum(jnp.exp(gl - gmax), axis=0, keepdims=True)
    e_sel = el[(N_GROUPS - 1) * EXPERTS_PER_GROUP:]
    for g in range(N_GROUPS - 2, -1, -1):
        e_sel = jnp.where(gidx == g, el[g * EXPERTS_PER_GROUP:(g + 1) * EXPERTS_PER_GROUP], e_sel)
    ei = lax.broadcasted_iota(jnp.int32, (EXPERTS_PER_GROUP, t), 0)
    m1 = jnp.max(e_sel, axis=0, keepdims=True)
    i1 = jnp.min(jnp.where(e_sel == m1, ei, EXPERTS_PER_GROUP), axis=0, keepdims=True)
    rest = jnp.where(ei == i1, -jnp.inf, e_sel)
    m2 = jnp.max(rest, axis=0, keepdims=True)
    i2 = jnp.min(jnp.where(rest == m2, ei, EXPERTS_PER_GROUP), axis=0, keepdims=True)
    ex = jnp.exp(m2 - m1)
    den = 1.0 + ex
    w1 = g_w * (1.0 / den)
    w2 = g_w * (ex / den)
    lo = jnp.minimum(i1, i2)
    hi = jnp.maximum(i1, i2)
    first_is_lo = i1 < i2
    w_lo = jnp.where(first_is_lo, w1, w2)
    w_hi = jnp.where(first_is_lo, w2, w1)
    pair = ((lo * (2 * EXPERTS_PER_GROUP - 1 - lo)) >> 1) + (hi - lo - 1)
    return gidx * PAIRS_PER_GROUP + pair, w_lo, w_hi


def _route_logits(wr_ref, br_ref, h2):
    return lax.dot_general(wr_ref[...], h2, (((1,), (1,)), ((), ())), preferred_element_type=F32) + br_ref[...]


def _weight_tile(w_lo, w_hi):
    t = w_lo.shape[1]
    return jnp.transpose(jnp.concatenate([w_lo, w_hi, jnp.zeros((LANES - 2, t), F32)], axis=0))


def _rank_in_class(cls, run_ref, tri_ref):
    t = cls.shape[1]
    onehot = lax.broadcasted_iota(jnp.int32, (CLASS_ROWS, t), 0) == cls
    ones = jnp.where(onehot, 1.0, 0.0)
    before = jnp.dot(ones.astype(BF16), tri_ref[...], preferred_element_type=F32)
    run = run_ref[...]
    rank = jnp.sum(jnp.where(onehot, before + run, 0.0), axis=0, keepdims=True)
    run_ref[...] = run + jnp.sum(ones, axis=1, keepdims=True)
    return rank.astype(jnp.int32)


def _post_kernel(x_ref, oa_ref, ob_ref, mod_ref, g1_ref, g1p_ref, g2_ref,
                 wg_ref, bg_ref, wpa_ref, wpb_ref, wo_ref, wr_ref, br_ref, tri_ref,
                 x1_ref, h2_ref, cls_ref, rank_ref, cnt_ref, run_ref):
    @pl.when((pl.program_id(0) == 0) & (pl.program_id(1) == 0))
    def _():
        run_ref[...] = jnp.zeros_like(run_ref)

    nb, rows, _ = x_ref.shape
    stack = lambda parts: parts[0] if nb == 1 else jnp.concatenate(parts, axis=0)
    xs = [x_ref[g] for g in range(nb)]
    h = stack([_norm_mod(xs[g], g1_ref[...], mod_ref[g, 1:2, :], mod_ref[g, 0:1, :]) for g in range(nb)]).astype(BF16)
    gates = jax.nn.sigmoid(jnp.dot(h, wg_ref[...], preferred_element_type=F32) + bg_ref[...])
    pa = jnp.dot(stack([oa_ref[g] for g in range(nb)]), wpa_ref[...], preferred_element_type=F32)
    pb = jnp.dot(stack([ob_ref[g] for g in range(nb)]), wpb_ref[...], preferred_element_type=F32)
    mixed = gates[:, :D_MODEL] * pa + gates[:, D_MODEL:] * pb
    y = jnp.dot(mixed.astype(BF16), wo_ref[...], preferred_element_type=F32)
    h2_parts = []
    for g in range(nb):
        x1 = xs[g] + mod_ref[g, 2:3, :] * _rms(y[g * rows:(g + 1) * rows], g1p_ref[...])
        x1_ref[g] = x1
        h2_parts.append(_norm_mod(x1, g2_ref[...], mod_ref[g, 4:5, :], mod_ref[g, 3:4, :]))
    h2 = stack(h2_parts).astype(BF16)
    cls, w_lo, w_hi = _route(_route_logits(wr_ref, br_ref, h2))
    cls_ref[0] = cls
    rank_ref[0] = _rank_in_class(cls, run_ref, tri_ref)
    cnt_ref[...] = run_ref[...]
    t = nb * rows
    h2f = h2.astype(F32)
    for c in range(D_MODEL // LANES):
        h2_ref[pl.ds(c, t, stride=IN_SLAB), :] = h2f[:, c * LANES:(c + 1) * LANES]
    h2_ref[pl.ds(D_MODEL // LANES, t, stride=IN_SLAB), :] = _weight_tile(w_lo, w_hi)


def _post(x, oa, ob, mods, g_pre_mix, g_post_mix, g_pre_ffn, wg, bg, wpa, wpb, wo, wr, br, tile):
    b, s, d = x.shape
    rows = min(s, tile)
    nb = math.gcd(tile // rows, b)
    tile = nb * rows
    assert s % rows == 0
    nt = s // rows
    n_steps = (b // nb) * nt
    tri = (jnp.arange(tile)[:, None] < jnp.arange(tile)[None, :]).astype(BF16)
    tok = lambda w: pl.BlockSpec((nb, rows, w), lambda i, j: (i, j, 0))
    const = lambda shp: pl.BlockSpec(shp, lambda i, j: (0,) * len(shp))
    per_tile = pl.BlockSpec((1, 1, tile), lambda i, j: (i * nt + j, 0, 0))
    return pl.pallas_call(
        _post_kernel,
        out_shape=(
            jax.ShapeDtypeStruct((b, s, d), F32),
            jax.ShapeDtypeStruct((b * s * IN_SLAB, LANES), F32),
            jax.ShapeDtypeStruct((n_steps, 1, tile), jnp.int32),
            jax.ShapeDtypeStruct((n_steps, 1, tile), jnp.int32),
            jax.ShapeDtypeStruct((CLASS_ROWS, 1), F32),
        ),
        grid=(b // nb, nt),
        in_specs=[
            tok(d), tok(QA_W), tok(B_W),
            pl.BlockSpec((nb, 6, d), lambda i, j: (i, 0, 0)),
            const((1, d)), const((1, d)), const((1, d)),
            const(wg.shape), const(bg.shape), const(wpa.shape), const(wpb.shape), const(wo.shape),
            const(wr.shape), const(br.shape), const(tri.shape),
        ],
        out_specs=(tok(d), pl.BlockSpec((tile * IN_SLAB, LANES), lambda i, j: (i * nt + j, 0)),
                   per_tile, per_tile, const((CLASS_ROWS, 1))),
        scratch_shapes=[pltpu.VMEM((CLASS_ROWS, 1), F32)],
        compiler_params=_cparams("arbitrary", "arbitrary"),
        name="post",
    )(x, oa, ob, mods, g_pre_mix, g_post_mix, g_pre_ffn, wg, bg, wpa, wpb, wo, wr, br, tri)


def _invert_kernel(upos_ref, row_ref, *, n, scale):
    def body(i, carry):
        for u in range(ISSUE_UNROLL):
            t = i * ISSUE_UNROLL + u
            row_ref[upos_ref[t]] = t * scale
        return carry
    lax.fori_loop(0, n // ISSUE_UNROLL, body, 0)


def _invert(upos, scale):
    n = upos.shape[0]
    return pl.pallas_call(
        functools.partial(_invert_kernel, n=n, scale=scale),
        out_shape=jax.ShapeDtypeStruct((n,), jnp.int32),
        in_specs=[pl.BlockSpec(memory_space=pltpu.SMEM)],
        out_specs=pl.BlockSpec(memory_space=pltpu.SMEM),
        name="invert",
    )(upos)


def _expert(xb, wgu_ref, wd_ref):
    gu = jnp.dot(xb, wgu_ref[0], preferred_element_type=F32)
    gate = gu[:, :D_EXPERT]
    he = (gate * jax.nn.sigmoid(gate)) * gu[:, D_EXPERT:]
    return jnp.dot(he.astype(BF16), wd_ref[0], preferred_element_type=F32)


def _moe_kernel(tok_ref, base_ref, ea_ref, eb_ref, nu_ref, h_ref, wgu_a_ref, wd_a_ref, wgu_b_ref, wd_b_ref,
                ys_ref, xbuf, sems, *, tile, n_tok):
    t = pl.program_id(0)
    nu = nu_ref[0]
    slot = t % 2

    def slab_copy(src_row0, dst_slot, r):
        return pltpu.make_async_copy(h_ref.at[pl.ds(src_row0, IN_SLAB)],
                                     xbuf.at[dst_slot, pl.ds(r * IN_SLAB, IN_SLAB)], sems.at[dst_slot])

    def fetch(step, dst_slot):
        base = base_ref[step]
        for r in range(tile):
            slab_copy(tok_ref[jnp.minimum(base + r, n_tok - 1)], dst_slot, r).start(priority=r % 2)

    def wait(dst_slot):
        for r in range(tile):
            slab_copy(0, dst_slot, 0).wait()

    @pl.when(t == 0)
    def _():
        fetch(0, 0)

    @pl.when(t < nu)
    def _():
        wait(slot)
        fetch(jnp.minimum(t + 1, nu - 1), 1 - slot)
        lane_tile = lambda c: xbuf[slot, pl.ds(c, tile, stride=IN_SLAB), :]
        xb = jnp.concatenate([lane_tile(c) for c in range(D_MODEL // LANES)], axis=1).astype(BF16)
        w = lane_tile(D_MODEL // LANES)
        y = w[:, 0:1] * _expert(xb, wgu_a_ref, wd_a_ref) + w[:, 1:2] * _expert(xb, wgu_b_ref, wd_b_ref)
        for c in range(OUT_SLAB):
            ys_ref[pl.ds(c, tile, stride=OUT_SLAB), :] = y[:, c * LANES:(c + 1) * LANES]

    @pl.when(t == nu - 1)
    def _():
        wait(1 - slot)

    @pl.when(t >= nu)
    def _():
        ys_ref[...] = jnp.zeros_like(ys_ref)


def _moe_grouped(h_slabs, sorted_tok, tile_base, tile_ea, tile_eb, n_used, wgu, wd, tile, n_tiles):
    d = D_MODEL
    n_tok = sorted_tok.shape[0]
    expert = lambda which, shp: pl.BlockSpec(shp, lambda t, tok, base, ea, eb, nu: ((ea, eb)[which][t], 0, 0))
    return pl.pallas_call(
        functools.partial(_moe_kernel, tile=tile, n_tok=n_tok),
        out_shape=jax.ShapeDtypeStruct((n_tiles * tile * OUT_SLAB, LANES), F32),
        grid_spec=pltpu.PrefetchScalarGridSpec(
            num_scalar_prefetch=5,
            grid=(n_tiles,),
            in_specs=[
                pl.BlockSpec(memory_space=pl.ANY),
                expert(0, (1, d, 2 * D_EXPERT)), expert(0, (1, D_EXPERT, d)),
                expert(1, (1, d, 2 * D_EXPERT)), expert(1, (1, D_EXPERT, d)),
            ],
            out_specs=pl.BlockSpec((tile * OUT_SLAB, LANES), lambda t, *_: (t, 0)),
            scratch_shapes=[pltpu.VMEM((2, tile * IN_SLAB, LANES), F32), pltpu.SemaphoreType.DMA((2,))],
        ),
        compiler_params=_cparams("arbitrary"),
        name="moe_grouped",
    )(sorted_tok, tile_base, tile_ea, tile_eb, n_used, h_slabs, wgu, wd, wgu, wd)


_PAIR_LO, _PAIR_HI = np.triu_indices(EXPERTS_PER_GROUP, k=1)
_CLASS_LO = np.concatenate([g * EXPERTS_PER_GROUP + _PAIR_LO for g in range(N_GROUPS)]).astype(np.int32)
_CLASS_HI = np.concatenate([g * EXPERTS_PER_GROUP + _PAIR_HI for g in range(N_GROUPS)]).astype(np.int32)


def _moe_plan(counts, cls, rank, n, tile):
    p_max = -(-(n + N_CLASSES * (tile - 1)) // tile) * tile
    nt = p_max // tile
    counts = counts[:N_CLASSES].astype(jnp.int32)
    padded = ((counts + tile - 1) // tile) * tile
    pad_end = jnp.cumsum(padded)
    gstart = pad_end - padded
    ustart = jnp.cumsum(counts) - counts
    onehot = cls[:, None] == jnp.arange(N_CLASSES, dtype=jnp.int32)[None, :]
    pos = rank + jnp.sum(jnp.where(onehot, gstart[None, :], 0), axis=1)
    upos = rank + jnp.sum(jnp.where(onehot, ustart[None, :], 0), axis=1)
    n_used = (pad_end[-1] // tile).astype(jnp.int32).reshape(1)
    tiles = jnp.arange(nt, dtype=jnp.int32)
    tile_cls = jnp.minimum(jnp.sum((pad_end[None, :] <= (tiles * tile)[:, None]).astype(jnp.int32), axis=1), N_CLASSES - 1)
    tile_ea = jnp.asarray(_CLASS_LO)[tile_cls]
    tile_eb = jnp.asarray(_CLASS_HI)[tile_cls]
    tile_base = jnp.clip(ustart[tile_cls] + tiles * tile - gstart[tile_cls], 0, n - 1)
    return nt, pos, upos, tile_base, tile_ea, tile_eb, n_used


def _final_kernel(pos_ref, x1_ref, ys_ref, mod_ref, g_ref, o_ref, ybuf, sems, *, tile, n_steps):
    s = pl.program_id(0)
    slot = s % 2

    def slab_copy(src_row0, dst_slot, dst_row):
        return pltpu.make_async_copy(ys_ref.at[pl.ds(src_row0, OUT_SLAB)],
                                     ybuf.at[dst_slot, pl.ds(dst_row * OUT_SLAB, OUT_SLAB)], sems.at[dst_slot])

    def fetch(step, dst_slot):
        for r in range(tile):
            slab_copy(pos_ref[step * tile + r], dst_slot, r).start(priority=r % 2)

    def wait(dst_slot):
        for r in range(tile):
            slab_copy(0, dst_slot, 0).wait()

    @pl.when(s == 0)
    def _():
        fetch(0, 0)

    wait(slot)
    fetch(jnp.minimum(s + 1, n_steps - 1), 1 - slot)
    y = jnp.concatenate([ybuf[slot, pl.ds(c, tile, stride=OUT_SLAB), :] for c in range(OUT_SLAB)], axis=1)
    o_ref[...] = x1_ref[...] + mod_ref[0, 5:6, :] * _rms(y, g_ref[...])

    @pl.when(s == n_steps - 1)
    def _():
        wait(1 - slot)


def _final(x1, ys, pos, mods, g_post_ffn, tile):
    b, s, d = x1.shape
    n = b * s
    per_b = s // tile
    n_steps = n // tile
    tok = pl.BlockSpec((tile, d), lambda i, *_: (i, 0))
    out = pl.pallas_call(
        functools.partial(_final_kernel, tile=tile, n_steps=n_steps),
        out_shape=jax.ShapeDtypeStruct((n, d), F32),
        grid_spec=pltpu.PrefetchScalarGridSpec(
            num_scalar_prefetch=1,
            grid=(n_steps,),
            in_specs=[
                tok,
                pl.BlockSpec(memory_space=pl.ANY),
                pl.BlockSpec((1, 6, d), lambda i, *_: (i // per_b, 0, 0)),
                pl.BlockSpec((1, d), lambda i, *_: (0, 0)),
            ],
            out_specs=tok,
            scratch_shapes=[pltpu.VMEM((2, tile * OUT_SLAB, LANES), F32), pltpu.SemaphoreType.DMA((2,))],
        ),
        compiler_params=_cparams("arbitrary"),
        name="final",
    )(pos, x1.reshape(n, d), ys, mods, g_post_ffn)
    return out.reshape(b, s, d)


def _t5_bucket(rel):
    half = T5_BUCKETS // 2
    exact = half // 2
    ret = jnp.where(rel > 0, half, 0)
    n = jnp.abs(rel)
    nf = jnp.maximum(n, 1).astype(F32)
    large = exact + (jnp.log(nf / exact) / math.log(T5_MAX_DIST / exact) * (half - exact)).astype(jnp.int32)
    large = jnp.minimum(large, half - 1)
    return ret + jnp.where(n < exact, n, large)


def _toeplitz(u, n_rows, n_cols):
    lead, p = u.shape[:-1], n_rows + n_cols
    w = jnp.concatenate([u, jnp.zeros(lead + (1,), u.dtype)], axis=-1)
    periodic = jnp.broadcast_to(w[..., None, :], lead + (n_rows + 1, p)).reshape(lead + ((n_rows + 1) * p,))
    shifted = periodic[..., :n_rows * (p + 1)].reshape(lead + (n_rows, p + 1))[..., :n_cols]
    return shifted[..., ::-1, :]


def _bias_tables(t5_table, rel_table):
    ja = jnp.arange(CHUNK - 1 + A_BAND)
    ua = t5_table[_t5_bucket(ja - (CHUNK - 1) - A_WINDOW)].T.astype(F32)
    jb = jnp.arange(CHUNK - 1 + B_BAND)
    ub = rel_table[:, jnp.clip((CHUNK - 1) - jb + B_REACH, -REL_CLIP, REL_CLIP) + REL_CLIP].astype(F32)
    return _toeplitz(ua, CHUNK, A_BAND), _toeplitz(ub, CHUNK, B_BAND)


def _two_chunk(bias):
    masked = jnp.full(bias.shape[:-1] + (CHUNK,), MASKED, F32)
    return jnp.concatenate([jnp.concatenate([bias, masked], axis=-1), jnp.concatenate([masked, bias], axis=-1)], axis=-2)


def _pair_tables(bias_a, bias_b, sinks):
    a2, b2 = _two_chunk(bias_a), _two_chunk(bias_b)
    stack2 = lambda x, h0, h1: jnp.concatenate([x[h0], x[h1]], axis=0)
    bias2_a = jnp.stack([jnp.concatenate([stack2(a2, 4 * kv, 4 * kv + 2), stack2(a2, 4 * kv + 1, 4 * kv + 3)], axis=1)
                         for kv in range(A_KV_HEADS)])
    bias2_b = jnp.concatenate([b2[0::2], b2[1::2]], axis=-1)
    sk = jnp.broadcast_to(sinks.astype(F32).reshape(A_Q_HEADS, 1, 1), (A_Q_HEADS, ATT_BLOCK, 1))
    sinks2 = jnp.stack([jnp.stack([stack2(sk, 4 * kv, 4 * kv + 2), stack2(sk, 4 * kv + 1, 4 * kv + 3)])
                        for kv in range(A_KV_HEADS)])
    return bias2_a, sinks2, bias2_b


def _moe_and_final(x1, h2e, cls, rank, counts, mods, g_post_ffn, wgu, wd, seq_tile, moe_tile):
    b, s, d = x1.shape
    n = b * s
    n_tiles, pos, upos, tile_base, tile_ea, tile_eb, n_used = _moe_plan(
        counts.reshape(CLASS_ROWS), cls.reshape(n), rank.reshape(n), n, moe_tile)
    ys = _moe_grouped(h2e, _invert(upos, IN_SLAB), tile_base, tile_ea, tile_eb, n_used, wgu, wd, moe_tile, n_tiles)
    return _final(x1, ys, pos * OUT_SLAB, mods, g_post_ffn, seq_tile)


def kernel(x_prompt, x_sample, c_prompt, c_sample, cache_a_k, cache_a_v, cache_b_k, cache_b_v, w_ada, b_ada, g_pre_mix, g_post_mix, g_pre_ffn, g_post_ffn, w_in, a_sinks, t5_table, b_rel_table, w_proj_a, w_proj_b, w_gate, b_gate, w_o, w_route_g, b_route_g, w_route_e, b_route_e, w_e_gate, w_e_up, w_e_down):
    depth = w_in.shape[0]
    assert depth == 1
    l = 0
    bp, sp, d = x_prompt.shape
    bs, ts, _ = x_sample.shape

    mods = _ada(jnp.concatenate([c_prompt, c_sample], axis=0), w_ada[l], b_ada[l]).reshape(bp + bs, 6, d)
    mods_p, mods_s = mods[:bp], mods[bp:]

    w_in_bf = w_in[l].astype(BF16)
    wg, wpa, wpb, wo = w_gate[l].astype(BF16), w_proj_a[l].astype(BF16), w_proj_b[l].astype(BF16), w_o[l].astype(BF16)
    bg = b_gate[l].reshape(1, 2 * d)
    pad_rows = ROUTE_ROWS - N_EXPERTS - N_GROUPS
    wr = jnp.concatenate([w_route_e[l].T, w_route_g[l].T, jnp.zeros((pad_rows, d), F32)], axis=0).astype(BF16)
    br = jnp.concatenate([b_route_e[l], b_route_g[l], jnp.zeros((pad_rows,), F32)]).reshape(ROUTE_ROWS, 1)
    wgu = jnp.concatenate([w_e_gate[l], w_e_up[l]], axis=-1).astype(BF16)
    wd = w_e_down[l].astype(BF16)
    g1, g1p, g2, g2p = (g[l].reshape(1, d) for g in (g_pre_mix, g_post_mix, g_pre_ffn, g_post_ffn))

    bias_a, bias_b = _bias_tables(t5_table, b_rel_table[l])
    bias2_a, sinks2, bias2_b = _pair_tables(bias_a, bias_b, a_sinks[l])
    bias_a = bias_a.reshape(A_KV_HEADS, A_GROUP * CHUNK, A_BAND)
    sinks = jnp.broadcast_to(a_sinks[l].reshape(A_KV_HEADS, A_GROUP, 1, 1), (A_KV_HEADS, A_GROUP, CHUNK, 1))
    sinks = sinks.reshape(A_KV_HEADS, A_GROUP * CHUNK, 1).astype(F32)

    qa, ka, va, qb, kb, vb, sak_p, sav_p, sbk_p, sbv_p = _pre_prompt(x_prompt, mods_p, g1, w_in_bf)
    oa, ob = _attn_prompt(qa, ka, va, qb, kb, vb, bias2_a, sinks2, bias2_b)
    x1, h2e, cls, rank, counts = _post(x_prompt, oa, ob, mods_p, g1, g1p, g2, wg, bg, wpa, wpb, wo, wr, br, SEQ_TILE)
    y_prompt = _moe_and_final(x1, h2e, cls, rank, counts, mods_p, g2p, wgu, wd, SEQ_TILE, 256)

    la, lb = cache_a_k.shape[2], cache_b_k.shape[2]
    cak = cache_a_k[l].reshape(bs, la * A_KV_HEADS, HEAD_DIM)
    cav = cache_a_v[l].reshape(bs, la * A_KV_HEADS, HEAD_DIM)
    cbk = cache_b_k[l].reshape(bs, lb * B_HEADS, HEAD_DIM)
    cbv = cache_b_v[l].reshape(bs, lb * B_HEADS, HEAD_DIM)
    qa, ka, va, qb, kb, vb, sak_s, sav_s, sbk_s, sbv_s = _pre_sample(x_sample, mods_s, g1, w_in_bf, cak, cav, cbk, cbv)
    oa, ob = _attn_sample(qa, ka, va, qb, kb, vb, cak, cav, cbk, cbv, bias_a, sinks, bias_b)
    x1, h2e, cls, rank, counts = _post(x_sample, oa, ob, mods_s, g1, g1p, g2, wg, bg, wpa, wpb, wo, wr, br, SEQ_TILE)
    y_sample = _moe_and_final(x1, h2e, cls, rank, counts, mods_s, g2p, wgu, wd, ts, 64)

    a_state = lambda v, b, r: v.reshape(1, b, r, A_KV_HEADS, HEAD_DIM)
    b_state = lambda v, b, r: v.reshape(1, b, r, B_HEADS, HEAD_DIM)
    return (y_prompt, y_sample,
            a_state(sak_p, bp, A_WINDOW), a_state(sav_p, bp, A_WINDOW),
            b_state(sbk_p, bp, B_REACH), b_state(sbv_p, bp, B_REACH),
            a_state(sak_s, bs, la), a_state(sav_s, bs, la),
            b_state(sbk_s, bs, lb), b_state(sbv_s, bs, lb))
```

```python
import functools
import math

import numpy as np
import jax
import jax.numpy as jnp
from jax import lax
from jax.experimental import pallas as pl
from jax.experimental.pallas import tpu as pltpu

D_MODEL = 1024
CHUNK = 64
HEAD_DIM = 64
A_Q_HEADS = 8
A_KV_HEADS = 2
A_GROUP = A_Q_HEADS // A_KV_HEADS
A_WINDOW = 128
A_BACK = A_WINDOW // CHUNK
B_HEADS = 8
B_BACK = 8
B_REACH = B_BACK * CHUNK
REL_CLIP = 256
T5_BUCKETS = 32
T5_MAX_DIST = 128
N_GROUPS = 4
EXPERTS_PER_GROUP = 8
N_EXPERTS = N_GROUPS * EXPERTS_PER_GROUP
D_EXPERT = D_MODEL // 4
EPS = 1e-6

QA_W = A_Q_HEADS * HEAD_DIM
KVA_W = A_KV_HEADS * HEAD_DIM
B_W = B_HEADS * HEAD_DIM
IN_W = QA_W + 2 * KVA_W + 3 * B_W
A_BAND = A_WINDOW + CHUNK
B_BAND = B_REACH + CHUNK

PAIRS_PER_GROUP = EXPERTS_PER_GROUP * (EXPERTS_PER_GROUP - 1) // 2
N_CLASSES = N_GROUPS * PAIRS_PER_GROUP
ROUTE_ROWS = 40
CLASS_ROWS = 128
LANES = 128
IN_SLAB = D_MODEL // LANES + 1
OUT_SLAB = D_MODEL // LANES

F32 = jnp.float32
BF16 = jnp.bfloat16

VMEM_LIMIT_BYTES = 56 * 1024 * 1024

SEQ_TILE = 512
ISSUE_UNROLL = 32


def _cparams(*sem):
    return pltpu.CompilerParams(dimension_semantics=sem, vmem_limit_bytes=VMEM_LIMIT_BYTES)


def _norm_mod(x, g, scale, shift):
    y = x * lax.rsqrt(jnp.mean(x * x, axis=-1, keepdims=True) + EPS)
    return (y * g) * (1.0 + scale) + shift


def _rms(x, g):
    return (x * lax.rsqrt(jnp.mean(x * x, axis=-1, keepdims=True) + EPS)) * g


def _ada_kernel(c_ref, w_ref, b_ref, o_ref):
    c = c_ref[...]
    s = (c * jax.nn.sigmoid(c)).astype(BF16)
    o_ref[...] = jnp.dot(s, w_ref[...].astype(BF16), preferred_element_type=F32) + b_ref[...]


def _ada(c, w_ada, b_ada):
    n, d = c.shape
    wn = w_ada.shape[1]
    tn = 512
    return pl.pallas_call(
        _ada_kernel,
        out_shape=jax.ShapeDtypeStruct((n, wn), F32),
        grid=(wn // tn,),
        in_specs=[
            pl.BlockSpec((n, d), lambda j: (0, 0)),
            pl.BlockSpec((d, tn), lambda j: (0, j)),
            pl.BlockSpec((1, tn), lambda j: (0, j)),
        ],
        out_specs=pl.BlockSpec((n, tn), lambda j: (0, j)),
        compiler_params=_cparams("arbitrary"),
        name="ada",
    )(c, w_ada, b_ada.reshape(1, wn))


_COL_QA = (0, QA_W)
_COL_KA = (QA_W, QA_W + KVA_W)
_COL_VA = (QA_W + KVA_W, QA_W + 2 * KVA_W)
_COL_QB = (QA_W + 2 * KVA_W, QA_W + 2 * KVA_W + B_W)
_COL_KB = (_COL_QB[1], _COL_QB[1] + B_W)
_COL_VB = (_COL_KB[1], _COL_KB[1] + B_W)
Q_SCALE = HEAD_DIM ** -0.5


def _project(x_ref, mod_ref, g_ref, w_ref):
    h = _norm_mod(x_ref[0], g_ref[...], mod_ref[0, 1:2, :], mod_ref[0, 0:1, :])
    return jnp.dot(h.astype(BF16), w_ref[...], preferred_element_type=F32)


def _cols(p, c):
    return p[:, c[0]:c[1]]


def _pre_prompt_kernel(x_ref, mod_ref, g_ref, w_ref,
                       qa_ref, ka_ref, va_ref, qb_ref, kb_ref, vb_ref,
                       sak_ref, sav_ref, sbk_ref, sbv_ref, *, n_tiles, tile):
    p = _project(x_ref, mod_ref, g_ref, w_ref)
    qa_ref[0] = (_cols(p, _COL_QA) * Q_SCALE).astype(BF16)
    ka_ref[0] = _cols(p, _COL_KA).astype(BF16)
    va_ref[0] = _cols(p, _COL_VA).astype(BF16)
    qb_ref[0] = (_cols(p, _COL_QB) * Q_SCALE).astype(BF16)
    kb_ref[0] = _cols(p, _COL_KB).astype(BF16)
    vb_ref[0] = _cols(p, _COL_VB).astype(BF16)

    @pl.when(pl.program_id(1) == n_tiles - 1)
    def _():
        sak_ref[0] = _cols(p, _COL_KA)[tile - A_WINDOW:, :]
        sav_ref[0] = _cols(p, _COL_VA)[tile - A_WINDOW:, :]
        sbk_ref[0] = _cols(p, _COL_KB)[tile - B_REACH:, :]
        sbv_ref[0] = _cols(p, _COL_VB)[tile - B_REACH:, :]


def _pre_prompt(x, mods, g_pre, w_in_bf):
    b, s, d = x.shape
    tile = SEQ_TILE
    assert s % tile == 0 and tile >= B_REACH and s >= B_REACH
    nt = s // tile
    tok = lambda w: pl.BlockSpec((1, tile, w), lambda i, j: (i, j, 0))
    state = lambda r, w: pl.BlockSpec((1, r, w), lambda i, j: (i, 0, 0))
    return pl.pallas_call(
        functools.partial(_pre_prompt_kernel, n_tiles=nt, tile=tile),
        out_shape=(
            jax.ShapeDtypeStruct((b, s, QA_W), BF16),
            jax.ShapeDtypeStruct((b, s, KVA_W), BF16),
            jax.ShapeDtypeStruct((b, s, KVA_W), BF16),
            jax.ShapeDtypeStruct((b, s, B_W), BF16),
            jax.ShapeDtypeStruct((b, s, B_W), BF16),
            jax.ShapeDtypeStruct((b, s, B_W), BF16),
            jax.ShapeDtypeStruct((b, A_WINDOW, KVA_W), F32),
            jax.ShapeDtypeStruct((b, A_WINDOW, KVA_W), F32),
            jax.ShapeDtypeStruct((b, B_REACH, B_W), F32),
            jax.ShapeDtypeStruct((b, B_REACH, B_W), F32),
        ),
        grid=(b, nt),
        in_specs=[
            tok(d),
            pl.BlockSpec((1, 6, d), lambda i, j: (i, 0, 0)),
            pl.BlockSpec((1, d), lambda i, j: (0, 0)),
            pl.BlockSpec((d, IN_W), lambda i, j: (0, 0)),
        ],
        out_specs=(
            tok(QA_W), tok(KVA_W), tok(KVA_W), tok(B_W), tok(B_W), tok(B_W),
            state(A_WINDOW, KVA_W), state(A_WINDOW, KVA_W), state(B_REACH, B_W), state(B_REACH, B_W),
        ),
        compiler_params=_cparams("parallel", "arbitrary"),
        name="pre_prompt",
    )(x, mods, g_pre, w_in_bf)


def _pre_sample_kernel(x_ref, mod_ref, g_ref, w_ref, cak_ref, cav_ref, cbk_ref, cbv_ref,
                       qa_ref, ka_ref, va_ref, qb_ref, kb_ref, vb_ref,
                       sak_ref, sav_ref, sbk_ref, sbv_ref, *, t, la, lb):
    p = _project(x_ref, mod_ref, g_ref, w_ref)
    qa_ref[0] = (_cols(p, _COL_QA) * Q_SCALE).astype(BF16)
    qb_ref[0] = (_cols(p, _COL_QB) * Q_SCALE).astype(BF16)
    for cache_ref, band_ref, state_ref, col, hist in (
            (cak_ref, ka_ref, sak_ref, _COL_KA, la), (cav_ref, va_ref, sav_ref, _COL_VA, la),
            (cbk_ref, kb_ref, sbk_ref, _COL_KB, lb), (cbv_ref, vb_ref, sbv_ref, _COL_VB, lb)):
        new = _cols(p, col)
        cache = cache_ref[0]
        band_ref[0, 0:hist, :] = cache.astype(BF16)
        band_ref[0, hist:hist + t, :] = new.astype(BF16)
        state_ref[0, 0:hist - t, :] = cache[t:, :]
        state_ref[0, hist - t:hist, :] = new


def _pre_sample(x, mods, g_pre, w_in_bf, cak, cav, cbk, cbv):
    b, t, d = x.shape
    la, lb = cak.shape[1], cbk.shape[1]
    assert t == CHUNK and la == A_WINDOW and lb == B_REACH
    per_b = lambda r, w: pl.BlockSpec((1, r, w), lambda i: (i, 0, 0))
    return pl.pallas_call(
        functools.partial(_pre_sample_kernel, t=t, la=la, lb=lb),
        out_shape=(
            jax.ShapeDtypeStruct((b, t, QA_W), BF16),
            jax.ShapeDtypeStruct((b, la + t, KVA_W), BF16),
            jax.ShapeDtypeStruct((b, la + t, KVA_W), BF16),
            jax.ShapeDtypeStruct((b, t, B_W), BF16),
            jax.ShapeDtypeStruct((b, lb + t, B_W), BF16),
            jax.ShapeDtypeStruct((b, lb + t, B_W), BF16),
            jax.ShapeDtypeStruct((b, la, KVA_W), F32),
            jax.ShapeDtypeStruct((b, la, KVA_W), F32),
            jax.ShapeDtypeStruct((b, lb, B_W), F32),
            jax.ShapeDtypeStruct((b, lb, B_W), F32),
        ),
        grid=(b,),
        in_specs=[
            per_b(t, d),
            per_b(6, d),
            pl.BlockSpec((1, d), lambda i: (0, 0)),
            pl.BlockSpec((d, IN_W), lambda i: (0, 0)),
            per_b(la, KVA_W), per_b(la, KVA_W), per_b(lb, B_W), per_b(lb, B_W),
        ],
        out_specs=(
            per_b(t, QA_W), per_b(la + t, KVA_W), per_b(la + t, KVA_W),
            per_b(t, B_W), per_b(lb + t, B_W), per_b(lb + t, B_W),
            per_b(la, KVA_W), per_b(la, KVA_W), per_b(lb, B_W), per_b(lb, B_W),
        ),
        compiler_params=_cparams("parallel"),
        name="pre_sample",
    )(x, mods, g_pre, w_in_bf, cak, cav, cbk, cbv)


MASKED = -1e30
PAIR_W = 2 * HEAD_DIM
ATT_BLOCK = 2 * CHUNK
A_COLS = A_WINDOW + ATT_BLOCK
B_COLS = B_REACH + ATT_BLOCK


def _pair_attend(q2, k2, v2, ind, bias, sinks):
    c = k2.shape[0] // 2
    s = lax.dot_general(q2, k2, (((1,), (1,)), ((), ())), preferred_element_type=F32) + bias
    es, mxs = [], []
    for half in range(2):
        sh = s[:, half * c:(half + 1) * c]
        mx = jnp.max(sh, axis=-1, keepdims=True)
        if sinks is not None:
            mx = jnp.maximum(mx, sinks[half])
        es.append(jnp.exp(sh - mx).astype(BF16))
        mxs.append(mx)
    o = jnp.dot(jnp.concatenate(es, axis=1), jnp.concatenate([v2, ind], axis=1), preferred_element_type=F32)
    dens = [o[:, PAIR_W + half:PAIR_W + half + 1] for half in range(2)]
    if sinks is not None:
        dens = [dens[half] + jnp.exp(sinks[half] - mxs[half]) for half in range(2)]
    lane = lax.broadcasted_iota(jnp.int32, (o.shape[0], PAIR_W), 1)
    return o[:, :PAIR_W] / jnp.where(lane < HEAD_DIM, dens[0], dens[1])


def _last_cols(table, full, n):
    if n == full:
        return table
    return jnp.concatenate([table[:, full - n:full], table[:, 2 * full - n:2 * full]], axis=1)


def _attn_prompt_kernel(qa_ref, ka_ref, va_ref, qb_ref, kb_ref, vb_ref, ba_ref, sk_ref, bb_ref, lo_ref, hi_ref,
                        ia_ref, ib_ref, oa_ref, ob_ref, ak_ref, av_ref, bk_ref, bv_ref, *, tile):
    j = pl.program_id(1)

    @pl.when(j == 0)
    def _():
        lo, hi = lo_ref[:, 0:PAIR_W], hi_ref[:, 0:PAIR_W]
        for src, dst in ((ka_ref, ak_ref), (va_ref, av_ref)):
            x = src[0]
            swapped = jnp.concatenate([x[:, HEAD_DIM:], x[:, :HEAD_DIM]], axis=1)
            dst[0] = x * lo
            dst[1] = swapped * hi
            dst[2] = swapped * lo
            dst[3] = x * hi
        for src, dst in ((kb_ref, bk_ref), (vb_ref, bv_ref)):
            x = src[0]
            dst[0] = x * lo_ref[...]
            dst[1] = x * hi_ref[...]

    def block(g, r0, n_a, n_b):
        rows = pl.ds(r0, ATT_BLOCK)
        band_a = pl.ds(g + ATT_BLOCK - n_a, n_a)
        band_b = pl.ds(g + ATT_BLOCK - n_b, n_b)
        ind_a = jnp.concatenate([ia_ref[0:n_a, :], ia_ref[A_COLS:A_COLS + n_a, :]], axis=0)
        ind_b = jnp.concatenate([ib_ref[0:n_b, :], ib_ref[B_COLS:B_COLS + n_b, :]], axis=0)
        for kv in range(A_KV_HEADS):
            c0, c1 = slice(2 * kv * PAIR_W, (2 * kv + 1) * PAIR_W), slice((2 * kv + 1) * PAIR_W, (2 * kv + 2) * PAIR_W)
            q2 = jnp.concatenate([qa_ref[0, rows, c0], qa_ref[0, rows, c1]], axis=0)
            k2 = jnp.concatenate([ak_ref[2 * kv, band_a, :], ak_ref[2 * kv + 1, band_a, :]], axis=0)
            v2 = jnp.concatenate([av_ref[2 * kv, band_a, :], av_ref[2 * kv + 1, band_a, :]], axis=0)
            o = _pair_attend(q2, k2, v2, ind_a, _last_cols(ba_ref[kv], A_COLS, n_a), (sk_ref[kv, 0], sk_ref[kv, 1]))
            oa_ref[0, rows, c0] = o[0:ATT_BLOCK].astype(BF16)
            oa_ref[0, rows, c1] = o[ATT_BLOCK:].astype(BF16)
        for p in range(B_HEADS // 2):
            cols = slice(p * PAIR_W, (p + 1) * PAIR_W)
            k2 = jnp.concatenate([bk_ref[0, band_b, cols], bk_ref[1, band_b, cols]], axis=0)
            v2 = jnp.concatenate([bv_ref[0, band_b, cols], bv_ref[1, band_b, cols]], axis=0)
            o = _pair_attend(qb_ref[0, rows, cols], k2, v2, ind_b, _last_cols(bb_ref[p], B_COLS, n_b), None)
            ob_ref[0, rows, cols] = o.astype(BF16)

    n_blocks = tile // ATT_BLOCK

    @pl.when(j == 0)
    def _():
        for blk in range(n_blocks):
            g = blk * ATT_BLOCK
            block(g, g, min(g + ATT_BLOCK, A_COLS), min(g + ATT_BLOCK, B_COLS))

    @pl.when(j > 0)
    def _():
        def body(blk, carry):
            r0 = pl.multiple_of(blk * ATT_BLOCK, ATT_BLOCK)
            block(pl.multiple_of(j * tile + r0, ATT_BLOCK), r0, A_COLS, B_COLS)
            return carry
        lax.fori_loop(0, n_blocks, body, 0)


def _head_indicator(c):
    ind = np.zeros((2 * c, PAIR_W), np.float32)
    ind[:c, 0] = 1.0
    ind[c:, 1] = 1.0
    return ind


def _pair_constants():
    lane = np.arange(B_W) % PAIR_W
    lo = jnp.asarray((lane < HEAD_DIM).astype(np.float32).reshape(1, B_W), BF16)
    hi = jnp.asarray((lane >= HEAD_DIM).astype(np.float32).reshape(1, B_W), BF16)
    return lo, hi, jnp.asarray(_head_indicator(A_COLS), BF16), jnp.asarray(_head_indicator(B_COLS), BF16)


def _attn_prompt(qa, ka, va, qb, kb, vb, bias2_a, sinks2, bias2_b):
    b, s, _ = qa.shape
    tile = SEQ_TILE
    assert s % tile == 0 and tile % ATT_BLOCK == 0
    lo, hi, ind_a, ind_b = _pair_constants()
    tok = lambda w: pl.BlockSpec((1, tile, w), lambda i, j: (i, j, 0))
    seq = lambda w: pl.BlockSpec((1, s, w), lambda i, j: (i, 0, 0))
    const = lambda shp: pl.BlockSpec(shp, lambda i, j: (0,) * len(shp))
    return pl.pallas_call(
        functools.partial(_attn_prompt_kernel, tile=tile),
        out_shape=(jax.ShapeDtypeStruct((b, s, QA_W), BF16), jax.ShapeDtypeStruct((b, s, B_W), BF16)),
        grid=(b, s // tile),
        in_specs=[tok(QA_W), seq(KVA_W), seq(KVA_W), tok(B_W), seq(B_W), seq(B_W),
                  const(bias2_a.shape), const(sinks2.shape), const(bias2_b.shape), const(lo.shape), const(hi.shape),
                  const(ind_a.shape), const(ind_b.shape)],
        out_specs=(tok(QA_W), tok(B_W)),
        scratch_shapes=[pltpu.VMEM((4, s, PAIR_W), BF16), pltpu.VMEM((4, s, PAIR_W), BF16),
                        pltpu.VMEM((2, s, B_W), BF16), pltpu.VMEM((2, s, B_W), BF16)],
        compiler_params=_cparams("arbitrary", "arbitrary"),
        name="attn_prompt",
    )(qa, ka, va, qb, kb, vb, bias2_a, sinks2, bias2_b, lo, hi, ind_a, ind_b)


def _attn_sample_kernel(qa_ref, ka_ref, va_ref, qb_ref, kb_ref, vb_ref, ba_ref, sk_ref, bb_ref, lo_ref, hi_ref,
                        ia_ref, ib_ref, oa_ref, ob_ref):
    second = lambda tbl: tbl[CHUNK:ATT_BLOCK]
    pad = lambda x: jnp.concatenate([jnp.zeros((CHUNK, PAIR_W), BF16), x], axis=0)
    lo, hi = lo_ref[:, 0:PAIR_W], hi_ref[:, 0:PAIR_W]

    def swap(x):
        return jnp.concatenate([x[:, HEAD_DIM:], x[:, :HEAD_DIM]], axis=1)

    ka, va = ka_ref[0], va_ref[0]
    for kv in range(A_KV_HEADS):
        c0, c1 = slice(2 * kv * PAIR_W, (2 * kv + 1) * PAIR_W), slice((2 * kv + 1) * PAIR_W, (2 * kv + 2) * PAIR_W)
        q2 = jnp.concatenate([qa_ref[0, :, c0], qa_ref[0, :, c1]], axis=0)
        first = (lambda x: x * lo) if kv == 0 else (lambda x: swap(x) * lo)
        last = (lambda x: swap(x) * hi) if kv == 0 else (lambda x: x * hi)
        k2 = jnp.concatenate([pad(first(ka)), pad(last(ka))], axis=0)
        v2 = jnp.concatenate([pad(first(va)), pad(last(va))], axis=0)
        two = lambda tbl: jnp.concatenate([second(tbl[0:ATT_BLOCK]), second(tbl[ATT_BLOCK:])], axis=0)
        o = _pair_attend(q2, k2, v2, ia_ref[...], two(ba_ref[kv]), (two(sk_ref[kv, 0]), two(sk_ref[kv, 1])))
        oa_ref[0, :, c0] = o[0:CHUNK].astype(BF16)
        oa_ref[0, :, c1] = o[CHUNK:].astype(BF16)
    for p in range(B_HEADS // 2):
        cols = slice(p * PAIR_W, (p + 1) * PAIR_W)
        kb, vb = kb_ref[0, :, cols], vb_ref[0, :, cols]
        k2 = jnp.concatenate([pad(kb * lo), pad(kb * hi)], axis=0)
        v2 = jnp.concatenate([pad(vb * lo), pad(vb * hi)], axis=0)
        o = _pair_attend(qb_ref[0, :, cols], k2, v2, ib_ref[...], second(bb_ref[p]), None)
        ob_ref[0, :, cols] = o.astype(BF16)


def _attn_sample(qa, ka, va, qb, kb, vb, bias2_a, sinks2, bias2_b):
    b, t, _ = qa.shape
    assert t == CHUNK
    lo, hi, ind_a, ind_b = _pair_constants()
    per_b = lambda r, w: pl.BlockSpec((1, r, w), lambda i: (i, 0, 0))
    const = lambda shp: pl.BlockSpec(shp, lambda i: (0,) * len(shp))
    return pl.pallas_call(
        _attn_sample_kernel,
        out_shape=(jax.ShapeDtypeStruct((b, t, QA_W), BF16), jax.ShapeDtypeStruct((b, t, B_W), BF16)),
        grid=(b,),
        in_specs=[per_b(t, QA_W), per_b(A_BAND, KVA_W), per_b(A_BAND, KVA_W),
                  per_b(t, B_W), per_b(B_BAND, B_W), per_b(B_BAND, B_W),
                  const(bias2_a.shape), const(sinks2.shape), const(bias2_b.shape), const(lo.shape), const(hi.shape),
                  const(ind_a.shape), const(ind_b.shape)],
        out_specs=(per_b(t, QA_W), per_b(t, B_W)),
        compiler_params=_cparams("parallel"),
        name="attn_sample",
    )(qa, ka, va, qb, kb, vb, bias2_a, sinks2, bias2_b, lo, hi, ind_a, ind_b)


def _route(lt):
    t = lt.shape[1]
    el = lt[0:N_EXPERTS]
    gl = lt[N_EXPERTS:N_EXPERTS + N_GROUPS]
    gmax = jnp.max(gl, axis=0, keepdims=True)
    gi = lax.broadcasted_iota(jnp.int32, (N_GROUPS, t), 0)
    gidx = jnp.min(jnp.where(gl == gmax, gi, N_GROUPS), axis=0, keepdims=True)
    g_w = 1.0 / jnp.sum(jnp.exp(gl - gmax), axis=0, keepdims=True)
    e_sel = el[(N_GROUPS - 1) * EXPERTS_PER_GROUP:]
    for g in range(N_GROUPS - 2, -1, -1):
        e_sel = jnp.where(gidx == g, el[g * EXPERTS_PER_GROUP:(g + 1) * EXPERTS_PER_GROUP], e_sel)
    ei = lax.broadcasted_iota(jnp.int32, (EXPERTS_PER_GROUP, t), 0)
    m1 = jnp.max(e_sel, axis=0, keepdims=True)
    i1 = jnp.min(jnp.where(e_sel == m1, ei, EXPERTS_PER_GROUP), axis=0, keepdims=True)
    rest = jnp.where(ei == i1, -jnp.inf, e_sel)
    m2 = jnp.max(rest, axis=0, keepdims=True)
    i2 = jnp.min(jnp.where(rest == m2, ei, EXPERTS_PER_GROUP), axis=0, keepdims=True)
    ex = jnp.exp(m2 - m1)
    den = 1.0 + ex
    w1 = g_w * (1.0 / den)
    w2 = g_w * (ex / den)
    lo = jnp.minimum(i1, i2)
    hi = jnp.maximum(i1, i2)
    first_is_lo = i1 < i2
    w_lo = jnp.where(first_is_lo, w1, w2)
    w_hi = jnp.where(first_is_lo, w2, w1)
    pair = ((lo * (2 * EXPERTS_PER_GROUP - 1 - lo)) >> 1) + (hi - lo - 1)
    return gidx * PAIRS_PER_GROUP + pair, w_lo, w_hi


def _route_logits(wr_ref, br_ref, h2):
    return lax.dot_general(wr_ref[...], h2, (((1,), (1,)), ((), ())), preferred_element_type=F32) + br_ref[...]


def _weight_tile(w_lo, w_hi):
    t = w_lo.shape[1]
    return jnp.transpose(jnp.concatenate([w_lo, w_hi, jnp.zeros((LANES - 2, t), F32)], axis=0))


def _rank_in_class(cls, run_ref, tri_ref):
    t = cls.shape[1]
    onehot = lax.broadcasted_iota(jnp.int32, (CLASS_ROWS, t), 0) == cls
    ones = jnp.where(onehot, 1.0, 0.0)
    before = jnp.dot(ones.astype(BF16), tri_ref[...], preferred_element_type=F32)
    run = run_ref[...]
    rank = jnp.sum(jnp.where(onehot, before + run, 0.0), axis=0, keepdims=True)
    run_ref[...] = run + jnp.sum(ones, axis=1, keepdims=True)
    return rank.astype(jnp.int32)


def _post_kernel(x_ref, oa_ref, ob_ref, mod_ref, g1_ref, g1p_ref, g2_ref,
                 wg_ref, bg_ref, wpa_ref, wpb_ref, wo_ref, wr_ref, br_ref, tri_ref,
                 x1_ref, h2_ref, cls_ref, rank_ref, cnt_ref, run_ref):
    @pl.when((pl.program_id(0) == 0) & (pl.program_id(1) == 0))
    def _():
        run_ref[...] = jnp.zeros_like(run_ref)

    nb, rows, _ = x_ref.shape
    stack = lambda parts: parts[0] if nb == 1 else jnp.concatenate(parts, axis=0)
    xs = [x_ref[g] for g in range(nb)]
    h = stack([_norm_mod(xs[g], g1_ref[...], mod_ref[g, 1:2, :], mod_ref[g, 0:1, :]) for g in range(nb)]).astype(BF16)
    gates = jax.nn.sigmoid(jnp.dot(h, wg_ref[...], preferred_element_type=F32) + bg_ref[...])
    pa = jnp.dot(stack([oa_ref[g] for g in range(nb)]), wpa_ref[...], preferred_element_type=F32)
    pb = jnp.dot(stack([ob_ref[g] for g in range(nb)]), wpb_ref[...], preferred_element_type=F32)
    mixed = gates[:, :D_MODEL] * pa + gates[:, D_MODEL:] * pb
    y = jnp.dot(mixed.astype(BF16), wo_ref[...], preferred_element_type=F32)
    h2_parts = []
    for g in range(nb):
        x1 = xs[g] + mod_ref[g, 2:3, :] * _rms(y[g * rows:(g + 1) * rows], g1p_ref[...])
        x1_ref[g] = x1
        h2_parts.append(_norm_mod(x1, g2_ref[...], mod_ref[g, 4:5, :], mod_ref[g, 3:4, :]))
    h2 = stack(h2_parts).astype(BF16)
    cls, w_lo, w_hi = _route(_route_logits(wr_ref, br_ref, h2))
    cls_ref[0] = cls
    rank_ref[0] = _rank_in_class(cls, run_ref, tri_ref)
    cnt_ref[...] = run_ref[...]
    t = nb * rows
    h2f = h2.astype(F32)
    for c in range(D_MODEL // LANES):
        h2_ref[pl.ds(c, t, stride=IN_SLAB), :] = h2f[:, c * LANES:(c + 1) * LANES]
    h2_ref[pl.ds(D_MODEL // LANES, t, stride=IN_SLAB), :] = _weight_tile(w_lo, w_hi)


def _post(x, oa, ob, mods, g_pre_mix, g_post_mix, g_pre_ffn, wg, bg, wpa, wpb, wo, wr, br, tile):
    b, s, d = x.shape
    rows = min(s, tile)
    nb = math.gcd(tile // rows, b)
    tile = nb * rows
    assert s % rows == 0
    nt = s // rows
    n_steps = (b // nb) * nt
    tri = (jnp.arange(tile)[:, None] < jnp.arange(tile)[None, :]).astype(BF16)
    tok = lambda w: pl.BlockSpec((nb, rows, w), lambda i, j: (i, j, 0))
    const = lambda shp: pl.BlockSpec(shp, lambda i, j: (0,) * len(shp))
    per_tile = pl.BlockSpec((1, 1, tile), lambda i, j: (i * nt + j, 0, 0))
    return pl.pallas_call(
        _post_kernel,
        out_shape=(
            jax.ShapeDtypeStruct((b, s, d), F32),
            jax.ShapeDtypeStruct((b * s * IN_SLAB, LANES), F32),
            jax.ShapeDtypeStruct((n_steps, 1, tile), jnp.int32),
            jax.ShapeDtypeStruct((n_steps, 1, tile), jnp.int32),
            jax.ShapeDtypeStruct((CLASS_ROWS, 1), F32),
        ),
        grid=(b // nb, nt),
        in_specs=[
            tok(d), tok(QA_W), tok(B_W),
            pl.BlockSpec((nb, 6, d), lambda i, j: (i, 0, 0)),
            const((1, d)), const((1, d)), const((1, d)),
            const(wg.shape), const(bg.shape), const(wpa.shape), const(wpb.shape), const(wo.shape),
            const(wr.shape), const(br.shape), const(tri.shape),
        ],
        out_specs=(tok(d), pl.BlockSpec((tile * IN_SLAB, LANES), lambda i, j: (i * nt + j, 0)),
                   per_tile, per_tile, const((CLASS_ROWS, 1))),
        scratch_shapes=[pltpu.VMEM((CLASS_ROWS, 1), F32)],
        compiler_params=_cparams("arbitrary", "arbitrary"),
        name="post",
    )(x, oa, ob, mods, g_pre_mix, g_post_mix, g_pre_ffn, wg, bg, wpa, wpb, wo, wr, br, tri)


def _invert_kernel(upos_ref, row_ref, *, n, scale):
    def body(i, carry):
        for u in range(ISSUE_UNROLL):
            t = i * ISSUE_UNROLL + u
            row_ref[upos_ref[t]] = t * scale
        return carry
    lax.fori_loop(0, n // ISSUE_UNROLL, body, 0)


def _invert(upos, scale):
    n = upos.shape[0]
    return pl.pallas_call(
        functools.partial(_invert_kernel, n=n, scale=scale),
        out_shape=jax.ShapeDtypeStruct((n,), jnp.int32),
        in_specs=[pl.BlockSpec(memory_space=pltpu.SMEM)],
        out_specs=pl.BlockSpec(memory_space=pltpu.SMEM),
        name="invert",
    )(upos)


def _expert(xb, wgu_ref, wd_ref):
    gu = jnp.dot(xb, wgu_ref[0], preferred_element_type=F32)
    gate = gu[:, :D_EXPERT]
    he = (gate * jax.nn.sigmoid(gate)) * gu[:, D_EXPERT:]
    return jnp.dot(he.astype(BF16), wd_ref[0], preferred_element_type=F32)


def _moe_kernel(tok_ref, base_ref, ea_ref, eb_ref, nu_ref, h_ref, wgu_a_ref, wd_a_ref, wgu_b_ref, wd_b_ref,
                ys_ref, xbuf, sems, *, tile, n_tok):
    t = pl.program_id(0)
    nu = nu_ref[0]
    slot = t % 2

    def slab_copy(src_row0, dst_slot, r):
        return pltpu.make_async_copy(h_ref.at[pl.ds(src_row0, IN_SLAB)],
                                     xbuf.at[dst_slot, pl.ds(r * IN_SLAB, IN_SLAB)], sems.at[dst_slot])

    def fetch(step, dst_slot):
        base = base_ref[step]
        for r in range(tile):
            slab_copy(tok_ref[jnp.minimum(base + r, n_tok - 1)], dst_slot, r).start(priority=r % 2)

    def wait(dst_slot):
        for r in range(tile):
            slab_copy(0, dst_slot, 0).wait()

    @pl.when(t == 0)
    def _():
        fetch(0, 0)

    @pl.when(t < nu)
    def _():
        wait(slot)
        fetch(jnp.minimum(t + 1, nu - 1), 1 - slot)
        lane_tile = lambda c: xbuf[slot, pl.ds(c, tile, stride=IN_SLAB), :]
        xb = jnp.concatenate([lane_tile(c) for c in range(D_MODEL // LANES)], axis=1).astype(BF16)
        w = lane_tile(D_MODEL // LANES)
        y = w[:, 0:1] * _expert(xb, wgu_a_ref, wd_a_ref) + w[:, 1:2] * _expert(xb, wgu_b_ref, wd_b_ref)
        for c in range(OUT_SLAB):
            ys_ref[pl.ds(c, tile, stride=OUT_SLAB), :] = y[:, c * LANES:(c + 1) * LANES]

    @pl.when(t == nu - 1)
    def _():
        wait(1 - slot)

    @pl.when(t >= nu)
    def _():
        ys_ref[...] = jnp.zeros_like(ys_ref)


def _moe_grouped(h_slabs, sorted_tok, tile_base, tile_ea, tile_eb, n_used, wgu, wd, tile, n_tiles):
    d = D_MODEL
    n_tok = sorted_tok.shape[0]
    expert = lambda which, shp: pl.BlockSpec(shp, lambda t, tok, base, ea, eb, nu: ((ea, eb)[which][t], 0, 0))
    return pl.pallas_call(
        functools.partial(_moe_kernel, tile=tile, n_tok=n_tok),
        out_shape=jax.ShapeDtypeStruct((n_tiles * tile * OUT_SLAB, LANES), F32),
        grid_spec=pltpu.PrefetchScalarGridSpec(
            num_scalar_prefetch=5,
            grid=(n_tiles,),
            in_specs=[
                pl.BlockSpec(memory_space=pl.ANY),
                expert(0, (1, d, 2 * D_EXPERT)), expert(0, (1, D_EXPERT, d)),
                expert(1, (1, d, 2 * D_EXPERT)), expert(1, (1, D_EXPERT, d)),
            ],
            out_specs=pl.BlockSpec((tile * OUT_SLAB, LANES), lambda t, *_: (t, 0)),
            scratch_shapes=[pltpu.VMEM((2, tile * IN_SLAB, LANES), F32), pltpu.SemaphoreType.DMA((2,))],
        ),
        compiler_params=_cparams("arbitrary"),
        name="moe_grouped",
    )(sorted_tok, tile_base, tile_ea, tile_eb, n_used, h_slabs, wgu, wd, wgu, wd)


_PAIR_LO, _PAIR_HI = np.triu_indices(EXPERTS_PER_GROUP, k=1)
_CLASS_LO = np.concatenate([g * EXPERTS_PER_GROUP + _PAIR_LO for g in range(N_GROUPS)]).astype(np.int32)
_CLASS_HI = np.concatenate([g * EXPERTS_PER_GROUP + _PAIR_HI for g in range(N_GROUPS)]).astype(np.int32)


def _moe_plan(counts, cls, rank, n, tile):
    p_max = -(-(n + N_CLASSES * (tile - 1)) // tile) * tile
    nt = p_max // tile
    counts = counts[:N_CLASSES].astype(jnp.int32)
    padded = ((counts + tile - 1) // tile) * tile
    pad_end = jnp.cumsum(padded)
    gstart = pad_end - padded
    ustart = jnp.cumsum(counts) - counts
    onehot = cls[:, None] == jnp.arange(N_CLASSES, dtype=jnp.int32)[None, :]
    pos = rank + jnp.sum(jnp.where(onehot, gstart[None, :], 0), axis=1)
    upos = rank + jnp.sum(jnp.where(onehot, ustart[None, :], 0), axis=1)
    n_used = (pad_end[-1] // tile).astype(jnp.int32).reshape(1)
    tiles = jnp.arange(nt, dtype=jnp.int32)
    tile_cls = jnp.minimum(jnp.sum((pad_end[None, :] <= (tiles * tile)[:, None]).astype(jnp.int32), axis=1), N_CLASSES - 1)
    tile_ea = jnp.asarray(_CLASS_LO)[tile_cls]
    tile_eb = jnp.asarray(_CLASS_HI)[tile_cls]
    tile_base = jnp.clip(ustart[tile_cls] + tiles * tile - gstart[tile_cls], 0, n - 1)
    return nt, pos, upos, tile_base, tile_ea, tile_eb, n_used


def _final_kernel(pos_ref, x1_ref, ys_ref, mod_ref, g_ref, o_ref, ybuf, sems, *, tile, n_steps):
    s = pl.program_id(0)
    slot = s % 2

    def slab_copy(src_row0, dst_slot, dst_row):
        return pltpu.make_async_copy(ys_ref.at[pl.ds(src_row0, OUT_SLAB)],
                                     ybuf.at[dst_slot, pl.ds(dst_row * OUT_SLAB, OUT_SLAB)], sems.at[dst_slot])

    def fetch(step, dst_slot):
        for r in range(tile):
            slab_copy(pos_ref[step * tile + r], dst_slot, r).start(priority=r % 2)

    def wait(dst_slot):
        for r in range(tile):
            slab_copy(0, dst_slot, 0).wait()

    @pl.when(s == 0)
    def _():
        fetch(0, 0)

    wait(slot)
    fetch(jnp.minimum(s + 1, n_steps - 1), 1 - slot)
    y = jnp.concatenate([ybuf[slot, pl.ds(c, tile, stride=OUT_SLAB), :] for c in range(OUT_SLAB)], axis=1)
    o_ref[...] = x1_ref[...] + mod_ref[0, 5:6, :] * _rms(y, g_ref[...])

    @pl.when(s == n_steps - 1)
    def _():
        wait(1 - slot)


def _final(x1, ys, pos, mods, g_post_ffn, tile):
    b, s, d = x1.shape
    n = b * s
    per_b = s // tile
    n_steps = n // tile
    tok = pl.BlockSpec((tile, d), lambda i, *_: (i, 0))
    out = pl.pallas_call(
        functools.partial(_final_kernel, tile=tile, n_steps=n_steps),
        out_shape=jax.ShapeDtypeStruct((n, d), F32),
        grid_spec=pltpu.PrefetchScalarGridSpec(
            num_scalar_prefetch=1,
            grid=(n_steps,),
            in_specs=[
                tok,
                pl.BlockSpec(memory_space=pl.ANY),
                pl.BlockSpec((1, 6, d), lambda i, *_: (i // per_b, 0, 0)),
                pl.BlockSpec((1, d), lambda i, *_: (0, 0)),
            ],
            out_specs=tok,
            scratch_shapes=[pltpu.VMEM((2, tile * OUT_SLAB, LANES), F32), pltpu.SemaphoreType.DMA((2,))],
        ),
        compiler_params=_cparams("arbitrary"),
        name="final",
    )(pos, x1.reshape(n, d), ys, mods, g_post_ffn)
    return out.reshape(b, s, d)


def _t5_bucket(rel):
    half = T5_BUCKETS // 2
    exact = half // 2
    ret = jnp.where(rel > 0, half, 0)
    n = jnp.abs(rel)
    nf = jnp.maximum(n, 1).astype(F32)
    large = exact + (jnp.log(nf / exact) / math.log(T5_MAX_DIST / exact) * (half - exact)).astype(jnp.int32)
    large = jnp.minimum(large, half - 1)
    return ret + jnp.where(n < exact, n, large)


def _toeplitz(u, n_rows, n_cols):
    lead, p = u.shape[:-1], n_rows + n_cols
    w = jnp.concatenate([u, jnp.zeros(lead + (1,), u.dtype)], axis=-1)
    periodic = jnp.broadcast_to(w[..., None, :], lead + (n_rows + 1, p)).reshape(lead + ((n_rows + 1) * p,))
    shifted = periodic[..., :n_rows * (p + 1)].reshape(lead + (n_rows, p + 1))[..., :n_cols]
    return shifted[..., ::-1, :]


def _bias_tables(t5_table, rel_table):
    ja = jnp.arange(CHUNK - 1 + A_BAND)
    ua = t5_table[_t5_bucket(ja - (CHUNK - 1) - A_WINDOW)].T.astype(F32)
    jb = jnp.arange(CHUNK - 1 + B_BAND)
    ub = rel_table[:, jnp.clip((CHUNK - 1) - jb + B_REACH, -REL_CLIP, REL_CLIP) + REL_CLIP].astype(F32)
    return _toeplitz(ua, CHUNK, A_BAND), _toeplitz(ub, CHUNK, B_BAND)


def _two_chunk(bias):
    masked = jnp.full(bias.shape[:-1] + (CHUNK,), MASKED, F32)
    return jnp.concatenate([jnp.concatenate([bias, masked], axis=-1), jnp.concatenate([masked, bias], axis=-1)], axis=-2)


def _pair_tables(bias_a, bias_b, sinks):
    a2, b2 = _two_chunk(bias_a), _two_chunk(bias_b)
    stack2 = lambda x, h0, h1: jnp.concatenate([x[h0], x[h1]], axis=0)
    bias2_a = jnp.stack([jnp.concatenate([stack2(a2, 4 * kv, 4 * kv + 2), stack2(a2, 4 * kv + 1, 4 * kv + 3)], axis=1)
                         for kv in range(A_KV_HEADS)])
    bias2_b = jnp.concatenate([b2[0::2], b2[1::2]], axis=-1)
    sk = jnp.broadcast_to(sinks.astype(F32).reshape(A_Q_HEADS, 1, 1), (A_Q_HEADS, ATT_BLOCK, 1))
    sinks2 = jnp.stack([jnp.stack([stack2(sk, 4 * kv, 4 * kv + 2), stack2(sk, 4 * kv + 1, 4 * kv + 3)])
                        for kv in range(A_KV_HEADS)])
    return bias2_a, sinks2, bias2_b


def _moe_and_final(x1, h2e, cls, rank, counts, mods, g_post_ffn, wgu, wd, seq_tile, moe_tile):
    b, s, d = x1.shape
    n = b * s
    n_tiles, pos, upos, tile_base, tile_ea, tile_eb, n_used = _moe_plan(
        counts.reshape(CLASS_ROWS), cls.reshape(n), rank.reshape(n), n, moe_tile)
    ys = _moe_grouped(h2e, _invert(upos, IN_SLAB), tile_base, tile_ea, tile_eb, n_used, wgu, wd, moe_tile, n_tiles)
    return _final(x1, ys, pos * OUT_SLAB, mods, g_post_ffn, seq_tile)


def kernel(x_prompt, x_sample, c_prompt, c_sample, cache_a_k, cache_a_v, cache_b_k, cache_b_v, w_ada, b_ada, g_pre_mix, g_post_mix, g_pre_ffn, g_post_ffn, w_in, a_sinks, t5_table, b_rel_table, w_proj_a, w_proj_b, w_gate, b_gate, w_o, w_route_g, b_route_g, w_route_e, b_route_e, w_e_gate, w_e_up, w_e_down):
    depth = w_in.shape[0]
    assert depth == 1
    l = 0
    bp, sp, d = x_prompt.shape
    bs, ts, _ = x_sample.shape

    mods = _ada(jnp.concatenate([c_prompt, c_sample], axis=0), w_ada[l], b_ada[l]).reshape(bp + bs, 6, d)
    mods_p, mods_s = mods[:bp], mods[bp:]

    w_in_bf = w_in[l].astype(BF16)
    wg, wpa, wpb, wo = w_gate[l].astype(BF16), w_proj_a[l].astype(BF16), w_proj_b[l].astype(BF16), w_o[l].astype(BF16)
    bg = b_gate[l].reshape(1, 2 * d)
    pad_rows = ROUTE_ROWS - N_EXPERTS - N_GROUPS
    wr = jnp.concatenate([w_route_e[l].T, w_route_g[l].T, jnp.zeros((pad_rows, d), F32)], axis=0).astype(BF16)
    br = jnp.concatenate([b_route_e[l], b_route_g[l], jnp.zeros((pad_rows,), F32)]).reshape(ROUTE_ROWS, 1)
    wgu = jnp.concatenate([w_e_gate[l], w_e_up[l]], axis=-1).astype(BF16)
    wd = w_e_down[l].astype(BF16)
    g1, g1p, g2, g2p = (g[l].reshape(1, d) for g in (g_pre_mix, g_post_mix, g_pre_ffn, g_post_ffn))

    bias_a, bias_b = _bias_tables(t5_table, b_rel_table[l])
    bias2_a, sinks2, bias2_b = _pair_tables(bias_a, bias_b, a_sinks[l])

    qa, ka, va, qb, kb, vb, sak_p, sav_p, sbk_p, sbv_p = _pre_prompt(x_prompt, mods_p, g1, w_in_bf)
    oa, ob = _attn_prompt(qa, ka, va, qb, kb, vb, bias2_a, sinks2, bias2_b)
    x1, h2e, cls, rank, counts = _post(x_prompt, oa, ob, mods_p, g1, g1p, g2, wg, bg, wpa, wpb, wo, wr, br, SEQ_TILE)
    y_prompt = _moe_and_final(x1, h2e, cls, rank, counts, mods_p, g2p, wgu, wd, SEQ_TILE, 256)

    la, lb = cache_a_k.shape[2], cache_b_k.shape[2]
    cak = cache_a_k[l].reshape(bs, la, KVA_W)
    cav = cache_a_v[l].reshape(bs, la, KVA_W)
    cbk = cache_b_k[l].reshape(bs, lb, B_W)
    cbv = cache_b_v[l].reshape(bs, lb, B_W)
    qa, ka, va, qb, kb, vb, sak_s, sav_s, sbk_s, sbv_s = _pre_sample(x_sample, mods_s, g1, w_in_bf, cak, cav, cbk, cbv)
    oa, ob = _attn_sample(qa, ka, va, qb, kb, vb, bias2_a, sinks2, bias2_b)
    x1, h2e, cls, rank, counts = _post(x_sample, oa, ob, mods_s, g1, g1p, g2, wg, bg, wpa, wpb, wo, wr, br, SEQ_TILE)
    y_sample = _moe_and_final(x1, h2e, cls, rank, counts, mods_s, g2p, wgu, wd, ts, 64)

    a_state = lambda v, b, r: v.reshape(1, b, r, A_KV_HEADS, HEAD_DIM)
    b_state = lambda v, b, r: v.reshape(1, b, r, B_HEADS, HEAD_DIM)
    return (y_prompt, y_sample,
            a_state(sak_p, bp, A_WINDOW), a_state(sav_p, bp, A_WINDOW),
            b_state(sbk_p, bp, B_REACH), b_state(sbv_p, bp, B_REACH),
            a_state(sak_s, bs, la), a_state(sav_s, bs, la),
            b_state(sbk_s, bs, lb), b_state(sbv_s, bs, lb))
```

```python
import functools
import math

import numpy as np
import jax
import jax.numpy as jnp
from jax import lax
from jax.experimental import pallas as pl
from jax.experimental.pallas import tpu as pltpu

D_MODEL = 1024
CHUNK = 64
HEAD_DIM = 64
A_Q_HEADS = 8
A_KV_HEADS = 2
A_WINDOW = 128
B_HEADS = 8
B_REACH = 8 * CHUNK
REL_CLIP = 256
T5_BUCKETS = 32
T5_MAX_DIST = 128
N_GROUPS = 4
EXPERTS_PER_GROUP = 8
N_EXPERTS = N_GROUPS * EXPERTS_PER_GROUP
D_EXPERT = D_MODEL // 4
EPS = 1e-6

QA_W = A_Q_HEADS * HEAD_DIM
KVA_W = A_KV_HEADS * HEAD_DIM
B_W = B_HEADS * HEAD_DIM
IN_W = QA_W + 2 * KVA_W + 3 * B_W
A_BAND = A_WINDOW + CHUNK
B_BAND = B_REACH + CHUNK

PAIRS_PER_GROUP = EXPERTS_PER_GROUP * (EXPERTS_PER_GROUP - 1) // 2
N_CLASSES = N_GROUPS * PAIRS_PER_GROUP
ROUTE_ROWS = 40
CLASS_ROWS = 128
LANES = 128
IN_SLAB = D_MODEL // LANES + 1
OUT_SLAB = D_MODEL // LANES

F32 = jnp.float32
BF16 = jnp.bfloat16

VMEM_LIMIT_BYTES = 56 * 1024 * 1024

SEQ_TILE = 512
MOE_TILE_PROMPT = 256
MOE_TILE_SAMPLE = 64
INVERT_UNROLL = 32


def _cparams(*sem):
    return pltpu.CompilerParams(dimension_semantics=sem, vmem_limit_bytes=VMEM_LIMIT_BYTES)


def _norm_mod(x, g, scale, shift):
    y = x * lax.rsqrt(jnp.mean(x * x, axis=-1, keepdims=True) + EPS)
    return (y * g) * (1.0 + scale) + shift


def _rms(x, g):
    return (x * lax.rsqrt(jnp.mean(x * x, axis=-1, keepdims=True) + EPS)) * g


def _ada_kernel(c_ref, w_ref, b_ref, o_ref):
    c = c_ref[...]
    s = (c * jax.nn.sigmoid(c)).astype(BF16)
    o_ref[...] = jnp.dot(s, w_ref[...].astype(BF16), preferred_element_type=F32) + b_ref[...]


def _ada(c, w_ada, b_ada):
    n, d = c.shape
    wn = w_ada.shape[1]
    tn = 512
    return pl.pallas_call(
        _ada_kernel,
        out_shape=jax.ShapeDtypeStruct((n, wn), F32),
        grid=(wn // tn,),
        in_specs=[
            pl.BlockSpec((n, d), lambda j: (0, 0)),
            pl.BlockSpec((d, tn), lambda j: (0, j)),
            pl.BlockSpec((1, tn), lambda j: (0, j)),
        ],
        out_specs=pl.BlockSpec((n, tn), lambda j: (0, j)),
        compiler_params=_cparams("arbitrary"),
        name="ada",
    )(c, w_ada, b_ada.reshape(1, wn))


_COL_QA = (0, QA_W)
_COL_KA = (QA_W, QA_W + KVA_W)
_COL_VA = (QA_W + KVA_W, QA_W + 2 * KVA_W)
_COL_QB = (QA_W + 2 * KVA_W, QA_W + 2 * KVA_W + B_W)
_COL_KB = (_COL_QB[1], _COL_QB[1] + B_W)
_COL_VB = (_COL_KB[1], _COL_KB[1] + B_W)
Q_SCALE = HEAD_DIM ** -0.5


def _project(x_ref, mod_ref, g_ref, w_ref):
    h = _norm_mod(x_ref[0], g_ref[...], mod_ref[0, 1:2, :], mod_ref[0, 0:1, :])
    return jnp.dot(h.astype(BF16), w_ref[...], preferred_element_type=F32)


def _cols(p, c):
    return p[:, c[0]:c[1]]


def _pre_prompt_kernel(x_ref, mod_ref, g_ref, w_ref,
                       qa_ref, ka_ref, va_ref, qb_ref, kb_ref, vb_ref,
                       sak_ref, sav_ref, sbk_ref, sbv_ref, *, n_tiles, tile):
    p = _project(x_ref, mod_ref, g_ref, w_ref)
    qa_ref[0] = (_cols(p, _COL_QA) * Q_SCALE).astype(BF16)
    ka_ref[0] = _cols(p, _COL_KA).astype(BF16)
    va_ref[0] = _cols(p, _COL_VA).astype(BF16)
    qb_ref[0] = (_cols(p, _COL_QB) * Q_SCALE).astype(BF16)
    kb_ref[0] = _cols(p, _COL_KB).astype(BF16)
    vb_ref[0] = _cols(p, _COL_VB).astype(BF16)

    @pl.when(pl.program_id(1) == n_tiles - 1)
    def _():
        sak_ref[0] = _cols(p, _COL_KA)[tile - A_WINDOW:, :]
        sav_ref[0] = _cols(p, _COL_VA)[tile - A_WINDOW:, :]
        sbk_ref[0] = _cols(p, _COL_KB)[tile - B_REACH:, :]
        sbv_ref[0] = _cols(p, _COL_VB)[tile - B_REACH:, :]


def _pre_prompt(x, mods, g_pre, w_in_bf):
    b, s, d = x.shape
    tile = SEQ_TILE
    assert s % tile == 0 and tile >= B_REACH and s >= B_REACH
    nt = s // tile
    tok = lambda w: pl.BlockSpec((1, tile, w), lambda i, j: (i, j, 0))
    state = lambda r, w: pl.BlockSpec((1, r, w), lambda i, j: (i, 0, 0))
    return pl.pallas_call(
        functools.partial(_pre_prompt_kernel, n_tiles=nt, tile=tile),
        out_shape=(
            jax.ShapeDtypeStruct((b, s, QA_W), BF16),
            jax.ShapeDtypeStruct((b, s, KVA_W), BF16),
            jax.ShapeDtypeStruct((b, s, KVA_W), BF16),
            jax.ShapeDtypeStruct((b, s, B_W), BF16),
            jax.ShapeDtypeStruct((b, s, B_W), BF16),
            jax.ShapeDtypeStruct((b, s, B_W), BF16),
            jax.ShapeDtypeStruct((b, A_WINDOW, KVA_W), F32),
            jax.ShapeDtypeStruct((b, A_WINDOW, KVA_W), F32),
            jax.ShapeDtypeStruct((b, B_REACH, B_W), F32),
            jax.ShapeDtypeStruct((b, B_REACH, B_W), F32),
        ),
        grid=(b, nt),
        in_specs=[
            tok(d),
            pl.BlockSpec((1, 6, d), lambda i, j: (i, 0, 0)),
            pl.BlockSpec((1, d), lambda i, j: (0, 0)),
            pl.BlockSpec((d, IN_W), lambda i, j: (0, 0)),
        ],
        out_specs=(
            tok(QA_W), tok(KVA_W), tok(KVA_W), tok(B_W), tok(B_W), tok(B_W),
            state(A_WINDOW, KVA_W), state(A_WINDOW, KVA_W), state(B_REACH, B_W), state(B_REACH, B_W),
        ),
        compiler_params=_cparams("parallel", "arbitrary"),
        name="pre_prompt",
    )(x, mods, g_pre, w_in_bf)


def _pre_sample_kernel(x_ref, mod_ref, g_ref, w_ref, cak_ref, cav_ref, cbk_ref, cbv_ref,
                       qa_ref, ka_ref, va_ref, qb_ref, kb_ref, vb_ref,
                       sak_ref, sav_ref, sbk_ref, sbv_ref, *, t, la, lb):
    p = _project(x_ref, mod_ref, g_ref, w_ref)
    qa_ref[0] = (_cols(p, _COL_QA) * Q_SCALE).astype(BF16)
    qb_ref[0] = (_cols(p, _COL_QB) * Q_SCALE).astype(BF16)
    for cache_ref, band_ref, state_ref, col, hist in (
            (cak_ref, ka_ref, sak_ref, _COL_KA, la), (cav_ref, va_ref, sav_ref, _COL_VA, la),
            (cbk_ref, kb_ref, sbk_ref, _COL_KB, lb), (cbv_ref, vb_ref, sbv_ref, _COL_VB, lb)):
        new = _cols(p, col)
        cache = cache_ref[0]
        band_ref[0, 0:hist, :] = cache.astype(BF16)
        band_ref[0, hist:hist + t, :] = new.astype(BF16)
        state_ref[0, 0:hist - t, :] = cache[t:, :]
        state_ref[0, hist - t:hist, :] = new


def _pre_sample(x, mods, g_pre, w_in_bf, cak, cav, cbk, cbv):
    b, t, d = x.shape
    la, lb = cak.shape[1], cbk.shape[1]
    assert t == CHUNK and la == A_WINDOW and lb == B_REACH
    per_b = lambda r, w: pl.BlockSpec((1, r, w), lambda i: (i, 0, 0))
    return pl.pallas_call(
        functools.partial(_pre_sample_kernel, t=t, la=la, lb=lb),
        out_shape=(
            jax.ShapeDtypeStruct((b, t, QA_W), BF16),
            jax.ShapeDtypeStruct((b, la + t, KVA_W), BF16),
            jax.ShapeDtypeStruct((b, la + t, KVA_W), BF16),
            jax.ShapeDtypeStruct((b, t, B_W), BF16),
            jax.ShapeDtypeStruct((b, lb + t, B_W), BF16),
            jax.ShapeDtypeStruct((b, lb + t, B_W), BF16),
            jax.ShapeDtypeStruct((b, la, KVA_W), F32),
            jax.ShapeDtypeStruct((b, la, KVA_W), F32),
            jax.ShapeDtypeStruct((b, lb, B_W), F32),
            jax.ShapeDtypeStruct((b, lb, B_W), F32),
        ),
        grid=(b,),
        in_specs=[
            per_b(t, d),
            per_b(6, d),
            pl.BlockSpec((1, d), lambda i: (0, 0)),
            pl.BlockSpec((d, IN_W), lambda i: (0, 0)),
            per_b(la, KVA_W), per_b(la, KVA_W), per_b(lb, B_W), per_b(lb, B_W),
        ],
        out_specs=(
            per_b(t, QA_W), per_b(la + t, KVA_W), per_b(la + t, KVA_W),
            per_b(t, B_W), per_b(lb + t, B_W), per_b(lb + t, B_W),
            per_b(la, KVA_W), per_b(la, KVA_W), per_b(lb, B_W), per_b(lb, B_W),
        ),
        compiler_params=_cparams("parallel"),
        name="pre_sample",
    )(x, mods, g_pre, w_in_bf, cak, cav, cbk, cbv)


MASKED = -1e30
PAIR_W = 2 * HEAD_DIM
ATT_BLOCK = 2 * CHUNK
A_COLS = A_WINDOW + ATT_BLOCK
B_COLS = B_REACH + ATT_BLOCK


def _pair_attend(q2, k2, v2, ind, bias, sinks):
    c = k2.shape[0] // 2
    s = lax.dot_general(q2, k2, (((1,), (1,)), ((), ())), preferred_element_type=F32) + bias
    es, mxs = [], []
    for half in range(2):
        sh = s[:, half * c:(half + 1) * c]
        mx = jnp.max(sh, axis=-1, keepdims=True)
        if sinks is not None:
            mx = jnp.maximum(mx, sinks[half])
        es.append(jnp.exp(sh - mx).astype(BF16))
        mxs.append(mx)
    o = jnp.dot(jnp.concatenate(es, axis=1), jnp.concatenate([v2, ind], axis=1), preferred_element_type=F32)
    dens = [o[:, PAIR_W + half:PAIR_W + half + 1] for half in range(2)]
    if sinks is not None:
        dens = [dens[half] + jnp.exp(sinks[half] - mxs[half]) for half in range(2)]
    lane = lax.broadcasted_iota(jnp.int32, (o.shape[0], PAIR_W), 1)
    return o[:, :PAIR_W] / jnp.where(lane < HEAD_DIM, dens[0], dens[1])


def _last_cols(table, full, n):
    if n == full:
        return table
    return jnp.concatenate([table[:, full - n:full], table[:, 2 * full - n:2 * full]], axis=1)


def _attn_prompt_kernel(qa_ref, ka_ref, va_ref, qb_ref, kb_ref, vb_ref, ba_ref, sk_ref, bb_ref, lo_ref, hi_ref,
                        ia_ref, ib_ref, oa_ref, ob_ref, ak_ref, av_ref, bk_ref, bv_ref, *, tile):
    j = pl.program_id(1)

    @pl.when(j == 0)
    def _():
        lo, hi = lo_ref[:, 0:PAIR_W], hi_ref[:, 0:PAIR_W]
        for src, dst in ((ka_ref, ak_ref), (va_ref, av_ref)):
            x = src[0]
            swapped = jnp.concatenate([x[:, HEAD_DIM:], x[:, :HEAD_DIM]], axis=1)
            dst[0] = x * lo
            dst[1] = swapped * hi
            dst[2] = swapped * lo
            dst[3] = x * hi
        for src, dst in ((kb_ref, bk_ref), (vb_ref, bv_ref)):
            x = src[0]
            dst[0] = x * lo_ref[...]
            dst[1] = x * hi_ref[...]

    def block(g, r0, n_a, n_b):
        rows = pl.ds(r0, ATT_BLOCK)
        band_a = pl.ds(g + ATT_BLOCK - n_a, n_a)
        band_b = pl.ds(g + ATT_BLOCK - n_b, n_b)
        ind_a = jnp.concatenate([ia_ref[0:n_a, :], ia_ref[A_COLS:A_COLS + n_a, :]], axis=0)
        ind_b = jnp.concatenate([ib_ref[0:n_b, :], ib_ref[B_COLS:B_COLS + n_b, :]], axis=0)
        for kv in range(A_KV_HEADS):
            c0, c1 = slice(2 * kv * PAIR_W, (2 * kv + 1) * PAIR_W), slice((2 * kv + 1) * PAIR_W, (2 * kv + 2) * PAIR_W)
            q2 = jnp.concatenate([qa_ref[0, rows, c0], qa_ref[0, rows, c1]], axis=0)
            k2 = jnp.concatenate([ak_ref[2 * kv, band_a, :], ak_ref[2 * kv + 1, band_a, :]], axis=0)
            v2 = jnp.concatenate([av_ref[2 * kv, band_a, :], av_ref[2 * kv + 1, band_a, :]], axis=0)
            o = _pair_attend(q2, k2, v2, ind_a, _last_cols(ba_ref[kv], A_COLS, n_a), (sk_ref[kv, 0], sk_ref[kv, 1]))
            oa_ref[0, rows, c0] = o[0:ATT_BLOCK].astype(BF16)
            oa_ref[0, rows, c1] = o[ATT_BLOCK:].astype(BF16)
        for p in range(B_HEADS // 2):
            cols = slice(p * PAIR_W, (p + 1) * PAIR_W)
            k2 = jnp.concatenate([bk_ref[0, band_b, cols], bk_ref[1, band_b, cols]], axis=0)
            v2 = jnp.concatenate([bv_ref[0, band_b, cols], bv_ref[1, band_b, cols]], axis=0)
            o = _pair_attend(qb_ref[0, rows, cols], k2, v2, ind_b, _last_cols(bb_ref[p], B_COLS, n_b), None)
            ob_ref[0, rows, cols] = o.astype(BF16)

    n_blocks = tile // ATT_BLOCK

    @pl.when(j == 0)
    def _():
        for blk in range(n_blocks):
            g = blk * ATT_BLOCK
            block(g, g, min(g + ATT_BLOCK, A_COLS), min(g + ATT_BLOCK, B_COLS))

    @pl.when(j > 0)
    def _():
        def body(blk, carry):
            r0 = pl.multiple_of(blk * ATT_BLOCK, ATT_BLOCK)
            block(pl.multiple_of(j * tile + r0, ATT_BLOCK), r0, A_COLS, B_COLS)
            return carry
        lax.fori_loop(0, n_blocks, body, 0)


def _head_indicator(c):
    ind = np.zeros((2 * c, PAIR_W), np.float32)
    ind[:c, 0] = 1.0
    ind[c:, 1] = 1.0
    return ind


def _pair_constants():
    lane = np.arange(B_W) % PAIR_W
    lo = jnp.asarray((lane < HEAD_DIM).astype(np.float32).reshape(1, B_W), BF16)
    hi = jnp.asarray((lane >= HEAD_DIM).astype(np.float32).reshape(1, B_W), BF16)
    return lo, hi, jnp.asarray(_head_indicator(A_COLS), BF16), jnp.asarray(_head_indicator(B_COLS), BF16)


def _attn_prompt(qa, ka, va, qb, kb, vb, bias2_a, sinks2, bias2_b):
    b, s, _ = qa.shape
    tile = SEQ_TILE
    assert s % tile == 0 and tile % ATT_BLOCK == 0
    lo, hi, ind_a, ind_b = _pair_constants()
    tok = lambda w: pl.BlockSpec((1, tile, w), lambda i, j: (i, j, 0))
    seq = lambda w: pl.BlockSpec((1, s, w), lambda i, j: (i, 0, 0))
    const = lambda shp: pl.BlockSpec(shp, lambda i, j: (0,) * len(shp))
    return pl.pallas_call(
        functools.partial(_attn_prompt_kernel, tile=tile),
        out_shape=(jax.ShapeDtypeStruct((b, s, QA_W), BF16), jax.ShapeDtypeStruct((b, s, B_W), BF16)),
        grid=(b, s // tile),
        in_specs=[tok(QA_W), seq(KVA_W), seq(KVA_W), tok(B_W), seq(B_W), seq(B_W),
                  const(bias2_a.shape), const(sinks2.shape), const(bias2_b.shape), const(lo.shape), const(hi.shape),
                  const(ind_a.shape), const(ind_b.shape)],
        out_specs=(tok(QA_W), tok(B_W)),
        scratch_shapes=[pltpu.VMEM((4, s, PAIR_W), BF16), pltpu.VMEM((4, s, PAIR_W), BF16),
                        pltpu.VMEM((2, s, B_W), BF16), pltpu.VMEM((2, s, B_W), BF16)],
        compiler_params=_cparams("arbitrary", "arbitrary"),
        name="attn_prompt",
    )(qa, ka, va, qb, kb, vb, bias2_a, sinks2, bias2_b, lo, hi, ind_a, ind_b)


def _attn_sample_kernel(qa_ref, ka_ref, va_ref, qb_ref, kb_ref, vb_ref, ba_ref, sk_ref, bb_ref, lo_ref, hi_ref,
                        ia_ref, ib_ref, oa_ref, ob_ref):
    second = lambda tbl: tbl[CHUNK:ATT_BLOCK]
    pad = lambda x: jnp.concatenate([jnp.zeros((CHUNK, PAIR_W), BF16), x], axis=0)
    lo, hi = lo_ref[:, 0:PAIR_W], hi_ref[:, 0:PAIR_W]

    def swap(x):
        return jnp.concatenate([x[:, HEAD_DIM:], x[:, :HEAD_DIM]], axis=1)

    ka, va = ka_ref[0], va_ref[0]
    for kv in range(A_KV_HEADS):
        c0, c1 = slice(2 * kv * PAIR_W, (2 * kv + 1) * PAIR_W), slice((2 * kv + 1) * PAIR_W, (2 * kv + 2) * PAIR_W)
        q2 = jnp.concatenate([qa_ref[0, :, c0], qa_ref[0, :, c1]], axis=0)
        first = (lambda x: x * lo) if kv == 0 else (lambda x: swap(x) * lo)
        last = (lambda x: swap(x) * hi) if kv == 0 else (lambda x: x * hi)
        k2 = jnp.concatenate([pad(first(ka)), pad(last(ka))], axis=0)
        v2 = jnp.concatenate([pad(first(va)), pad(last(va))], axis=0)
        two = lambda tbl: jnp.concatenate([second(tbl[0:ATT_BLOCK]), second(tbl[ATT_BLOCK:])], axis=0)
        o = _pair_attend(q2, k2, v2, ia_ref[...], two(ba_ref[kv]), (two(sk_ref[kv, 0]), two(sk_ref[kv, 1])))
        oa_ref[0, :, c0] = o[0:CHUNK].astype(BF16)
        oa_ref[0, :, c1] = o[CHUNK:].astype(BF16)
    for p in range(B_HEADS // 2):
        cols = slice(p * PAIR_W, (p + 1) * PAIR_W)
        kb, vb = kb_ref[0, :, cols], vb_ref[0, :, cols]
        k2 = jnp.concatenate([pad(kb * lo), pad(kb * hi)], axis=0)
        v2 = jnp.concatenate([pad(vb * lo), pad(vb * hi)], axis=0)
        o = _pair_attend(qb_ref[0, :, cols], k2, v2, ib_ref[...], second(bb_ref[p]), None)
        ob_ref[0, :, cols] = o.astype(BF16)


def _attn_sample(qa, ka, va, qb, kb, vb, bias2_a, sinks2, bias2_b):
    b, t, _ = qa.shape
    assert t == CHUNK
    lo, hi, ind_a, ind_b = _pair_constants()
    per_b = lambda r, w: pl.BlockSpec((1, r, w), lambda i: (i, 0, 0))
    const = lambda shp: pl.BlockSpec(shp, lambda i: (0,) * len(shp))
    return pl.pallas_call(
        _attn_sample_kernel,
        out_shape=(jax.ShapeDtypeStruct((b, t, QA_W), BF16), jax.ShapeDtypeStruct((b, t, B_W), BF16)),
        grid=(b,),
        in_specs=[per_b(t, QA_W), per_b(A_BAND, KVA_W), per_b(A_BAND, KVA_W),
                  per_b(t, B_W), per_b(B_BAND, B_W), per_b(B_BAND, B_W),
                  const(bias2_a.shape), const(sinks2.shape), const(bias2_b.shape), const(lo.shape), const(hi.shape),
                  const(ind_a.shape), const(ind_b.shape)],
        out_specs=(per_b(t, QA_W), per_b(t, B_W)),
        compiler_params=_cparams("parallel"),
        name="attn_sample",
    )(qa, ka, va, qb, kb, vb, bias2_a, sinks2, bias2_b, lo, hi, ind_a, ind_b)


def _route(lt):
    t = lt.shape[1]
    el = lt[0:N_EXPERTS]
    gl = lt[N_EXPERTS:N_EXPERTS + N_GROUPS]
    gmax = jnp.max(gl, axis=0, keepdims=True)
    gi = lax.broadcasted_iota(jnp.int32, (N_GROUPS, t), 0)
    gidx = jnp.min(jnp.where(gl == gmax, gi, N_GROUPS), axis=0, keepdims=True)
    g_w = 1.0 / jnp.sum(jnp.exp(gl - gmax), axis=0, keepdims=True)
    e_sel = el[(N_GROUPS - 1) * EXPERTS_PER_GROUP:]
    for g in range(N_GROUPS - 2, -1, -1):
        e_sel = jnp.where(gidx == g, el[g * EXPERTS_PER_GROUP:(g + 1) * EXPERTS_PER_GROUP], e_sel)
    ei = lax.broadcasted_iota(jnp.int32, (EXPERTS_PER_GROUP, t), 0)
    m1 = jnp.max(e_sel, axis=0, keepdims=True)
    i1 = jnp.min(jnp.where(e_sel == m1, ei, EXPERTS_PER_GROUP), axis=0, keepdims=True)
    rest = jnp.where(ei == i1, -jnp.inf, e_sel)
    m2 = jnp.max(rest, axis=0, keepdims=True)
    i2 = jnp.min(jnp.where(rest == m2, ei, EXPERTS_PER_GROUP), axis=0, keepdims=True)
    ex = jnp.exp(m2 - m1)
    den = 1.0 + ex
    w1 = g_w * (1.0 / den)
    w2 = g_w * (ex / den)
    lo = jnp.minimum(i1, i2)
    hi = jnp.maximum(i1, i2)
    first_is_lo = i1 < i2
    w_lo = jnp.where(first_is_lo, w1, w2)
    w_hi = jnp.where(first_is_lo, w2, w1)
    pair = ((lo * (2 * EXPERTS_PER_GROUP - 1 - lo)) >> 1) + (hi - lo - 1)
    return gidx * PAIRS_PER_GROUP + pair, w_lo, w_hi


def _route_logits(wr_ref, br_ref, h2):
    return lax.dot_general(wr_ref[...], h2, (((1,), (1,)), ((), ())), preferred_element_type=F32) + br_ref[...]


def _weight_tile(w_lo, w_hi):
    t = w_lo.shape[1]
    return jnp.transpose(jnp.concatenate([w_lo, w_hi, jnp.zeros((LANES - 2, t), F32)], axis=0))


def _rank_in_class(cls, run_ref, tri_ref):
    t = cls.shape[1]
    onehot = lax.broadcasted_iota(jnp.int32, (CLASS_ROWS, t), 0) == cls
    ones = jnp.where(onehot, 1.0, 0.0)
    before = jnp.dot(ones.astype(BF16), tri_ref[...], preferred_element_type=F32)
    run = run_ref[...]
    rank = jnp.sum(jnp.where(onehot, before + run, 0.0), axis=0, keepdims=True)
    run_ref[...] = run + jnp.sum(ones, axis=1, keepdims=True)
    return rank.astype(jnp.int32)


def _post_kernel(x_ref, oa_ref, ob_ref, mod_ref, g1_ref, g1p_ref, g2_ref,
                 wg_ref, bg_ref, wpa_ref, wpb_ref, wo_ref, wr_ref, br_ref, tri_ref,
                 x1_ref, h2_ref, cls_ref, rank_ref, cnt_ref, run_ref):
    @pl.when((pl.program_id(0) == 0) & (pl.program_id(1) == 0))
    def _():
        run_ref[...] = jnp.zeros_like(run_ref)

    nb, rows, _ = x_ref.shape
    stack = lambda parts: parts[0] if nb == 1 else jnp.concatenate(parts, axis=0)
    xs = [x_ref[g] for g in range(nb)]
    h = stack([_norm_mod(xs[g], g1_ref[...], mod_ref[g, 1:2, :], mod_ref[g, 0:1, :]) for g in range(nb)]).astype(BF16)
    gates = jax.nn.sigmoid(jnp.dot(h, wg_ref[...], preferred_element_type=F32) + bg_ref[...])
    pa = jnp.dot(stack([oa_ref[g] for g in range(nb)]), wpa_ref[...], preferred_element_type=F32)
    pb = jnp.dot(stack([ob_ref[g] for g in range(nb)]), wpb_ref[...], preferred_element_type=F32)
    mixed = gates[:, :D_MODEL] * pa + gates[:, D_MODEL:] * pb
    y = jnp.dot(mixed.astype(BF16), wo_ref[...], preferred_element_type=F32)
    h2_parts = []
    for g in range(nb):
        x1 = xs[g] + mod_ref[g, 2:3, :] * _rms(y[g * rows:(g + 1) * rows], g1p_ref[...])
        x1_ref[g] = x1
        h2_parts.append(_norm_mod(x1, g2_ref[...], mod_ref[g, 4:5, :], mod_ref[g, 3:4, :]))
    h2 = stack(h2_parts).astype(BF16)
    cls, w_lo, w_hi = _route(_route_logits(wr_ref, br_ref, h2))
    cls_ref[0] = cls
    rank_ref[0] = _rank_in_class(cls, run_ref, tri_ref)
    cnt_ref[...] = run_ref[...]
    t = nb * rows
    h2f = h2.astype(F32)
    for c in range(D_MODEL // LANES):
        h2_ref[pl.ds(c, t, stride=IN_SLAB), :] = h2f[:, c * LANES:(c + 1) * LANES]
    h2_ref[pl.ds(D_MODEL // LANES, t, stride=IN_SLAB), :] = _weight_tile(w_lo, w_hi)


def _post(x, oa, ob, mods, g_pre_mix, g_post_mix, g_pre_ffn, wg, bg, wpa, wpb, wo, wr, br, tile):
    b, s, d = x.shape
    rows = min(s, tile)
    nb = math.gcd(tile // rows, b)
    tile = nb * rows
    assert s % rows == 0
    nt = s // rows
    n_steps = (b // nb) * nt
    tri = (jnp.arange(tile)[:, None] < jnp.arange(tile)[None, :]).astype(BF16)
    tok = lambda w: pl.BlockSpec((nb, rows, w), lambda i, j: (i, j, 0))
    const = lambda shp: pl.BlockSpec(shp, lambda i, j: (0,) * len(shp))
    per_tile = pl.BlockSpec((1, 1, tile), lambda i, j: (i * nt + j, 0, 0))
    return pl.pallas_call(
        _post_kernel,
        out_shape=(
            jax.ShapeDtypeStruct((b, s, d), F32),
            jax.ShapeDtypeStruct((b * s * IN_SLAB, LANES), F32),
            jax.ShapeDtypeStruct((n_steps, 1, tile), jnp.int32),
            jax.ShapeDtypeStruct((n_steps, 1, tile), jnp.int32),
            jax.ShapeDtypeStruct((CLASS_ROWS, 1), F32),
        ),
        grid=(b // nb, nt),
        in_specs=[
            tok(d), tok(QA_W), tok(B_W),
            pl.BlockSpec((nb, 6, d), lambda i, j: (i, 0, 0)),
            const((1, d)), const((1, d)), const((1, d)),
            const(wg.shape), const(bg.shape), const(wpa.shape), const(wpb.shape), const(wo.shape),
            const(wr.shape), const(br.shape), const(tri.shape),
        ],
        out_specs=(tok(d), pl.BlockSpec((tile * IN_SLAB, LANES), lambda i, j: (i * nt + j, 0)),
                   per_tile, per_tile, const((CLASS_ROWS, 1))),
        scratch_shapes=[pltpu.VMEM((CLASS_ROWS, 1), F32)],
        compiler_params=_cparams("arbitrary", "arbitrary"),
        name="post",
    )(x, oa, ob, mods, g_pre_mix, g_post_mix, g_pre_ffn, wg, bg, wpa, wpb, wo, wr, br, tri)


def _invert_kernel(upos_ref, row_ref, *, n, scale):
    def body(i, carry):
        for u in range(INVERT_UNROLL):
            t = i * INVERT_UNROLL + u
            row_ref[upos_ref[t]] = t * scale
        return carry
    lax.fori_loop(0, n // INVERT_UNROLL, body, 0)


def _invert(upos, scale):
    n = upos.shape[0]
    return pl.pallas_call(
        functools.partial(_invert_kernel, n=n, scale=scale),
        out_shape=jax.ShapeDtypeStruct((n,), jnp.int32),
        in_specs=[pl.BlockSpec(memory_space=pltpu.SMEM)],
        out_specs=pl.BlockSpec(memory_space=pltpu.SMEM),
        name="invert",
    )(upos)


def _expert(xb, wgu_ref, wd_ref):
    gu = jnp.dot(xb, wgu_ref[0], preferred_element_type=F32)
    gate = gu[:, :D_EXPERT]
    he = (gate * jax.nn.sigmoid(gate)) * gu[:, D_EXPERT:]
    return jnp.dot(he.astype(BF16), wd_ref[0], preferred_element_type=F32)


def _moe_kernel(tok_ref, base_ref, ea_ref, eb_ref, nu_ref, h_ref, wgu_a_ref, wd_a_ref, wgu_b_ref, wd_b_ref,
                ys_ref, xbuf, sems, *, tile, n_tok):
    t = pl.program_id(0)
    nu = nu_ref[0]
    slot = t % 2

    def slab_copy(src_row0, dst_slot, r):
        return pltpu.make_async_copy(h_ref.at[pl.ds(src_row0, IN_SLAB)],
                                     xbuf.at[dst_slot, pl.ds(r * IN_SLAB, IN_SLAB)], sems.at[dst_slot])

    def fetch(step, dst_slot):
        base = base_ref[step]
        for r in range(tile):
            slab_copy(tok_ref[jnp.minimum(base + r, n_tok - 1)], dst_slot, r).start(priority=r % 2)

    def wait(dst_slot):
        for r in range(tile):
            slab_copy(0, dst_slot, 0).wait()

    @pl.when(t == 0)
    def _():
        fetch(0, 0)

    @pl.when(t < nu)
    def _():
        wait(slot)
        fetch(jnp.minimum(t + 1, nu - 1), 1 - slot)
        lane_tile = lambda c: xbuf[slot, pl.ds(c, tile, stride=IN_SLAB), :]
        xb = jnp.concatenate([lane_tile(c) for c in range(D_MODEL // LANES)], axis=1).astype(BF16)
        w = lane_tile(D_MODEL // LANES)
        y = w[:, 0:1] * _expert(xb, wgu_a_ref, wd_a_ref) + w[:, 1:2] * _expert(xb, wgu_b_ref, wd_b_ref)
        for c in range(OUT_SLAB):
            ys_ref[pl.ds(c, tile, stride=OUT_SLAB), :] = y[:, c * LANES:(c + 1) * LANES]

    @pl.when(t == nu - 1)
    def _():
        wait(1 - slot)

    @pl.when(t >= nu)
    def _():
        ys_ref[...] = jnp.zeros_like(ys_ref)


def _moe_grouped(h_slabs, sorted_tok, tile_base, tile_ea, tile_eb, n_used, wgu, wd, tile, n_tiles):
    d = D_MODEL
    n_tok = sorted_tok.shape[0]
    expert = lambda which, shp: pl.BlockSpec(shp, lambda t, tok, base, ea, eb, nu: ((ea, eb)[which][t], 0, 0))
    return pl.pallas_call(
        functools.partial(_moe_kernel, tile=tile, n_tok=n_tok),
        out_shape=jax.ShapeDtypeStruct((n_tiles * tile * OUT_SLAB, LANES), F32),
        grid_spec=pltpu.PrefetchScalarGridSpec(
            num_scalar_prefetch=5,
            grid=(n_tiles,),
            in_specs=[
                pl.BlockSpec(memory_space=pl.ANY),
                expert(0, (1, d, 2 * D_EXPERT)), expert(0, (1, D_EXPERT, d)),
                expert(1, (1, d, 2 * D_EXPERT)), expert(1, (1, D_EXPERT, d)),
            ],
            out_specs=pl.BlockSpec((tile * OUT_SLAB, LANES), lambda t, *_: (t, 0)),
            scratch_shapes=[pltpu.VMEM((2, tile * IN_SLAB, LANES), F32), pltpu.SemaphoreType.DMA((2,))],
        ),
        compiler_params=_cparams("arbitrary"),
        name="moe_grouped",
    )(sorted_tok, tile_base, tile_ea, tile_eb, n_used, h_slabs, wgu, wd, wgu, wd)


_PAIR_LO, _PAIR_HI = np.triu_indices(EXPERTS_PER_GROUP, k=1)
_CLASS_LO = np.concatenate([g * EXPERTS_PER_GROUP + _PAIR_LO for g in range(N_GROUPS)]).astype(np.int32)
_CLASS_HI = np.concatenate([g * EXPERTS_PER_GROUP + _PAIR_HI for g in range(N_GROUPS)]).astype(np.int32)


def _moe_plan(counts, cls, rank, n, tile):
    p_max = -(-(n + N_CLASSES * (tile - 1)) // tile) * tile
    nt = p_max // tile
    counts = counts[:N_CLASSES].astype(jnp.int32)
    padded = ((counts + tile - 1) // tile) * tile
    pad_end = jnp.cumsum(padded)
    gstart = pad_end - padded
    ustart = jnp.cumsum(counts) - counts
    onehot = cls[:, None] == jnp.arange(N_CLASSES, dtype=jnp.int32)[None, :]
    pos = rank + jnp.sum(jnp.where(onehot, gstart[None, :], 0), axis=1)
    upos = rank + jnp.sum(jnp.where(onehot, ustart[None, :], 0), axis=1)
    n_used = (pad_end[-1] // tile).astype(jnp.int32).reshape(1)
    tiles = jnp.arange(nt, dtype=jnp.int32)
    tile_cls = jnp.minimum(jnp.sum((pad_end[None, :] <= (tiles * tile)[:, None]).astype(jnp.int32), axis=1), N_CLASSES - 1)
    tile_ea = jnp.asarray(_CLASS_LO)[tile_cls]
    tile_eb = jnp.asarray(_CLASS_HI)[tile_cls]
    tile_base = jnp.clip(ustart[tile_cls] + tiles * tile - gstart[tile_cls], 0, n - 1)
    return nt, pos, upos, tile_base, tile_ea, tile_eb, n_used


def _final_kernel(pos_ref, x1_ref, ys_ref, mod_ref, g_ref, o_ref, ybuf, sems, *, tile, n_steps):
    s = pl.program_id(0)
    slot = s % 2

    def slab_copy(src_row0, dst_slot, dst_row):
        return pltpu.make_async_copy(ys_ref.at[pl.ds(src_row0, OUT_SLAB)],
                                     ybuf.at[dst_slot, pl.ds(dst_row * OUT_SLAB, OUT_SLAB)], sems.at[dst_slot])

    def fetch(step, dst_slot):
        for r in range(tile):
            slab_copy(pos_ref[step * tile + r], dst_slot, r).start(priority=r % 2)

    def wait(dst_slot):
        for r in range(tile):
            slab_copy(0, dst_slot, 0).wait()

    @pl.when(s == 0)
    def _():
        fetch(0, 0)

    wait(slot)
    fetch(jnp.minimum(s + 1, n_steps - 1), 1 - slot)
    y = jnp.concatenate([ybuf[slot, pl.ds(c, tile, stride=OUT_SLAB), :] for c in range(OUT_SLAB)], axis=1)
    o_ref[...] = x1_ref[...] + mod_ref[0, 5:6, :] * _rms(y, g_ref[...])

    @pl.when(s == n_steps - 1)
    def _():
        wait(1 - slot)


def _final(x1, ys, pos, mods, g_post_ffn, tile):
    b, s, d = x1.shape
    n = b * s
    per_b = s // tile
    n_steps = n // tile
    tok = pl.BlockSpec((tile, d), lambda i, *_: (i, 0))
    out = pl.pallas_call(
        functools.partial(_final_kernel, tile=tile, n_steps=n_steps),
        out_shape=jax.ShapeDtypeStruct((n, d), F32),
        grid_spec=pltpu.PrefetchScalarGridSpec(
            num_scalar_prefetch=1,
            grid=(n_steps,),
            in_specs=[
                tok,
                pl.BlockSpec(memory_space=pl.ANY),
                pl.BlockSpec((1, 6, d), lambda i, *_: (i // per_b, 0, 0)),
                pl.BlockSpec((1, d), lambda i, *_: (0, 0)),
            ],
            out_specs=tok,
            scratch_shapes=[pltpu.VMEM((2, tile * OUT_SLAB, LANES), F32), pltpu.SemaphoreType.DMA((2,))],
        ),
        compiler_params=_cparams("arbitrary"),
        name="final",
    )(pos, x1.reshape(n, d), ys, mods, g_post_ffn)
    return out.reshape(b, s, d)


def _t5_bucket(rel):
    half = T5_BUCKETS // 2
    exact = half // 2
    ret = jnp.where(rel > 0, half, 0)
    n = jnp.abs(rel)
    nf = jnp.maximum(n, 1).astype(F32)
    large = exact + (jnp.log(nf / exact) / math.log(T5_MAX_DIST / exact) * (half - exact)).astype(jnp.int32)
    large = jnp.minimum(large, half - 1)
    return ret + jnp.where(n < exact, n, large)


def _toeplitz(u, n_rows, n_cols):
    lead, p = u.shape[:-1], n_rows + n_cols
    w = jnp.concatenate([u, jnp.zeros(lead + (1,), u.dtype)], axis=-1)
    periodic = jnp.broadcast_to(w[..., None, :], lead + (n_rows + 1, p)).reshape(lead + ((n_rows + 1) * p,))
    shifted = periodic[..., :n_rows * (p + 1)].reshape(lead + (n_rows, p + 1))[..., :n_cols]
    return shifted[..., ::-1, :]


def _bias_tables(t5_table, rel_table):
    ja = jnp.arange(CHUNK - 1 + A_BAND)
    ua = t5_table[_t5_bucket(ja - (CHUNK - 1) - A_WINDOW)].T.astype(F32)
    jb = jnp.arange(CHUNK - 1 + B_BAND)
    ub = rel_table[:, jnp.clip((CHUNK - 1) - jb + B_REACH, -REL_CLIP, REL_CLIP) + REL_CLIP].astype(F32)
    return _toeplitz(ua, CHUNK, A_BAND), _toeplitz(ub, CHUNK, B_BAND)


def _two_chunk(bias):
    masked = jnp.full(bias.shape[:-1] + (CHUNK,), MASKED, F32)
    return jnp.concatenate([jnp.concatenate([bias, masked], axis=-1), jnp.concatenate([masked, bias], axis=-1)], axis=-2)


def _pair_tables(bias_a, bias_b, sinks):
    a2, b2 = _two_chunk(bias_a), _two_chunk(bias_b)
    stack2 = lambda x, h0, h1: jnp.concatenate([x[h0], x[h1]], axis=0)
    bias2_a = jnp.stack([jnp.concatenate([stack2(a2, 4 * kv, 4 * kv + 2), stack2(a2, 4 * kv + 1, 4 * kv + 3)], axis=1)
                         for kv in range(A_KV_HEADS)])
    bias2_b = jnp.concatenate([b2[0::2], b2[1::2]], axis=-1)
    sk = jnp.broadcast_to(sinks.astype(F32).reshape(A_Q_HEADS, 1, 1), (A_Q_HEADS, ATT_BLOCK, 1))
    sinks2 = jnp.stack([jnp.stack([stack2(sk, 4 * kv, 4 * kv + 2), stack2(sk, 4 * kv + 1, 4 * kv + 3)])
                        for kv in range(A_KV_HEADS)])
    return bias2_a, sinks2, bias2_b


def _moe_and_final(x1, h2e, cls, rank, counts, mods, g_post_ffn, wgu, wd, seq_tile, moe_tile):
    b, s, d = x1.shape
    n = b * s
    n_tiles, pos, upos, tile_base, tile_ea, tile_eb, n_used = _moe_plan(
        counts.reshape(CLASS_ROWS), cls.reshape(n), rank.reshape(n), n, moe_tile)
    ys = _moe_grouped(h2e, _invert(upos, IN_SLAB), tile_base, tile_ea, tile_eb, n_used, wgu, wd, moe_tile, n_tiles)
    return _final(x1, ys, pos * OUT_SLAB, mods, g_post_ffn, seq_tile)


def kernel(x_prompt, x_sample, c_prompt, c_sample, cache_a_k, cache_a_v, cache_b_k, cache_b_v, w_ada, b_ada, g_pre_mix, g_post_mix, g_pre_ffn, g_post_ffn, w_in, a_sinks, t5_table, b_rel_table, w_proj_a, w_proj_b, w_gate, b_gate, w_o, w_route_g, b_route_g, w_route_e, b_route_e, w_e_gate, w_e_up, w_e_down):
    depth = w_in.shape[0]
    assert depth == 1
    l = 0
    bp, sp, d = x_prompt.shape
    bs, ts, _ = x_sample.shape

    mods = _ada(jnp.concatenate([c_prompt, c_sample], axis=0), w_ada[l], b_ada[l]).reshape(bp + bs, 6, d)
    mods_p, mods_s = mods[:bp], mods[bp:]

    w_in_bf = w_in[l].astype(BF16)
    wg, wpa, wpb, wo = w_gate[l].astype(BF16), w_proj_a[l].astype(BF16), w_proj_b[l].astype(BF16), w_o[l].astype(BF16)
    bg = b_gate[l].reshape(1, 2 * d)
    pad_rows = ROUTE_ROWS - N_EXPERTS - N_GROUPS
    wr = jnp.concatenate([w_route_e[l].T, w_route_g[l].T, jnp.zeros((pad_rows, d), F32)], axis=0).astype(BF16)
    br = jnp.concatenate([b_route_e[l], b_route_g[l], jnp.zeros((pad_rows,), F32)]).reshape(ROUTE_ROWS, 1)
    wgu = jnp.concatenate([w_e_gate[l], w_e_up[l]], axis=-1).astype(BF16)
    wd = w_e_down[l].astype(BF16)
    g1, g1p, g2, g2p = (g[l].reshape(1, d) for g in (g_pre_mix, g_post_mix, g_pre_ffn, g_post_ffn))

    bias_a, bias_b = _bias_tables(t5_table, b_rel_table[l])
    bias2_a, sinks2, bias2_b = _pair_tables(bias_a, bias_b, a_sinks[l])

    qa, ka, va, qb, kb, vb, sak_p, sav_p, sbk_p, sbv_p = _pre_prompt(x_prompt, mods_p, g1, w_in_bf)
    oa, ob = _attn_prompt(qa, ka, va, qb, kb, vb, bias2_a, sinks2, bias2_b)
    x1, h2e, cls, rank, counts = _post(x_prompt, oa, ob, mods_p, g1, g1p, g2, wg, bg, wpa, wpb, wo, wr, br, SEQ_TILE)
    y_prompt = _moe_and_final(x1, h2e, cls, rank, counts, mods_p, g2p, wgu, wd, SEQ_TILE, MOE_TILE_PROMPT)

    la, lb = cache_a_k.shape[2], cache_b_k.shape[2]
    cak = cache_a_k[l].reshape(bs, la, KVA_W)
    cav = cache_a_v[l].reshape(bs, la, KVA_W)
    cbk = cache_b_k[l].reshape(bs, lb, B_W)
    cbv = cache_b_v[l].reshape(bs, lb, B_W)
    qa, ka, va, qb, kb, vb, sak_s, sav_s, sbk_s, sbv_s = _pre_sample(x_sample, mods_s, g1, w_in_bf, cak, cav, cbk, cbv)
    oa, ob = _attn_sample(qa, ka, va, qb, kb, vb, bias2_a, sinks2, bias2_b)
    x1, h2e, cls, rank, counts = _post(x_sample, oa, ob, mods_s, g1, g1p, g2, wg, bg, wpa, wpb, wo, wr, br, SEQ_TILE)
    y_sample = _moe_and_final(x1, h2e, cls, rank, counts, mods_s, g2p, wgu, wd, ts, MOE_TILE_SAMPLE)

    a_state = lambda v, b, r: v.reshape(1, b, r, A_KV_HEADS, HEAD_DIM)
    b_state = lambda v, b, r: v.reshape(1, b, r, B_HEADS, HEAD_DIM)
    return (y_prompt, y_sample,
            a_state(sak_p, bp, A_WINDOW), a_state(sav_p, bp, A_WINDOW),
            b_state(sbk_p, bp, B_REACH), b_state(sbv_p, bp, B_REACH),
            a_state(sak_s, bs, la), a_state(sav_s, bs, la),
            b_state(sbk_s, bs, lb), b_state(sbv_s, bs, lb))
```

```python
import functools
import math

import numpy as np
import jax
import jax.numpy as jnp
from jax import lax
from jax.experimental import pallas as pl
from jax.experimental.pallas import tpu as pltpu

D_MODEL = 1024
CHUNK = 64
HEAD_DIM = 64
A_Q_HEADS = 8
A_KV_HEADS = 2
A_WINDOW = 128
B_HEADS = 8
B_REACH = 8 * CHUNK
REL_CLIP = 256
T5_BUCKETS = 32
T5_MAX_DIST = 128
N_GROUPS = 4
EXPERTS_PER_GROUP = 8
N_EXPERTS = N_GROUPS * EXPERTS_PER_GROUP
D_EXPERT = D_MODEL // 4
EPS = 1e-6

QA_W = A_Q_HEADS * HEAD_DIM
KVA_W = A_KV_HEADS * HEAD_DIM
B_W = B_HEADS * HEAD_DIM
IN_W = QA_W + 2 * KVA_W + 3 * B_W
A_BAND = A_WINDOW + CHUNK
B_BAND = B_REACH + CHUNK

PAIRS_PER_GROUP = EXPERTS_PER_GROUP * (EXPERTS_PER_GROUP - 1) // 2
N_CLASSES = N_GROUPS * PAIRS_PER_GROUP
ROUTE_ROWS = 40
CLASS_ROWS = 128
LANES = 128
IN_SLAB = D_MODEL // LANES + 1
OUT_SLAB = D_MODEL // LANES

F32 = jnp.float32
BF16 = jnp.bfloat16

VMEM_LIMIT_BYTES = 56 * 1024 * 1024

SEQ_TILE = 512
ATT_TILE = 512
MOE_TILE_PROMPT = 256
MOE_TILE_SAMPLE = 32
INVERT_UNROLL = 32


def _cparams(*sem):
    return pltpu.CompilerParams(dimension_semantics=sem, vmem_limit_bytes=VMEM_LIMIT_BYTES)


def _norm_mod(x, g, scale, shift):
    y = x * lax.rsqrt(jnp.mean(x * x, axis=-1, keepdims=True) + EPS)
    return (y * g) * (1.0 + scale) + shift


def _rms(x, g):
    return (x * lax.rsqrt(jnp.mean(x * x, axis=-1, keepdims=True) + EPS)) * g


def _ada_kernel(c_ref, w_ref, b_ref, o_ref):
    c = c_ref[...]
    s = (c * jax.nn.sigmoid(c)).astype(BF16)
    o_ref[...] = jnp.dot(s, w_ref[...].astype(BF16), preferred_element_type=F32) + b_ref[...]


def _ada(c, w_ada, b_ada):
    n, d = c.shape
    wn = w_ada.shape[1]
    tn = 512
    return pl.pallas_call(
        _ada_kernel,
        out_shape=jax.ShapeDtypeStruct((n, wn), F32),
        grid=(wn // tn,),
        in_specs=[
            pl.BlockSpec((n, d), lambda j: (0, 0)),
            pl.BlockSpec((d, tn), lambda j: (0, j)),
            pl.BlockSpec((1, tn), lambda j: (0, j)),
        ],
        out_specs=pl.BlockSpec((n, tn), lambda j: (0, j)),
        compiler_params=_cparams("arbitrary"),
        name="ada",
    )(c, w_ada, b_ada.reshape(1, wn))


_COL_QA = (0, QA_W)
_COL_KA = (QA_W, QA_W + KVA_W)
_COL_VA = (QA_W + KVA_W, QA_W + 2 * KVA_W)
_COL_QB = (QA_W + 2 * KVA_W, QA_W + 2 * KVA_W + B_W)
_COL_KB = (_COL_QB[1], _COL_QB[1] + B_W)
_COL_VB = (_COL_KB[1], _COL_KB[1] + B_W)
Q_SCALE = HEAD_DIM ** -0.5


def _project(x_ref, mod_ref, g_ref, w_ref):
    h = _norm_mod(x_ref[0], g_ref[...], mod_ref[0, 1:2, :], mod_ref[0, 0:1, :])
    return jnp.dot(h.astype(BF16), w_ref[...], preferred_element_type=F32)


def _cols(p, c):
    return p[:, c[0]:c[1]]


def _pre_prompt_kernel(x_ref, mod_ref, g_ref, w_ref,
                       qa_ref, ka_ref, va_ref, qb_ref, kb_ref, vb_ref,
                       sak_ref, sav_ref, sbk_ref, sbv_ref, *, n_tiles, tile):
    p = _project(x_ref, mod_ref, g_ref, w_ref)
    qa_ref[0] = (_cols(p, _COL_QA) * Q_SCALE).astype(BF16)
    ka_ref[0] = _cols(p, _COL_KA).astype(BF16)
    va_ref[0] = _cols(p, _COL_VA).astype(BF16)
    qb_ref[0] = (_cols(p, _COL_QB) * Q_SCALE).astype(BF16)
    kb_ref[0] = _cols(p, _COL_KB).astype(BF16)
    vb_ref[0] = _cols(p, _COL_VB).astype(BF16)

    @pl.when(pl.program_id(1) == n_tiles - 1)
    def _():
        sak_ref[0] = _cols(p, _COL_KA)[tile - A_WINDOW:, :]
        sav_ref[0] = _cols(p, _COL_VA)[tile - A_WINDOW:, :]
        sbk_ref[0] = _cols(p, _COL_KB)[tile - B_REACH:, :]
        sbv_ref[0] = _cols(p, _COL_VB)[tile - B_REACH:, :]


def _pre_prompt(x, mods, g_pre, w_in_bf):
    b, s, d = x.shape
    tile = SEQ_TILE
    assert s % tile == 0 and tile >= B_REACH and s >= B_REACH
    nt = s // tile
    tok = lambda w: pl.BlockSpec((1, tile, w), lambda i, j: (i, j, 0))
    state = lambda r, w: pl.BlockSpec((1, r, w), lambda i, j: (i, 0, 0))
    return pl.pallas_call(
        functools.partial(_pre_prompt_kernel, n_tiles=nt, tile=tile),
        out_shape=(
            jax.ShapeDtypeStruct((b, s, QA_W), BF16),
            jax.ShapeDtypeStruct((b, s, KVA_W), BF16),
            jax.ShapeDtypeStruct((b, s, KVA_W), BF16),
            jax.ShapeDtypeStruct((b, s, B_W), BF16),
            jax.ShapeDtypeStruct((b, s, B_W), BF16),
            jax.ShapeDtypeStruct((b, s, B_W), BF16),
            jax.ShapeDtypeStruct((b, A_WINDOW, KVA_W), F32),
            jax.ShapeDtypeStruct((b, A_WINDOW, KVA_W), F32),
            jax.ShapeDtypeStruct((b, B_REACH, B_W), F32),
            jax.ShapeDtypeStruct((b, B_REACH, B_W), F32),
        ),
        grid=(b, nt),
        in_specs=[
            tok(d),
            pl.BlockSpec((1, 6, d), lambda i, j: (i, 0, 0)),
            pl.BlockSpec((1, d), lambda i, j: (0, 0)),
            pl.BlockSpec((d, IN_W), lambda i, j: (0, 0)),
        ],
        out_specs=(
            tok(QA_W), tok(KVA_W), tok(KVA_W), tok(B_W), tok(B_W), tok(B_W),
            state(A_WINDOW, KVA_W), state(A_WINDOW, KVA_W), state(B_REACH, B_W), state(B_REACH, B_W),
        ),
        compiler_params=_cparams("parallel", "arbitrary"),
        name="pre_prompt",
    )(x, mods, g_pre, w_in_bf)


def _pre_sample_kernel(x_ref, mod_ref, g_ref, w_ref, cak_ref, cav_ref, cbk_ref, cbv_ref,
                       qa_ref, ka_ref, va_ref, qb_ref, kb_ref, vb_ref,
                       sak_ref, sav_ref, sbk_ref, sbv_ref, *, t, la, lb):
    p = _project(x_ref, mod_ref, g_ref, w_ref)
    qa_ref[0] = (_cols(p, _COL_QA) * Q_SCALE).astype(BF16)
    qb_ref[0] = (_cols(p, _COL_QB) * Q_SCALE).astype(BF16)
    for cache_ref, band_ref, state_ref, col, hist in (
            (cak_ref, ka_ref, sak_ref, _COL_KA, la), (cav_ref, va_ref, sav_ref, _COL_VA, la),
            (cbk_ref, kb_ref, sbk_ref, _COL_KB, lb), (cbv_ref, vb_ref, sbv_ref, _COL_VB, lb)):
        new = _cols(p, col)
        cache = cache_ref[0]
        band_ref[0, 0:hist, :] = cache.astype(BF16)
        band_ref[0, hist:hist + t, :] = new.astype(BF16)
        state_ref[0, 0:hist - t, :] = cache[t:, :]
        state_ref[0, hist - t:hist, :] = new


def _pre_sample(x, mods, g_pre, w_in_bf, cak, cav, cbk, cbv):
    b, t, d = x.shape
    la, lb = cak.shape[1], cbk.shape[1]
    assert t == CHUNK and la == A_WINDOW and lb == B_REACH
    per_b = lambda r, w: pl.BlockSpec((1, r, w), lambda i: (i, 0, 0))
    return pl.pallas_call(
        functools.partial(_pre_sample_kernel, t=t, la=la, lb=lb),
        out_shape=(
            jax.ShapeDtypeStruct((b, t, QA_W), BF16),
            jax.ShapeDtypeStruct((b, la + t, KVA_W), BF16),
            jax.ShapeDtypeStruct((b, la + t, KVA_W), BF16),
            jax.ShapeDtypeStruct((b, t, B_W), BF16),
            jax.ShapeDtypeStruct((b, lb + t, B_W), BF16),
            jax.ShapeDtypeStruct((b, lb + t, B_W), BF16),
            jax.ShapeDtypeStruct((b, la, KVA_W), F32),
            jax.ShapeDtypeStruct((b, la, KVA_W), F32),
            jax.ShapeDtypeStruct((b, lb, B_W), F32),
            jax.ShapeDtypeStruct((b, lb, B_W), F32),
        ),
        grid=(b,),
        in_specs=[
            per_b(t, d),
            per_b(6, d),
            pl.BlockSpec((1, d), lambda i: (0, 0)),
            pl.BlockSpec((d, IN_W), lambda i: (0, 0)),
            per_b(la, KVA_W), per_b(la, KVA_W), per_b(lb, B_W), per_b(lb, B_W),
        ],
        out_specs=(
            per_b(t, QA_W), per_b(la + t, KVA_W), per_b(la + t, KVA_W),
            per_b(t, B_W), per_b(lb + t, B_W), per_b(lb + t, B_W),
            per_b(la, KVA_W), per_b(la, KVA_W), per_b(lb, B_W), per_b(lb, B_W),
        ),
        compiler_params=_cparams("parallel"),
        name="pre_sample",
    )(x, mods, g_pre, w_in_bf, cak, cav, cbk, cbv)


MASKED = -1e30
PAIR_W = 2 * HEAD_DIM
ATT_BLOCK = 2 * CHUNK
A_COLS = A_WINDOW + ATT_BLOCK
B_COLS = B_REACH + ATT_BLOCK


def _pair_attend(q2, k2, v2, ind, bias, sinks):
    c = k2.shape[0] // 2
    s = lax.dot_general(q2, k2, (((1,), (1,)), ((), ())), preferred_element_type=F32) + bias
    es, mxs = [], []
    for half in range(2):
        sh = s[:, half * c:(half + 1) * c]
        mx = jnp.max(sh, axis=-1, keepdims=True)
        if sinks is not None:
            mx = jnp.maximum(mx, sinks[half])
        es.append(jnp.exp(sh - mx).astype(BF16))
        mxs.append(mx)
    o = jnp.dot(jnp.concatenate(es, axis=1), jnp.concatenate([v2, ind], axis=1), preferred_element_type=F32)
    dens = [o[:, PAIR_W + half:PAIR_W + half + 1] for half in range(2)]
    if sinks is not None:
        dens = [dens[half] + jnp.exp(sinks[half] - mxs[half]) for half in range(2)]
    lane = lax.broadcasted_iota(jnp.int32, (o.shape[0], PAIR_W), 1)
    return o[:, :PAIR_W] / jnp.where(lane < HEAD_DIM, dens[0], dens[1])


def _last_cols(table, full, n):
    if n == full:
        return table
    return jnp.concatenate([table[:, full - n:full], table[:, 2 * full - n:2 * full]], axis=1)


def _attn_prompt_kernel(qa_ref, ka_ref, va_ref, qb_ref, kb_ref, vb_ref, ba_ref, sk_ref, bb_ref, lo_ref, hi_ref,
                        ia_ref, ib_ref, oa_ref, ob_ref, ak_ref, av_ref, bk_ref, bv_ref, *, tile):
    j = pl.program_id(1)

    @pl.when(j == 0)
    def _():
        lo, hi = lo_ref[:, 0:PAIR_W], hi_ref[:, 0:PAIR_W]
        for src, dst in ((ka_ref, ak_ref), (va_ref, av_ref)):
            x = src[0]
            swapped = jnp.concatenate([x[:, HEAD_DIM:], x[:, :HEAD_DIM]], axis=1)
            dst[0] = x * lo
            dst[1] = swapped * hi
            dst[2] = swapped * lo
            dst[3] = x * hi
        for src, dst in ((kb_ref, bk_ref), (vb_ref, bv_ref)):
            x = src[0]
            dst[0] = x * lo_ref[...]
            dst[1] = x * hi_ref[...]

    def block(g, r0, n_a, n_b):
        rows = pl.ds(r0, ATT_BLOCK)
        band_a = pl.ds(g + ATT_BLOCK - n_a, n_a)
        band_b = pl.ds(g + ATT_BLOCK - n_b, n_b)
        ind_a = jnp.concatenate([ia_ref[0:n_a, :], ia_ref[A_COLS:A_COLS + n_a, :]], axis=0)
        ind_b = jnp.concatenate([ib_ref[0:n_b, :], ib_ref[B_COLS:B_COLS + n_b, :]], axis=0)
        for kv in range(A_KV_HEADS):
            c0, c1 = slice(2 * kv * PAIR_W, (2 * kv + 1) * PAIR_W), slice((2 * kv + 1) * PAIR_W, (2 * kv + 2) * PAIR_W)
            q2 = jnp.concatenate([qa_ref[0, rows, c0], qa_ref[0, rows, c1]], axis=0)
            k2 = jnp.concatenate([ak_ref[2 * kv, band_a, :], ak_ref[2 * kv + 1, band_a, :]], axis=0)
            v2 = jnp.concatenate([av_ref[2 * kv, band_a, :], av_ref[2 * kv + 1, band_a, :]], axis=0)
            o = _pair_attend(q2, k2, v2, ind_a, _last_cols(ba_ref[kv], A_COLS, n_a), (sk_ref[kv, 0], sk_ref[kv, 1]))
            oa_ref[0, rows, c0] = o[0:ATT_BLOCK].astype(BF16)
            oa_ref[0, rows, c1] = o[ATT_BLOCK:].astype(BF16)
        for p in range(B_HEADS // 2):
            cols = slice(p * PAIR_W, (p + 1) * PAIR_W)
            k2 = jnp.concatenate([bk_ref[0, band_b, cols], bk_ref[1, band_b, cols]], axis=0)
            v2 = jnp.concatenate([bv_ref[0, band_b, cols], bv_ref[1, band_b, cols]], axis=0)
            o = _pair_attend(qb_ref[0, rows, cols], k2, v2, ind_b, _last_cols(bb_ref[p], B_COLS, n_b), None)
            ob_ref[0, rows, cols] = o.astype(BF16)

    n_blocks = tile // ATT_BLOCK

    @pl.when(j == 0)
    def _():
        for blk in range(n_blocks):
            g = blk * ATT_BLOCK
            block(g, g, min(g + ATT_BLOCK, A_COLS), min(g + ATT_BLOCK, B_COLS))

    @pl.when(j > 0)
    def _():
        def body(blk, carry):
            r0 = pl.multiple_of(blk * ATT_BLOCK, ATT_BLOCK)
            block(pl.multiple_of(j * tile + r0, ATT_BLOCK), r0, A_COLS, B_COLS)
            return carry
        lax.fori_loop(0, n_blocks, body, 0)


def _head_indicator(c):
    ind = np.zeros((2 * c, PAIR_W), np.float32)
    ind[:c, 0] = 1.0
    ind[c:, 1] = 1.0
    return ind


def _pair_constants():
    lane = np.arange(B_W) % PAIR_W
    lo = jnp.asarray((lane < HEAD_DIM).astype(np.float32).reshape(1, B_W), BF16)
    hi = jnp.asarray((lane >= HEAD_DIM).astype(np.float32).reshape(1, B_W), BF16)
    return lo, hi, jnp.asarray(_head_indicator(A_COLS), BF16), jnp.asarray(_head_indicator(B_COLS), BF16)


def _attn_prompt(qa, ka, va, qb, kb, vb, bias2_a, sinks2, bias2_b):
    b, s, _ = qa.shape
    tile = min(ATT_TILE, s)
    assert s % tile == 0 and tile % ATT_BLOCK == 0
    lo, hi, ind_a, ind_b = _pair_constants()
    tok = lambda w: pl.BlockSpec((1, tile, w), lambda i, j: (i, j, 0))
    seq = lambda w: pl.BlockSpec((1, s, w), lambda i, j: (i, 0, 0))
    const = lambda shp: pl.BlockSpec(shp, lambda i, j: (0,) * len(shp))
    return pl.pallas_call(
        functools.partial(_attn_prompt_kernel, tile=tile),
        out_shape=(jax.ShapeDtypeStruct((b, s, QA_W), BF16), jax.ShapeDtypeStruct((b, s, B_W), BF16)),
        grid=(b, s // tile),
        in_specs=[tok(QA_W), seq(KVA_W), seq(KVA_W), tok(B_W), seq(B_W), seq(B_W),
                  const(bias2_a.shape), const(sinks2.shape), const(bias2_b.shape), const(lo.shape), const(hi.shape),
                  const(ind_a.shape), const(ind_b.shape)],
        out_specs=(tok(QA_W), tok(B_W)),
        scratch_shapes=[pltpu.VMEM((4, s, PAIR_W), BF16), pltpu.VMEM((4, s, PAIR_W), BF16),
                        pltpu.VMEM((2, s, B_W), BF16), pltpu.VMEM((2, s, B_W), BF16)],
        compiler_params=_cparams("arbitrary", "arbitrary"),
        name="attn_prompt",
    )(qa, ka, va, qb, kb, vb, bias2_a, sinks2, bias2_b, lo, hi, ind_a, ind_b)


def _attn_sample_kernel(qa_ref, ka_ref, va_ref, qb_ref, kb_ref, vb_ref, ba_ref, sk_ref, bb_ref, lo_ref, hi_ref,
                        ia_ref, ib_ref, oa_ref, ob_ref):
    second = lambda tbl: tbl[CHUNK:ATT_BLOCK]
    pad = lambda x: jnp.concatenate([jnp.zeros((CHUNK, PAIR_W), BF16), x], axis=0)
    lo, hi = lo_ref[:, 0:PAIR_W], hi_ref[:, 0:PAIR_W]

    def swap(x):
        return jnp.concatenate([x[:, HEAD_DIM:], x[:, :HEAD_DIM]], axis=1)

    ka, va = ka_ref[0], va_ref[0]
    for kv in range(A_KV_HEADS):
        c0, c1 = slice(2 * kv * PAIR_W, (2 * kv + 1) * PAIR_W), slice((2 * kv + 1) * PAIR_W, (2 * kv + 2) * PAIR_W)
        q2 = jnp.concatenate([qa_ref[0, :, c0], qa_ref[0, :, c1]], axis=0)
        first = (lambda x: x * lo) if kv == 0 else (lambda x: swap(x) * lo)
        last = (lambda x: swap(x) * hi) if kv == 0 else (lambda x: x * hi)
        k2 = jnp.concatenate([pad(first(ka)), pad(last(ka))], axis=0)
        v2 = jnp.concatenate([pad(first(va)), pad(last(va))], axis=0)
        two = lambda tbl: jnp.concatenate([second(tbl[0:ATT_BLOCK]), second(tbl[ATT_BLOCK:])], axis=0)
        o = _pair_attend(q2, k2, v2, ia_ref[...], two(ba_ref[kv]), (two(sk_ref[kv, 0]), two(sk_ref[kv, 1])))
        oa_ref[0, :, c0] = o[0:CHUNK].astype(BF16)
        oa_ref[0, :, c1] = o[CHUNK:].astype(BF16)
    for p in range(B_HEADS // 2):
        cols = slice(p * PAIR_W, (p + 1) * PAIR_W)
        kb, vb = kb_ref[0, :, cols], vb_ref[0, :, cols]
        k2 = jnp.concatenate([pad(kb * lo), pad(kb * hi)], axis=0)
        v2 = jnp.concatenate([pad(vb * lo), pad(vb * hi)], axis=0)
        o = _pair_attend(qb_ref[0, :, cols], k2, v2, ib_ref[...], second(bb_ref[p]), None)
        ob_ref[0, :, cols] = o.astype(BF16)


def _attn_sample(qa, ka, va, qb, kb, vb, bias2_a, sinks2, bias2_b):
    b, t, _ = qa.shape
    assert t == CHUNK
    lo, hi, ind_a, ind_b = _pair_constants()
    per_b = lambda r, w: pl.BlockSpec((1, r, w), lambda i: (i, 0, 0))
    const = lambda shp: pl.BlockSpec(shp, lambda i: (0,) * len(shp))
    return pl.pallas_call(
        _attn_sample_kernel,
        out_shape=(jax.ShapeDtypeStruct((b, t, QA_W), BF16), jax.ShapeDtypeStruct((b, t, B_W), BF16)),
        grid=(b,),
        in_specs=[per_b(t, QA_W), per_b(A_BAND, KVA_W), per_b(A_BAND, KVA_W),
                  per_b(t, B_W), per_b(B_BAND, B_W), per_b(B_BAND, B_W),
                  const(bias2_a.shape), const(sinks2.shape), const(bias2_b.shape), const(lo.shape), const(hi.shape),
                  const(ind_a.shape), const(ind_b.shape)],
        out_specs=(per_b(t, QA_W), per_b(t, B_W)),
        compiler_params=_cparams("parallel"),
        name="attn_sample",
    )(qa, ka, va, qb, kb, vb, bias2_a, sinks2, bias2_b, lo, hi, ind_a, ind_b)


def _route(lt):
    t = lt.shape[1]
    el = lt[0:N_EXPERTS]
    gl = lt[N_EXPERTS:N_EXPERTS + N_GROUPS]
    gmax = jnp.max(gl, axis=0, keepdims=True)
    gi = lax.broadcasted_iota(jnp.int32, (N_GROUPS, t), 0)
    gidx = jnp.min(jnp.where(gl == gmax, gi, N_GROUPS), axis=0, keepdims=True)
    g_w = 1.0 / jnp.sum(jnp.exp(gl - gmax), axis=0, keepdims=True)
    e_sel = el[(N_GROUPS - 1) * EXPERTS_PER_GROUP:]
    for g in range(N_GROUPS - 2, -1, -1):
        e_sel = jnp.where(gidx == g, el[g * EXPERTS_PER_GROUP:(g + 1) * EXPERTS_PER_GROUP], e_sel)
    ei = lax.broadcasted_iota(jnp.int32, (EXPERTS_PER_GROUP, t), 0)
    m1 = jnp.max(e_sel, axis=0, keepdims=True)
    i1 = jnp.min(jnp.where(e_sel == m1, ei, EXPERTS_PER_GROUP), axis=0, keepdims=True)
    rest = jnp.where(ei == i1, -jnp.inf, e_sel)
    m2 = jnp.max(rest, axis=0, keepdims=True)
    i2 = jnp.min(jnp.where(rest == m2, ei, EXPERTS_PER_GROUP), axis=0, keepdims=True)
    ex = jnp.exp(m2 - m1)
    den = 1.0 + ex
    w1 = g_w * (1.0 / den)
    w2 = g_w * (ex / den)
    lo = jnp.minimum(i1, i2)
    hi = jnp.maximum(i1, i2)
    first_is_lo = i1 < i2
    w_lo = jnp.where(first_is_lo, w1, w2)
    w_hi = jnp.where(first_is_lo, w2, w1)
    pair = ((lo * (2 * EXPERTS_PER_GROUP - 1 - lo)) >> 1) + (hi - lo - 1)
    return gidx * PAIRS_PER_GROUP + pair, w_lo, w_hi


def _route_logits(wr_ref, br_ref, h2):
    return lax.dot_general(wr_ref[...], h2, (((1,), (1,)), ((), ())), preferred_element_type=F32) + br_ref[...]


def _weight_tile(w_lo, w_hi):
    t = w_lo.shape[1]
    return jnp.transpose(jnp.concatenate([w_lo, w_hi, jnp.zeros((LANES - 2, t), F32)], axis=0))


def _rank_in_class(cls, run_ref, tri_ref):
    t = cls.shape[1]
    onehot = lax.broadcasted_iota(jnp.int32, (CLASS_ROWS, t), 0) == cls
    ones = jnp.where(onehot, 1.0, 0.0)
    before = jnp.dot(ones.astype(BF16), tri_ref[...], preferred_element_type=F32)
    run = run_ref[...]
    rank = jnp.sum(jnp.where(onehot, before + run, 0.0), axis=0, keepdims=True)
    run_ref[...] = run + jnp.sum(ones, axis=1, keepdims=True)
    return rank.astype(jnp.int32)


def _post_kernel(x_ref, oa_ref, ob_ref, mod_ref, g1_ref, g1p_ref, g2_ref,
                 wg_ref, bg_ref, wpa_ref, wpb_ref, wo_ref, wr_ref, br_ref, tri_ref,
                 x1_ref, h2_ref, cls_ref, rank_ref, cnt_ref, run_ref):
    @pl.when((pl.program_id(0) == 0) & (pl.program_id(1) == 0))
    def _():
        run_ref[...] = jnp.zeros_like(run_ref)

    nb, rows, _ = x_ref.shape
    stack = lambda parts: parts[0] if nb == 1 else jnp.concatenate(parts, axis=0)
    xs = [x_ref[g] for g in range(nb)]
    h = stack([_norm_mod(xs[g], g1_ref[...], mod_ref[g, 1:2, :], mod_ref[g, 0:1, :]) for g in range(nb)]).astype(BF16)
    gates = jax.nn.sigmoid(jnp.dot(h, wg_ref[...], preferred_element_type=F32) + bg_ref[...])
    pa = jnp.dot(stack([oa_ref[g] for g in range(nb)]), wpa_ref[...], preferred_element_type=F32)
    pb = jnp.dot(stack([ob_ref[g] for g in range(nb)]), wpb_ref[...], preferred_element_type=F32)
    mixed = gates[:, :D_MODEL] * pa + gates[:, D_MODEL:] * pb
    y = jnp.dot(mixed.astype(BF16), wo_ref[...], preferred_element_type=F32)
    h2_parts = []
    for g in range(nb):
        x1 = xs[g] + mod_ref[g, 2:3, :] * _rms(y[g * rows:(g + 1) * rows], g1p_ref[...])
        x1_ref[g] = x1
        h2_parts.append(_norm_mod(x1, g2_ref[...], mod_ref[g, 4:5, :], mod_ref[g, 3:4, :]))
    h2 = stack(h2_parts).astype(BF16)
    cls, w_lo, w_hi = _route(_route_logits(wr_ref, br_ref, h2))
    cls_ref[0] = cls
    rank_ref[0] = _rank_in_class(cls, run_ref, tri_ref)
    cnt_ref[...] = run_ref[...]
    t = nb * rows
    h2f = h2.astype(F32)
    for c in range(D_MODEL // LANES):
        h2_ref[pl.ds(c, t, stride=IN_SLAB), :] = h2f[:, c * LANES:(c + 1) * LANES]
    h2_ref[pl.ds(D_MODEL // LANES, t, stride=IN_SLAB), :] = _weight_tile(w_lo, w_hi)


def _post(x, oa, ob, mods, g_pre_mix, g_post_mix, g_pre_ffn, wg, bg, wpa, wpb, wo, wr, br, tile):
    b, s, d = x.shape
    rows = min(s, tile)
    nb = math.gcd(tile // rows, b)
    tile = nb * rows
    assert s % rows == 0
    nt = s // rows
    n_steps = (b // nb) * nt
    tri = (jnp.arange(tile)[:, None] < jnp.arange(tile)[None, :]).astype(BF16)
    tok = lambda w: pl.BlockSpec((nb, rows, w), lambda i, j: (i, j, 0))
    const = lambda shp: pl.BlockSpec(shp, lambda i, j: (0,) * len(shp))
    per_tile = pl.BlockSpec((1, 1, tile), lambda i, j: (i * nt + j, 0, 0))
    return pl.pallas_call(
        _post_kernel,
        out_shape=(
            jax.ShapeDtypeStruct((b, s, d), F32),
            jax.ShapeDtypeStruct((b * s * IN_SLAB, LANES), F32),
            jax.ShapeDtypeStruct((n_steps, 1, tile), jnp.int32),
            jax.ShapeDtypeStruct((n_steps, 1, tile), jnp.int32),
            jax.ShapeDtypeStruct((CLASS_ROWS, 1), F32),
        ),
        grid=(b // nb, nt),
        in_specs=[
            tok(d), tok(QA_W), tok(B_W),
            pl.BlockSpec((nb, 6, d), lambda i, j: (i, 0, 0)),
            const((1, d)), const((1, d)), const((1, d)),
            const(wg.shape), const(bg.shape), const(wpa.shape), const(wpb.shape), const(wo.shape),
            const(wr.shape), const(br.shape), const(tri.shape),
        ],
        out_specs=(tok(d), pl.BlockSpec((tile * IN_SLAB, LANES), lambda i, j: (i * nt + j, 0)),
                   per_tile, per_tile, const((CLASS_ROWS, 1))),
        scratch_shapes=[pltpu.VMEM((CLASS_ROWS, 1), F32)],
        compiler_params=_cparams("arbitrary", "arbitrary"),
        name="post",
    )(x, oa, ob, mods, g_pre_mix, g_post_mix, g_pre_ffn, wg, bg, wpa, wpb, wo, wr, br, tri)


def _invert_kernel(upos_ref, row_ref, *, n, scale):
    def body(i, carry):
        for u in range(INVERT_UNROLL):
            t = i * INVERT_UNROLL + u
            row_ref[upos_ref[t]] = t * scale
        return carry
    lax.fori_loop(0, n // INVERT_UNROLL, body, 0)


def _invert(upos, scale):
    n = upos.shape[0]
    return pl.pallas_call(
        functools.partial(_invert_kernel, n=n, scale=scale),
        out_shape=jax.ShapeDtypeStruct((n,), jnp.int32),
        in_specs=[pl.BlockSpec(memory_space=pltpu.SMEM)],
        out_specs=pl.BlockSpec(memory_space=pltpu.SMEM),
        name="invert",
    )(upos)


def _expert(xb, wgu_ref, wd_ref):
    gu = jnp.dot(xb, wgu_ref[0], preferred_element_type=F32)
    gate = gu[:, :D_EXPERT]
    he = (gate * jax.nn.sigmoid(gate)) * gu[:, D_EXPERT:]
    return jnp.dot(he.astype(BF16), wd_ref[0], preferred_element_type=F32)


def _moe_kernel(tok_ref, base_ref, ea_ref, eb_ref, nu_ref, h_ref, wgu_a_ref, wd_a_ref, wgu_b_ref, wd_b_ref,
                ys_ref, xbuf, sems, *, tile, n_tok):
    t = pl.program_id(0)
    nu = nu_ref[0]
    slot = t % 2

    def slab_copy(src_row0, dst_slot, r):
        return pltpu.make_async_copy(h_ref.at[pl.ds(src_row0, IN_SLAB)],
                                     xbuf.at[dst_slot, pl.ds(r * IN_SLAB, IN_SLAB)], sems.at[dst_slot])

    def fetch(step, dst_slot):
        base = base_ref[step]
        for r in range(tile):
            slab_copy(tok_ref[jnp.minimum(base + r, n_tok - 1)], dst_slot, r).start(priority=r % 2)

    def wait(dst_slot):
        for r in range(tile):
            slab_copy(0, dst_slot, 0).wait()

    @pl.when(t == 0)
    def _():
        fetch(0, 0)

    @pl.when(t < nu)
    def _():
        wait(slot)
        fetch(jnp.minimum(t + 1, nu - 1), 1 - slot)
        lane_tile = lambda c: xbuf[slot, pl.ds(c, tile, stride=IN_SLAB), :]
        xb = jnp.concatenate([lane_tile(c) for c in range(D_MODEL // LANES)], axis=1).astype(BF16)
        w = lane_tile(D_MODEL // LANES)
        y = w[:, 0:1] * _expert(xb, wgu_a_ref, wd_a_ref) + w[:, 1:2] * _expert(xb, wgu_b_ref, wd_b_ref)
        for c in range(OUT_SLAB):
            ys_ref[pl.ds(c, tile, stride=OUT_SLAB), :] = y[:, c * LANES:(c + 1) * LANES]

    @pl.when(t == nu - 1)
    def _():
        wait(1 - slot)

    @pl.when(t >= nu)
    def _():
        ys_ref[...] = jnp.zeros_like(ys_ref)


def _moe_grouped(h_slabs, sorted_tok, tile_base, tile_ea, tile_eb, n_used, wgu, wd, tile, n_tiles):
    d = D_MODEL
    n_tok = sorted_tok.shape[0]
    expert = lambda which, shp: pl.BlockSpec(shp, lambda t, tok, base, ea, eb, nu: ((ea, eb)[which][t], 0, 0))
    return pl.pallas_call(
        functools.partial(_moe_kernel, tile=tile, n_tok=n_tok),
        out_shape=jax.ShapeDtypeStruct((n_tiles * tile * OUT_SLAB, LANES), F32),
        grid_spec=pltpu.PrefetchScalarGridSpec(
            num_scalar_prefetch=5,
            grid=(n_tiles,),
            in_specs=[
                pl.BlockSpec(memory_space=pl.ANY),
                expert(0, (1, d, 2 * D_EXPERT)), expert(0, (1, D_EXPERT, d)),
                expert(1, (1, d, 2 * D_EXPERT)), expert(1, (1, D_EXPERT, d)),
            ],
            out_specs=pl.BlockSpec((tile * OUT_SLAB, LANES), lambda t, *_: (t, 0)),
            scratch_shapes=[pltpu.VMEM((2, tile * IN_SLAB, LANES), F32), pltpu.SemaphoreType.DMA((2,))],
        ),
        compiler_params=_cparams("arbitrary"),
        name="moe_grouped",
    )(sorted_tok, tile_base, tile_ea, tile_eb, n_used, h_slabs, wgu, wd, wgu, wd)


_PAIR_LO, _PAIR_HI = np.triu_indices(EXPERTS_PER_GROUP, k=1)
_CLASS_LO = np.concatenate([g * EXPERTS_PER_GROUP + _PAIR_LO for g in range(N_GROUPS)]).astype(np.int32)
_CLASS_HI = np.concatenate([g * EXPERTS_PER_GROUP + _PAIR_HI for g in range(N_GROUPS)]).astype(np.int32)


def _moe_plan(counts, cls, rank, n, tile):
    p_max = -(-(n + N_CLASSES * (tile - 1)) // tile) * tile
    nt = p_max // tile
    counts = counts[:N_CLASSES].astype(jnp.int32)
    padded = ((counts + tile - 1) // tile) * tile
    pad_end = jnp.cumsum(padded)
    gstart = pad_end - padded
    ustart = jnp.cumsum(counts) - counts
    onehot = cls[:, None] == jnp.arange(N_CLASSES, dtype=jnp.int32)[None, :]
    pos = rank + jnp.sum(jnp.where(onehot, gstart[None, :], 0), axis=1)
    upos = rank + jnp.sum(jnp.where(onehot, ustart[None, :], 0), axis=1)
    n_used = (pad_end[-1] // tile).astype(jnp.int32).reshape(1)
    tiles = jnp.arange(nt, dtype=jnp.int32)
    tile_cls = jnp.minimum(jnp.sum((pad_end[None, :] <= (tiles * tile)[:, None]).astype(jnp.int32), axis=1), N_CLASSES - 1)
    tile_ea = jnp.asarray(_CLASS_LO)[tile_cls]
    tile_eb = jnp.asarray(_CLASS_HI)[tile_cls]
    tile_base = jnp.clip(ustart[tile_cls] + tiles * tile - gstart[tile_cls], 0, n - 1)
    return nt, pos, upos, tile_base, tile_ea, tile_eb, n_used


def _final_kernel(pos_ref, x1_ref, ys_ref, mod_ref, g_ref, o_ref, ybuf, sems, *, tile, n_steps):
    s = pl.program_id(0)
    slot = s % 2

    def slab_copy(src_row0, dst_slot, dst_row):
        return pltpu.make_async_copy(ys_ref.at[pl.ds(src_row0, OUT_SLAB)],
                                     ybuf.at[dst_slot, pl.ds(dst_row * OUT_SLAB, OUT_SLAB)], sems.at[dst_slot])

    def fetch(step, dst_slot):
        for r in range(tile):
            slab_copy(pos_ref[step * tile + r], dst_slot, r).start(priority=r % 2)

    def wait(dst_slot):
        for r in range(tile):
            slab_copy(0, dst_slot, 0).wait()

    @pl.when(s == 0)
    def _():
        fetch(0, 0)

    wait(slot)
    fetch(jnp.minimum(s + 1, n_steps - 1), 1 - slot)
    y = jnp.concatenate([ybuf[slot, pl.ds(c, tile, stride=OUT_SLAB), :] for c in range(OUT_SLAB)], axis=1)
    o_ref[...] = x1_ref[...] + mod_ref[0, 5:6, :] * _rms(y, g_ref[...])

    @pl.when(s == n_steps - 1)
    def _():
        wait(1 - slot)


def _final(x1, ys, pos, mods, g_post_ffn, tile):
    b, s, d = x1.shape
    n = b * s
    per_b = s // tile
    n_steps = n // tile
    tok = pl.BlockSpec((tile, d), lambda i, *_: (i, 0))
    out = pl.pallas_call(
        functools.partial(_final_kernel, tile=tile, n_steps=n_steps),
        out_shape=jax.ShapeDtypeStruct((n, d), F32),
        grid_spec=pltpu.PrefetchScalarGridSpec(
            num_scalar_prefetch=1,
            grid=(n_steps,),
            in_specs=[
                tok,
                pl.BlockSpec(memory_space=pl.ANY),
                pl.BlockSpec((1, 6, d), lambda i, *_: (i // per_b, 0, 0)),
                pl.BlockSpec((1, d), lambda i, *_: (0, 0)),
            ],
            out_specs=tok,
            scratch_shapes=[pltpu.VMEM((2, tile * OUT_SLAB, LANES), F32), pltpu.SemaphoreType.DMA((2,))],
        ),
        compiler_params=_cparams("arbitrary"),
        name="final",
    )(pos, x1.reshape(n, d), ys, mods, g_post_ffn)
    return out.reshape(b, s, d)


def _t5_bucket(rel):
    half = T5_BUCKETS // 2
    exact = half // 2
    ret = jnp.where(rel > 0, half, 0)
    n = jnp.abs(rel)
    nf = jnp.maximum(n, 1).astype(F32)
    large = exact + (jnp.log(nf / exact) / math.log(T5_MAX_DIST / exact) * (half - exact)).astype(jnp.int32)
    large = jnp.minimum(large, half - 1)
    return ret + jnp.where(n < exact, n, large)


def _toeplitz(u, n_rows, n_cols):
    lead, p = u.shape[:-1], n_rows + n_cols
    w = jnp.concatenate([u, jnp.zeros(lead + (1,), u.dtype)], axis=-1)
    periodic = jnp.broadcast_to(w[..., None, :], lead + (n_rows + 1, p)).reshape(lead + ((n_rows + 1) * p,))
    shifted = periodic[..., :n_rows * (p + 1)].reshape(lead + (n_rows, p + 1))[..., :n_cols]
    return shifted[..., ::-1, :]


def _bias_tables(t5_table, rel_table):
    ja = jnp.arange(CHUNK - 1 + A_BAND)
    ua = t5_table[_t5_bucket(ja - (CHUNK - 1) - A_WINDOW)].T.astype(F32)
    jb = jnp.arange(CHUNK - 1 + B_BAND)
    ub = rel_table[:, jnp.clip((CHUNK - 1) - jb + B_REACH, -REL_CLIP, REL_CLIP) + REL_CLIP].astype(F32)
    return _toeplitz(ua, CHUNK, A_BAND), _toeplitz(ub, CHUNK, B_BAND)


def _two_chunk(bias):
    masked = jnp.full(bias.shape[:-1] + (CHUNK,), MASKED, F32)
    return jnp.concatenate([jnp.concatenate([bias, masked], axis=-1), jnp.concatenate([masked, bias], axis=-1)], axis=-2)


def _pair_tables(bias_a, bias_b, sinks):
    a2, b2 = _two_chunk(bias_a), _two_chunk(bias_b)
    stack2 = lambda x, h0, h1: jnp.concatenate([x[h0], x[h1]], axis=0)
    bias2_a = jnp.stack([jnp.concatenate([stack2(a2, 4 * kv, 4 * kv + 2), stack2(a2, 4 * kv + 1, 4 * kv + 3)], axis=1)
                         for kv in range(A_KV_HEADS)])
    bias2_b = jnp.concatenate([b2[0::2], b2[1::2]], axis=-1)
    sk = jnp.broadcast_to(sinks.astype(F32).reshape(A_Q_HEADS, 1, 1), (A_Q_HEADS, ATT_BLOCK, 1))
    sinks2 = jnp.stack([jnp.stack([stack2(sk, 4 * kv, 4 * kv + 2), stack2(sk, 4 * kv + 1, 4 * kv + 3)])
                        for kv in range(A_KV_HEADS)])
    return bias2_a, sinks2, bias2_b


def _moe_and_final(x1, h2e, cls, rank, counts, mods, g_post_ffn, wgu, wd, seq_tile, moe_tile):
    b, s, d = x1.shape
    n = b * s
    n_tiles, pos, upos, tile_base, tile_ea, tile_eb, n_used = _moe_plan(
        counts.reshape(CLASS_ROWS), cls.reshape(n), rank.reshape(n), n, moe_tile)
    ys = _moe_grouped(h2e, _invert(upos, IN_SLAB), tile_base, tile_ea, tile_eb, n_used, wgu, wd, moe_tile, n_tiles)
    return _final(x1, ys, pos * OUT_SLAB, mods, g_post_ffn, seq_tile)


def kernel(x_prompt, x_sample, c_prompt, c_sample, cache_a_k, cache_a_v, cache_b_k, cache_b_v, w_ada, b_ada, g_pre_mix, g_post_mix, g_pre_ffn, g_post_ffn, w_in, a_sinks, t5_table, b_rel_table, w_proj_a, w_proj_b, w_gate, b_gate, w_o, w_route_g, b_route_g, w_route_e, b_route_e, w_e_gate, w_e_up, w_e_down):
    depth = w_in.shape[0]
    assert depth == 1
    l = 0
    bp, sp, d = x_prompt.shape
    bs, ts, _ = x_sample.shape

    mods = _ada(jnp.concatenate([c_prompt, c_sample], axis=0), w_ada[l], b_ada[l]).reshape(bp + bs, 6, d)
    mods_p, mods_s = mods[:bp], mods[bp:]

    w_in_bf = w_in[l].astype(BF16)
    wg, wpa, wpb, wo = w_gate[l].astype(BF16), w_proj_a[l].astype(BF16), w_proj_b[l].astype(BF16), w_o[l].astype(BF16)
    bg = b_gate[l].reshape(1, 2 * d)
    pad_rows = ROUTE_ROWS - N_EXPERTS - N_GROUPS
    wr = jnp.concatenate([w_route_e[l].T, w_route_g[l].T, jnp.zeros((pad_rows, d), F32)], axis=0).astype(BF16)
    br = jnp.concatenate([b_route_e[l], b_route_g[l], jnp.zeros((pad_rows,), F32)]).reshape(ROUTE_ROWS, 1)
    wgu = jnp.concatenate([w_e_gate[l], w_e_up[l]], axis=-1).astype(BF16)
    wd = w_e_down[l].astype(BF16)
    g1, g1p, g2, g2p = (g[l].reshape(1, d) for g in (g_pre_mix, g_post_mix, g_pre_ffn, g_post_ffn))

    bias_a, bias_b = _bias_tables(t5_table, b_rel_table[l])
    bias2_a, sinks2, bias2_b = _pair_tables(bias_a, bias_b, a_sinks[l])

    qa, ka, va, qb, kb, vb, sak_p, sav_p, sbk_p, sbv_p = _pre_prompt(x_prompt, mods_p, g1, w_in_bf)
    oa, ob = _attn_prompt(qa, ka, va, qb, kb, vb, bias2_a, sinks2, bias2_b)
    x1, h2e, cls, rank, counts = _post(x_prompt, oa, ob, mods_p, g1, g1p, g2, wg, bg, wpa, wpb, wo, wr, br, SEQ_TILE)
    y_prompt = _moe_and_final(x1, h2e, cls, rank, counts, mods_p, g2p, wgu, wd, SEQ_TILE, MOE_TILE_PROMPT)

    la, lb = cache_a_k.shape[2], cache_b_k.shape[2]
    cak = cache_a_k[l].reshape(bs, la, KVA_W)
    cav = cache_a_v[l].reshape(bs, la, KVA_W)
    cbk = cache_b_k[l].reshape(bs, lb, B_W)
    cbv = cache_b_v[l].reshape(bs, lb, B_W)
    qa, ka, va, qb, kb, vb, sak_s, sav_s, sbk_s, sbv_s = _pre_sample(x_sample, mods_s, g1, w_in_bf, cak, cav, cbk, cbv)
    oa, ob = _attn_sample(qa, ka, va, qb, kb, vb, bias2_a, sinks2, bias2_b)
    x1, h2e, cls, rank, counts = _post(x_sample, oa, ob, mods_s, g1, g1p, g2, wg, bg, wpa, wpb, wo, wr, br, SEQ_TILE)
    y_sample = _moe_and_final(x1, h2e, cls, rank, counts, mods_s, g2p, wgu, wd, ts, MOE_TILE_SAMPLE)

    a_state = lambda v, b, r: v.reshape(1, b, r, A_KV_HEADS, HEAD_DIM)
    b_state = lambda v, b, r: v.reshape(1, b, r, B_HEADS, HEAD_DIM)
    return (y_prompt, y_sample,
            a_state(sak_p, bp, A_WINDOW), a_state(sav_p, bp, A_WINDOW),
            b_state(sbk_p, bp, B_REACH), b_state(sbv_p, bp, B_REACH),
            a_state(sak_s, bs, la), a_state(sav_s, bs, la),
            b_state(sbk_s, bs, lb), b_state(sbv_s, bs, lb))
```

```python
import functools
import math

import numpy as np
import jax
import jax.numpy as jnp
from jax import lax
from jax.experimental import pallas as pl
from jax.experimental.pallas import tpu as pltpu

D_MODEL = 1024
CHUNK = 64
HEAD_DIM = 64
A_Q_HEADS = 8
A_KV_HEADS = 2
A_WINDOW = 128
B_HEADS = 8
B_REACH = 8 * CHUNK
REL_CLIP = 256
T5_BUCKETS = 32
T5_MAX_DIST = 128
N_GROUPS = 4
EXPERTS_PER_GROUP = 8
N_EXPERTS = N_GROUPS * EXPERTS_PER_GROUP
D_EXPERT = D_MODEL // 4
EPS = 1e-6

QA_W = A_Q_HEADS * HEAD_DIM
KVA_W = A_KV_HEADS * HEAD_DIM
B_W = B_HEADS * HEAD_DIM
IN_W = QA_W + 2 * KVA_W + 3 * B_W
A_BAND = A_WINDOW + CHUNK
B_BAND = B_REACH + CHUNK

PAIRS_PER_GROUP = EXPERTS_PER_GROUP * (EXPERTS_PER_GROUP - 1) // 2
N_CLASSES = N_GROUPS * PAIRS_PER_GROUP
ROUTE_ROWS = 40
CLASS_ROWS = 128
LANES = 128
IN_SLAB = D_MODEL // LANES + 1
OUT_SLAB = D_MODEL // LANES

F32 = jnp.float32
BF16 = jnp.bfloat16

VMEM_LIMIT_BYTES = 56 * 1024 * 1024

SEQ_TILE = 512
ATT_TILE = 512
MOE_TILE_PROMPT = 256
MOE_TILE_SAMPLE = 32
MOE_BUFFERS = 3
INVERT_UNROLL = 32


def _cparams(*sem):
    return pltpu.CompilerParams(dimension_semantics=sem, vmem_limit_bytes=VMEM_LIMIT_BYTES)


def _norm_mod(x, g, scale, shift):
    y = x * lax.rsqrt(jnp.mean(x * x, axis=-1, keepdims=True) + EPS)
    return (y * g) * (1.0 + scale) + shift


def _rms(x, g):
    return (x * lax.rsqrt(jnp.mean(x * x, axis=-1, keepdims=True) + EPS)) * g


def _ada_kernel(c_ref, w_ref, b_ref, o_ref):
    c = c_ref[...]
    s = (c * jax.nn.sigmoid(c)).astype(BF16)
    o_ref[...] = jnp.dot(s, w_ref[...].astype(BF16), preferred_element_type=F32) + b_ref[...]


def _ada(c, w_ada, b_ada):
    n, d = c.shape
    wn = w_ada.shape[1]
    tn = 512
    return pl.pallas_call(
        _ada_kernel,
        out_shape=jax.ShapeDtypeStruct((n, wn), F32),
        grid=(wn // tn,),
        in_specs=[
            pl.BlockSpec((n, d), lambda j: (0, 0)),
            pl.BlockSpec((d, tn), lambda j: (0, j)),
            pl.BlockSpec((1, tn), lambda j: (0, j)),
        ],
        out_specs=pl.BlockSpec((n, tn), lambda j: (0, j)),
        compiler_params=_cparams("arbitrary"),
        name="ada",
    )(c, w_ada, b_ada.reshape(1, wn))


_COL_QA = (0, QA_W)
_COL_KA = (QA_W, QA_W + KVA_W)
_COL_VA = (QA_W + KVA_W, QA_W + 2 * KVA_W)
_COL_QB = (QA_W + 2 * KVA_W, QA_W + 2 * KVA_W + B_W)
_COL_KB = (_COL_QB[1], _COL_QB[1] + B_W)
_COL_VB = (_COL_KB[1], _COL_KB[1] + B_W)
Q_SCALE = HEAD_DIM ** -0.5


def _project(x_ref, mod_ref, g_ref, w_ref):
    h = _norm_mod(x_ref[0], g_ref[...], mod_ref[0, 1:2, :], mod_ref[0, 0:1, :])
    return jnp.dot(h.astype(BF16), w_ref[...], preferred_element_type=F32)


def _cols(p, c):
    return p[:, c[0]:c[1]]


def _pre_prompt_kernel(x_ref, mod_ref, g_ref, w_ref,
                       qa_ref, ka_ref, va_ref, qb_ref, kb_ref, vb_ref,
                       sak_ref, sav_ref, sbk_ref, sbv_ref, *, n_tiles, tile):
    p = _project(x_ref, mod_ref, g_ref, w_ref)
    qa_ref[0] = (_cols(p, _COL_QA) * Q_SCALE).astype(BF16)
    ka_ref[0] = _cols(p, _COL_KA).astype(BF16)
    va_ref[0] = _cols(p, _COL_VA).astype(BF16)
    qb_ref[0] = (_cols(p, _COL_QB) * Q_SCALE).astype(BF16)
    kb_ref[0] = _cols(p, _COL_KB).astype(BF16)
    vb_ref[0] = _cols(p, _COL_VB).astype(BF16)

    @pl.when(pl.program_id(1) == n_tiles - 1)
    def _():
        sak_ref[0] = _cols(p, _COL_KA)[tile - A_WINDOW:, :]
        sav_ref[0] = _cols(p, _COL_VA)[tile - A_WINDOW:, :]
        sbk_ref[0] = _cols(p, _COL_KB)[tile - B_REACH:, :]
        sbv_ref[0] = _cols(p, _COL_VB)[tile - B_REACH:, :]


def _pre_prompt(x, mods, g_pre, w_in_bf):
    b, s, d = x.shape
    tile = SEQ_TILE
    assert s % tile == 0 and tile >= B_REACH and s >= B_REACH
    nt = s // tile
    tok = lambda w: pl.BlockSpec((1, tile, w), lambda i, j: (i, j, 0))
    state = lambda r, w: pl.BlockSpec((1, r, w), lambda i, j: (i, 0, 0))
    return pl.pallas_call(
        functools.partial(_pre_prompt_kernel, n_tiles=nt, tile=tile),
        out_shape=(
            jax.ShapeDtypeStruct((b, s, QA_W), BF16),
            jax.ShapeDtypeStruct((b, s, KVA_W), BF16),
            jax.ShapeDtypeStruct((b, s, KVA_W), BF16),
            jax.ShapeDtypeStruct((b, s, B_W), BF16),
            jax.ShapeDtypeStruct((b, s, B_W), BF16),
            jax.ShapeDtypeStruct((b, s, B_W), BF16),
            jax.ShapeDtypeStruct((b, A_WINDOW, KVA_W), F32),
            jax.ShapeDtypeStruct((b, A_WINDOW, KVA_W), F32),
            jax.ShapeDtypeStruct((b, B_REACH, B_W), F32),
            jax.ShapeDtypeStruct((b, B_REACH, B_W), F32),
        ),
        grid=(b, nt),
        in_specs=[
            tok(d),
            pl.BlockSpec((1, 6, d), lambda i, j: (i, 0, 0)),
            pl.BlockSpec((1, d), lambda i, j: (0, 0)),
            pl.BlockSpec((d, IN_W), lambda i, j: (0, 0)),
        ],
        out_specs=(
            tok(QA_W), tok(KVA_W), tok(KVA_W), tok(B_W), tok(B_W), tok(B_W),
            state(A_WINDOW, KVA_W), state(A_WINDOW, KVA_W), state(B_REACH, B_W), state(B_REACH, B_W),
        ),
        compiler_params=_cparams("parallel", "arbitrary"),
        name="pre_prompt",
    )(x, mods, g_pre, w_in_bf)


def _pre_sample_kernel(x_ref, mod_ref, g_ref, w_ref, cak_ref, cav_ref, cbk_ref, cbv_ref,
                       qa_ref, ka_ref, va_ref, qb_ref, kb_ref, vb_ref,
                       sak_ref, sav_ref, sbk_ref, sbv_ref, *, t, la, lb):
    p = _project(x_ref, mod_ref, g_ref, w_ref)
    qa_ref[0] = (_cols(p, _COL_QA) * Q_SCALE).astype(BF16)
    qb_ref[0] = (_cols(p, _COL_QB) * Q_SCALE).astype(BF16)
    for cache_ref, band_ref, state_ref, col, hist in (
            (cak_ref, ka_ref, sak_ref, _COL_KA, la), (cav_ref, va_ref, sav_ref, _COL_VA, la),
            (cbk_ref, kb_ref, sbk_ref, _COL_KB, lb), (cbv_ref, vb_ref, sbv_ref, _COL_VB, lb)):
        new = _cols(p, col)
        cache = cache_ref[0]
        band_ref[0, 0:hist, :] = cache.astype(BF16)
        band_ref[0, hist:hist + t, :] = new.astype(BF16)
        state_ref[0, 0:hist - t, :] = cache[t:, :]
        state_ref[0, hist - t:hist, :] = new


def _pre_sample(x, mods, g_pre, w_in_bf, cak, cav, cbk, cbv):
    b, t, d = x.shape
    la, lb = cak.shape[1], cbk.shape[1]
    assert t == CHUNK and la == A_WINDOW and lb == B_REACH
    per_b = lambda r, w: pl.BlockSpec((1, r, w), lambda i: (i, 0, 0))
    return pl.pallas_call(
        functools.partial(_pre_sample_kernel, t=t, la=la, lb=lb),
        out_shape=(
            jax.ShapeDtypeStruct((b, t, QA_W), BF16),
            jax.ShapeDtypeStruct((b, la + t, KVA_W), BF16),
            jax.ShapeDtypeStruct((b, la + t, KVA_W), BF16),
            jax.ShapeDtypeStruct((b, t, B_W), BF16),
            jax.ShapeDtypeStruct((b, lb + t, B_W), BF16),
            jax.ShapeDtypeStruct((b, lb + t, B_W), BF16),
            jax.ShapeDtypeStruct((b, la, KVA_W), F32),
            jax.ShapeDtypeStruct((b, la, KVA_W), F32),
            jax.ShapeDtypeStruct((b, lb, B_W), F32),
            jax.ShapeDtypeStruct((b, lb, B_W), F32),
        ),
        grid=(b,),
        in_specs=[
            per_b(t, d),
            per_b(6, d),
            pl.BlockSpec((1, d), lambda i: (0, 0)),
            pl.BlockSpec((d, IN_W), lambda i: (0, 0)),
            per_b(la, KVA_W), per_b(la, KVA_W), per_b(lb, B_W), per_b(lb, B_W),
        ],
        out_specs=(
            per_b(t, QA_W), per_b(la + t, KVA_W), per_b(la + t, KVA_W),
            per_b(t, B_W), per_b(lb + t, B_W), per_b(lb + t, B_W),
            per_b(la, KVA_W), per_b(la, KVA_W), per_b(lb, B_W), per_b(lb, B_W),
        ),
        compiler_params=_cparams("parallel"),
        name="pre_sample",
    )(x, mods, g_pre, w_in_bf, cak, cav, cbk, cbv)


MASKED = -1e30
PAIR_W = 2 * HEAD_DIM
ATT_BLOCK = 2 * CHUNK
A_COLS = A_WINDOW + ATT_BLOCK
B_COLS = B_REACH + ATT_BLOCK


def _pair_attend(q2, k2, v2, ind, bias, sinks):
    c = k2.shape[0] // 2
    s = lax.dot_general(q2, k2, (((1,), (1,)), ((), ())), preferred_element_type=F32) + bias
    es, mxs = [], []
    for half in range(2):
        sh = s[:, half * c:(half + 1) * c]
        mx = jnp.max(sh, axis=-1, keepdims=True)
        if sinks is not None:
            mx = jnp.maximum(mx, sinks[half])
        es.append(jnp.exp(sh - mx).astype(BF16))
        mxs.append(mx)
    o = jnp.dot(jnp.concatenate(es, axis=1), jnp.concatenate([v2, ind], axis=1), preferred_element_type=F32)
    dens = [o[:, PAIR_W + half:PAIR_W + half + 1] for half in range(2)]
    if sinks is not None:
        dens = [dens[half] + jnp.exp(sinks[half] - mxs[half]) for half in range(2)]
    lane = lax.broadcasted_iota(jnp.int32, (o.shape[0], PAIR_W), 1)
    return o[:, :PAIR_W] / jnp.where(lane < HEAD_DIM, dens[0], dens[1])


def _last_cols(table, full, n):
    if n == full:
        return table
    return jnp.concatenate([table[:, full - n:full], table[:, 2 * full - n:2 * full]], axis=1)


def _attn_prompt_kernel(qa_ref, ka_ref, va_ref, qb_ref, kb_ref, vb_ref, ba_ref, sk_ref, bb_ref, lo_ref, hi_ref,
                        ia_ref, ib_ref, oa_ref, ob_ref, ak_ref, av_ref, bk_ref, bv_ref, *, tile):
    j = pl.program_id(1)

    @pl.when(j == 0)
    def _():
        lo, hi = lo_ref[:, 0:PAIR_W], hi_ref[:, 0:PAIR_W]
        for src, dst in ((ka_ref, ak_ref), (va_ref, av_ref)):
            x = src[0]
            swapped = jnp.concatenate([x[:, HEAD_DIM:], x[:, :HEAD_DIM]], axis=1)
            dst[0] = x * lo
            dst[1] = swapped * hi
            dst[2] = swapped * lo
            dst[3] = x * hi
        for src, dst in ((kb_ref, bk_ref), (vb_ref, bv_ref)):
            x = src[0]
            dst[0] = x * lo_ref[...]
            dst[1] = x * hi_ref[...]

    def block(g, r0, n_a, n_b):
        rows = pl.ds(r0, ATT_BLOCK)
        band_a = pl.ds(g + ATT_BLOCK - n_a, n_a)
        band_b = pl.ds(g + ATT_BLOCK - n_b, n_b)
        ind_a = jnp.concatenate([ia_ref[0:n_a, :], ia_ref[A_COLS:A_COLS + n_a, :]], axis=0)
        ind_b = jnp.concatenate([ib_ref[0:n_b, :], ib_ref[B_COLS:B_COLS + n_b, :]], axis=0)
        for kv in range(A_KV_HEADS):
            c0, c1 = slice(2 * kv * PAIR_W, (2 * kv + 1) * PAIR_W), slice((2 * kv + 1) * PAIR_W, (2 * kv + 2) * PAIR_W)
            q2 = jnp.concatenate([qa_ref[0, rows, c0], qa_ref[0, rows, c1]], axis=0)
            k2 = jnp.concatenate([ak_ref[2 * kv, band_a, :], ak_ref[2 * kv + 1, band_a, :]], axis=0)
            v2 = jnp.concatenate([av_ref[2 * kv, band_a, :], av_ref[2 * kv + 1, band_a, :]], axis=0)
            o = _pair_attend(q2, k2, v2, ind_a, _last_cols(ba_ref[kv], A_COLS, n_a), (sk_ref[kv, 0], sk_ref[kv, 1]))
            oa_ref[0, rows, c0] = o[0:ATT_BLOCK].astype(BF16)
            oa_ref[0, rows, c1] = o[ATT_BLOCK:].astype(BF16)
        for p in range(B_HEADS // 2):
            cols = slice(p * PAIR_W, (p + 1) * PAIR_W)
            k2 = jnp.concatenate([bk_ref[0, band_b, cols], bk_ref[1, band_b, cols]], axis=0)
            v2 = jnp.concatenate([bv_ref[0, band_b, cols], bv_ref[1, band_b, cols]], axis=0)
            o = _pair_attend(qb_ref[0, rows, cols], k2, v2, ind_b, _last_cols(bb_ref[p], B_COLS, n_b), None)
            ob_ref[0, rows, cols] = o.astype(BF16)

    n_blocks = tile // ATT_BLOCK

    @pl.when(j == 0)
    def _():
        for blk in range(n_blocks):
            g = blk * ATT_BLOCK
            block(g, g, min(g + ATT_BLOCK, A_COLS), min(g + ATT_BLOCK, B_COLS))

    @pl.when(j > 0)
    def _():
        def body(blk, carry):
            r0 = pl.multiple_of(blk * ATT_BLOCK, ATT_BLOCK)
            block(pl.multiple_of(j * tile + r0, ATT_BLOCK), r0, A_COLS, B_COLS)
            return carry
        lax.fori_loop(0, n_blocks, body, 0)


def _head_indicator(c):
    ind = np.zeros((2 * c, PAIR_W), np.float32)
    ind[:c, 0] = 1.0
    ind[c:, 1] = 1.0
    return ind


def _pair_constants():
    lane = np.arange(B_W) % PAIR_W
    lo = jnp.asarray((lane < HEAD_DIM).astype(np.float32).reshape(1, B_W), BF16)
    hi = jnp.asarray((lane >= HEAD_DIM).astype(np.float32).reshape(1, B_W), BF16)
    return lo, hi, jnp.asarray(_head_indicator(A_COLS), BF16), jnp.asarray(_head_indicator(B_COLS), BF16)


def _attn_prompt(qa, ka, va, qb, kb, vb, bias2_a, sinks2, bias2_b):
    b, s, _ = qa.shape
    tile = min(ATT_TILE, s)
    assert s % tile == 0 and tile % ATT_BLOCK == 0
    lo, hi, ind_a, ind_b = _pair_constants()
    tok = lambda w: pl.BlockSpec((1, tile, w), lambda i, j: (i, j, 0))
    seq = lambda w: pl.BlockSpec((1, s, w), lambda i, j: (i, 0, 0))
    const = lambda shp: pl.BlockSpec(shp, lambda i, j: (0,) * len(shp))
    return pl.pallas_call(
        functools.partial(_attn_prompt_kernel, tile=tile),
        out_shape=(jax.ShapeDtypeStruct((b, s, QA_W), BF16), jax.ShapeDtypeStruct((b, s, B_W), BF16)),
        grid=(b, s // tile),
        in_specs=[tok(QA_W), seq(KVA_W), seq(KVA_W), tok(B_W), seq(B_W), seq(B_W),
                  const(bias2_a.shape), const(sinks2.shape), const(bias2_b.shape), const(lo.shape), const(hi.shape),
                  const(ind_a.shape), const(ind_b.shape)],
        out_specs=(tok(QA_W), tok(B_W)),
        scratch_shapes=[pltpu.VMEM((4, s, PAIR_W), BF16), pltpu.VMEM((4, s, PAIR_W), BF16),
                        pltpu.VMEM((2, s, B_W), BF16), pltpu.VMEM((2, s, B_W), BF16)],
        compiler_params=_cparams("arbitrary", "arbitrary"),
        name="attn_prompt",
    )(qa, ka, va, qb, kb, vb, bias2_a, sinks2, bias2_b, lo, hi, ind_a, ind_b)


def _attn_sample_kernel(qa_ref, ka_ref, va_ref, qb_ref, kb_ref, vb_ref, ba_ref, sk_ref, bb_ref, lo_ref, hi_ref,
                        ia_ref, ib_ref, oa_ref, ob_ref):
    second = lambda tbl: tbl[CHUNK:ATT_BLOCK]
    pad = lambda x: jnp.concatenate([jnp.zeros((CHUNK, PAIR_W), BF16), x], axis=0)
    lo, hi = lo_ref[:, 0:PAIR_W], hi_ref[:, 0:PAIR_W]

    def swap(x):
        return jnp.concatenate([x[:, HEAD_DIM:], x[:, :HEAD_DIM]], axis=1)

    ka, va = ka_ref[0], va_ref[0]
    for kv in range(A_KV_HEADS):
        c0, c1 = slice(2 * kv * PAIR_W, (2 * kv + 1) * PAIR_W), slice((2 * kv + 1) * PAIR_W, (2 * kv + 2) * PAIR_W)
        q2 = jnp.concatenate([qa_ref[0, :, c0], qa_ref[0, :, c1]], axis=0)
        first = (lambda x: x * lo) if kv == 0 else (lambda x: swap(x) * lo)
        last = (lambda x: swap(x) * hi) if kv == 0 else (lambda x: x * hi)
        k2 = jnp.concatenate([pad(first(ka)), pad(last(ka))], axis=0)
        v2 = jnp.concatenate([pad(first(va)), pad(last(va))], axis=0)
        two = lambda tbl: jnp.concatenate([second(tbl[0:ATT_BLOCK]), second(tbl[ATT_BLOCK:])], axis=0)
        o = _pair_attend(q2, k2, v2, ia_ref[...], two(ba_ref[kv]), (two(sk_ref[kv, 0]), two(sk_ref[kv, 1])))
        oa_ref[0, :, c0] = o[0:CHUNK].astype(BF16)
        oa_ref[0, :, c1] = o[CHUNK:].astype(BF16)
    for p in range(B_HEADS // 2):
        cols = slice(p * PAIR_W, (p + 1) * PAIR_W)
        kb, vb = kb_ref[0, :, cols], vb_ref[0, :, cols]
        k2 = jnp.concatenate([pad(kb * lo), pad(kb * hi)], axis=0)
        v2 = jnp.concatenate([pad(vb * lo), pad(vb * hi)], axis=0)
        o = _pair_attend(qb_ref[0, :, cols], k2, v2, ib_ref[...], second(bb_ref[p]), None)
        ob_ref[0, :, cols] = o.astype(BF16)


def _attn_sample(qa, ka, va, qb, kb, vb, bias2_a, sinks2, bias2_b):
    b, t, _ = qa.shape
    assert t == CHUNK
    lo, hi, ind_a, ind_b = _pair_constants()
    per_b = lambda r, w: pl.BlockSpec((1, r, w), lambda i: (i, 0, 0))
    const = lambda shp: pl.BlockSpec(shp, lambda i: (0,) * len(shp))
    return pl.pallas_call(
        _attn_sample_kernel,
        out_shape=(jax.ShapeDtypeStruct((b, t, QA_W), BF16), jax.ShapeDtypeStruct((b, t, B_W), BF16)),
        grid=(b,),
        in_specs=[per_b(t, QA_W), per_b(A_BAND, KVA_W), per_b(A_BAND, KVA_W),
                  per_b(t, B_W), per_b(B_BAND, B_W), per_b(B_BAND, B_W),
                  const(bias2_a.shape), const(sinks2.shape), const(bias2_b.shape), const(lo.shape), const(hi.shape),
                  const(ind_a.shape), const(ind_b.shape)],
        out_specs=(per_b(t, QA_W), per_b(t, B_W)),
        compiler_params=_cparams("parallel"),
        name="attn_sample",
    )(qa, ka, va, qb, kb, vb, bias2_a, sinks2, bias2_b, lo, hi, ind_a, ind_b)


def _route(lt):
    t = lt.shape[1]
    el = lt[0:N_EXPERTS]
    gl = lt[N_EXPERTS:N_EXPERTS + N_GROUPS]
    gmax = jnp.max(gl, axis=0, keepdims=True)
    gi = lax.broadcasted_iota(jnp.int32, (N_GROUPS, t), 0)
    gidx = jnp.min(jnp.where(gl == gmax, gi, N_GROUPS), axis=0, keepdims=True)
    g_w = 1.0 / jnp.sum(jnp.exp(gl - gmax), axis=0, keepdims=True)
    e_sel = el[(N_GROUPS - 1) * EXPERTS_PER_GROUP:]
    for g in range(N_GROUPS - 2, -1, -1):
        e_sel = jnp.where(gidx == g, el[g * EXPERTS_PER_GROUP:(g + 1) * EXPERTS_PER_GROUP], e_sel)
    ei = lax.broadcasted_iota(jnp.int32, (EXPERTS_PER_GROUP, t), 0)
    m1 = jnp.max(e_sel, axis=0, keepdims=True)
    i1 = jnp.min(jnp.where(e_sel == m1, ei, EXPERTS_PER_GROUP), axis=0, keepdims=True)
    rest = jnp.where(ei == i1, -jnp.inf, e_sel)
    m2 = jnp.max(rest, axis=0, keepdims=True)
    i2 = jnp.min(jnp.where(rest == m2, ei, EXPERTS_PER_GROUP), axis=0, keepdims=True)
    ex = jnp.exp(m2 - m1)
    den = 1.0 + ex
    w1 = g_w * (1.0 / den)
    w2 = g_w * (ex / den)
    lo = jnp.minimum(i1, i2)
    hi = jnp.maximum(i1, i2)
    first_is_lo = i1 < i2
    w_lo = jnp.where(first_is_lo, w1, w2)
    w_hi = jnp.where(first_is_lo, w2, w1)
    pair = ((lo * (2 * EXPERTS_PER_GROUP - 1 - lo)) >> 1) + (hi - lo - 1)
    return gidx * PAIRS_PER_GROUP + pair, w_lo, w_hi


def _route_logits(wr_ref, br_ref, h2):
    return lax.dot_general(wr_ref[...], h2, (((1,), (1,)), ((), ())), preferred_element_type=F32) + br_ref[...]


def _weight_tile(w_lo, w_hi):
    t = w_lo.shape[1]
    return jnp.transpose(jnp.concatenate([w_lo, w_hi, jnp.zeros((LANES - 2, t), F32)], axis=0))


def _rank_in_class(cls, run_ref, tri_ref):
    t = cls.shape[1]
    onehot = lax.broadcasted_iota(jnp.int32, (CLASS_ROWS, t), 0) == cls
    ones = jnp.where(onehot, 1.0, 0.0)
    before = jnp.dot(ones.astype(BF16), tri_ref[...], preferred_element_type=F32)
    run = run_ref[...]
    rank = jnp.sum(jnp.where(onehot, before + run, 0.0), axis=0, keepdims=True)
    run_ref[...] = run + jnp.sum(ones, axis=1, keepdims=True)
    return rank.astype(jnp.int32)


def _post_kernel(x_ref, oa_ref, ob_ref, mod_ref, g1_ref, g1p_ref, g2_ref,
                 wg_ref, bg_ref, wpa_ref, wpb_ref, wo_ref, wr_ref, br_ref, tri_ref,
                 x1_ref, h2_ref, cls_ref, rank_ref, cnt_ref, run_ref):
    @pl.when((pl.program_id(0) == 0) & (pl.program_id(1) == 0))
    def _():
        run_ref[...] = jnp.zeros_like(run_ref)

    nb, rows, _ = x_ref.shape
    stack = lambda parts: parts[0] if nb == 1 else jnp.concatenate(parts, axis=0)
    xs = [x_ref[g] for g in range(nb)]
    h = stack([_norm_mod(xs[g], g1_ref[...], mod_ref[g, 1:2, :], mod_ref[g, 0:1, :]) for g in range(nb)]).astype(BF16)
    gates = jax.nn.sigmoid(jnp.dot(h, wg_ref[...], preferred_element_type=F32) + bg_ref[...])
    pa = jnp.dot(stack([oa_ref[g] for g in range(nb)]), wpa_ref[...], preferred_element_type=F32)
    pb = jnp.dot(stack([ob_ref[g] for g in range(nb)]), wpb_ref[...], preferred_element_type=F32)
    mixed = gates[:, :D_MODEL] * pa + gates[:, D_MODEL:] * pb
    y = jnp.dot(mixed.astype(BF16), wo_ref[...], preferred_element_type=F32)
    h2_parts = []
    for g in range(nb):
        x1 = xs[g] + mod_ref[g, 2:3, :] * _rms(y[g * rows:(g + 1) * rows], g1p_ref[...])
        x1_ref[g] = x1
        h2_parts.append(_norm_mod(x1, g2_ref[...], mod_ref[g, 4:5, :], mod_ref[g, 3:4, :]))
    h2 = stack(h2_parts).astype(BF16)
    cls, w_lo, w_hi = _route(_route_logits(wr_ref, br_ref, h2))
    cls_ref[0] = cls
    rank_ref[0] = _rank_in_class(cls, run_ref, tri_ref)
    cnt_ref[...] = run_ref[...]
    t = nb * rows
    h2f = h2.astype(F32)
    for c in range(D_MODEL // LANES):
        h2_ref[pl.ds(c, t, stride=IN_SLAB), :] = h2f[:, c * LANES:(c + 1) * LANES]
    h2_ref[pl.ds(D_MODEL // LANES, t, stride=IN_SLAB), :] = _weight_tile(w_lo, w_hi)


def _post(x, oa, ob, mods, g_pre_mix, g_post_mix, g_pre_ffn, wg, bg, wpa, wpb, wo, wr, br, tile):
    b, s, d = x.shape
    rows = min(s, tile)
    nb = math.gcd(tile // rows, b)
    tile = nb * rows
    assert s % rows == 0
    nt = s // rows
    n_steps = (b // nb) * nt
    tri = (jnp.arange(tile)[:, None] < jnp.arange(tile)[None, :]).astype(BF16)
    tok = lambda w: pl.BlockSpec((nb, rows, w), lambda i, j: (i, j, 0))
    const = lambda shp: pl.BlockSpec(shp, lambda i, j: (0,) * len(shp))
    per_tile = pl.BlockSpec((1, 1, tile), lambda i, j: (i * nt + j, 0, 0))
    return pl.pallas_call(
        _post_kernel,
        out_shape=(
            jax.ShapeDtypeStruct((b, s, d), F32),
            jax.ShapeDtypeStruct((b * s * IN_SLAB, LANES), F32),
            jax.ShapeDtypeStruct((n_steps, 1, tile), jnp.int32),
            jax.ShapeDtypeStruct((n_steps, 1, tile), jnp.int32),
            jax.ShapeDtypeStruct((CLASS_ROWS, 1), F32),
        ),
        grid=(b // nb, nt),
        in_specs=[
            tok(d), tok(QA_W), tok(B_W),
            pl.BlockSpec((nb, 6, d), lambda i, j: (i, 0, 0)),
            const((1, d)), const((1, d)), const((1, d)),
            const(wg.shape), const(bg.shape), const(wpa.shape), const(wpb.shape), const(wo.shape),
            const(wr.shape), const(br.shape), const(tri.shape),
        ],
        out_specs=(tok(d), pl.BlockSpec((tile * IN_SLAB, LANES), lambda i, j: (i * nt + j, 0)),
                   per_tile, per_tile, const((CLASS_ROWS, 1))),
        scratch_shapes=[pltpu.VMEM((CLASS_ROWS, 1), F32)],
        compiler_params=_cparams("arbitrary", "arbitrary"),
        name="post",
    )(x, oa, ob, mods, g_pre_mix, g_post_mix, g_pre_ffn, wg, bg, wpa, wpb, wo, wr, br, tri)


def _invert_kernel(upos_ref, row_ref, *, n, scale):
    def body(i, carry):
        for u in range(INVERT_UNROLL):
            t = i * INVERT_UNROLL + u
            row_ref[upos_ref[t]] = t * scale
        return carry
    lax.fori_loop(0, n // INVERT_UNROLL, body, 0)


def _invert(upos, scale):
    n = upos.shape[0]
    return pl.pallas_call(
        functools.partial(_invert_kernel, n=n, scale=scale),
        out_shape=jax.ShapeDtypeStruct((n,), jnp.int32),
        in_specs=[pl.BlockSpec(memory_space=pltpu.SMEM)],
        out_specs=pl.BlockSpec(memory_space=pltpu.SMEM),
        name="invert",
    )(upos)


def _expert(xb, wgu_ref, wd_ref):
    gu = jnp.dot(xb, wgu_ref[0], preferred_element_type=F32)
    gate = gu[:, :D_EXPERT]
    he = (gate * jax.nn.sigmoid(gate)) * gu[:, D_EXPERT:]
    return jnp.dot(he.astype(BF16), wd_ref[0], preferred_element_type=F32)


def _moe_kernel(tok_ref, base_ref, ea_ref, eb_ref, nu_ref, h_ref, wgu_a_ref, wd_a_ref, wgu_b_ref, wd_b_ref,
                ys_ref, xbuf, sems, *, tile, n_tok):
    t = pl.program_id(0)
    nu = nu_ref[0]
    slot = t % MOE_BUFFERS

    def slab_copy(src_row0, dst_slot, r):
        return pltpu.make_async_copy(h_ref.at[pl.ds(src_row0, IN_SLAB)],
                                     xbuf.at[dst_slot, pl.ds(r * IN_SLAB, IN_SLAB)], sems.at[dst_slot])

    def fetch(step, dst_slot):
        base = base_ref[step]
        for r in range(tile):
            slab_copy(tok_ref[jnp.minimum(base + r, n_tok - 1)], dst_slot, r).start(priority=r % 2)

    def wait(dst_slot):
        for r in range(tile):
            slab_copy(0, dst_slot, 0).wait()

    ahead = MOE_BUFFERS - 1

    @pl.when(t == 0)
    def _():
        for k in range(ahead):
            fetch(jnp.minimum(k, nu - 1), k)

    @pl.when(t < nu)
    def _():
        wait(slot)
        fetch(jnp.minimum(t + ahead, nu - 1), (t + ahead) % MOE_BUFFERS)
        lane_tile = lambda c: xbuf[slot, pl.ds(c, tile, stride=IN_SLAB), :]
        xb = jnp.concatenate([lane_tile(c) for c in range(D_MODEL // LANES)], axis=1).astype(BF16)
        w = lane_tile(D_MODEL // LANES)
        y = w[:, 0:1] * _expert(xb, wgu_a_ref, wd_a_ref) + w[:, 1:2] * _expert(xb, wgu_b_ref, wd_b_ref)
        for c in range(OUT_SLAB):
            ys_ref[pl.ds(c, tile, stride=OUT_SLAB), :] = y[:, c * LANES:(c + 1) * LANES]

    @pl.when(t == nu - 1)
    def _():
        for k in range(1, MOE_BUFFERS):
            wait((t + k) % MOE_BUFFERS)

    @pl.when(t >= nu)
    def _():
        ys_ref[...] = jnp.zeros_like(ys_ref)


def _moe_grouped(h_slabs, sorted_tok, tile_base, tile_ea, tile_eb, n_used, wgu, wd, tile, n_tiles):
    d = D_MODEL
    n_tok = sorted_tok.shape[0]
    expert = lambda which, shp: pl.BlockSpec(shp, lambda t, tok, base, ea, eb, nu: ((ea, eb)[which][t], 0, 0))
    return pl.pallas_call(
        functools.partial(_moe_kernel, tile=tile, n_tok=n_tok),
        out_shape=jax.ShapeDtypeStruct((n_tiles * tile * OUT_SLAB, LANES), F32),
        grid_spec=pltpu.PrefetchScalarGridSpec(
            num_scalar_prefetch=5,
            grid=(n_tiles,),
            in_specs=[
                pl.BlockSpec(memory_space=pl.ANY),
                expert(0, (1, d, 2 * D_EXPERT)), expert(0, (1, D_EXPERT, d)),
                expert(1, (1, d, 2 * D_EXPERT)), expert(1, (1, D_EXPERT, d)),
            ],
            out_specs=pl.BlockSpec((tile * OUT_SLAB, LANES), lambda t, *_: (t, 0)),
            scratch_shapes=[pltpu.VMEM((MOE_BUFFERS, tile * IN_SLAB, LANES), F32),
                            pltpu.SemaphoreType.DMA((MOE_BUFFERS,))],
        ),
        compiler_params=_cparams("arbitrary"),
        name="moe_grouped",
    )(sorted_tok, tile_base, tile_ea, tile_eb, n_used, h_slabs, wgu, wd, wgu, wd)


_PAIR_LO, _PAIR_HI = np.triu_indices(EXPERTS_PER_GROUP, k=1)
_CLASS_LO = np.concatenate([g * EXPERTS_PER_GROUP + _PAIR_LO for g in range(N_GROUPS)]).astype(np.int32)
_CLASS_HI = np.concatenate([g * EXPERTS_PER_GROUP + _PAIR_HI for g in range(N_GROUPS)]).astype(np.int32)


def _moe_plan(counts, cls, rank, n, tile):
    p_max = -(-(n + N_CLASSES * (tile - 1)) // tile) * tile
    nt = p_max // tile
    counts = counts[:N_CLASSES].astype(jnp.int32)
    padded = ((counts + tile - 1) // tile) * tile
    pad_end = jnp.cumsum(padded)
    gstart = pad_end - padded
    ustart = jnp.cumsum(counts) - counts
    onehot = cls[:, None] == jnp.arange(N_CLASSES, dtype=jnp.int32)[None, :]
    pos = rank + jnp.sum(jnp.where(onehot, gstart[None, :], 0), axis=1)
    upos = rank + jnp.sum(jnp.where(onehot, ustart[None, :], 0), axis=1)
    n_used = (pad_end[-1] // tile).astype(jnp.int32).reshape(1)
    tiles = jnp.arange(nt, dtype=jnp.int32)
    tile_cls = jnp.minimum(jnp.sum((pad_end[None, :] <= (tiles * tile)[:, None]).astype(jnp.int32), axis=1), N_CLASSES - 1)
    tile_ea = jnp.asarray(_CLASS_LO)[tile_cls]
    tile_eb = jnp.asarray(_CLASS_HI)[tile_cls]
    tile_base = jnp.clip(ustart[tile_cls] + tiles * tile - gstart[tile_cls], 0, n - 1)
    return nt, pos, upos, tile_base, tile_ea, tile_eb, n_used


def _final_kernel(pos_ref, x1_ref, ys_ref, mod_ref, g_ref, o_ref, ybuf, sems, *, tile, n_steps):
    s = pl.program_id(0)
    slot = s % 2

    def slab_copy(src_row0, dst_slot, dst_row):
        return pltpu.make_async_copy(ys_ref.at[pl.ds(src_row0, OUT_SLAB)],
                                     ybuf.at[dst_slot, pl.ds(dst_row * OUT_SLAB, OUT_SLAB)], sems.at[dst_slot])

    def fetch(step, dst_slot):
        for r in range(tile):
            slab_copy(pos_ref[step * tile + r], dst_slot, r).start(priority=r % 2)

    def wait(dst_slot):
        for r in range(tile):
            slab_copy(0, dst_slot, 0).wait()

    @pl.when(s == 0)
    def _():
        fetch(0, 0)

    wait(slot)
    fetch(jnp.minimum(s + 1, n_steps - 1), 1 - slot)
    y = jnp.concatenate([ybuf[slot, pl.ds(c, tile, stride=OUT_SLAB), :] for c in range(OUT_SLAB)], axis=1)
    o_ref[...] = x1_ref[...] + mod_ref[0, 5:6, :] * _rms(y, g_ref[...])

    @pl.when(s == n_steps - 1)
    def _():
        wait(1 - slot)


def _final(x1, ys, pos, mods, g_post_ffn, tile):
    b, s, d = x1.shape
    n = b * s
    per_b = s // tile
    n_steps = n // tile
    tok = pl.BlockSpec((tile, d), lambda i, *_: (i, 0))
    out = pl.pallas_call(
        functools.partial(_final_kernel, tile=tile, n_steps=n_steps),
        out_shape=jax.ShapeDtypeStruct((n, d), F32),
        grid_spec=pltpu.PrefetchScalarGridSpec(
            num_scalar_prefetch=1,
            grid=(n_steps,),
            in_specs=[
                tok,
                pl.BlockSpec(memory_space=pl.ANY),
                pl.BlockSpec((1, 6, d), lambda i, *_: (i // per_b, 0, 0)),
                pl.BlockSpec((1, d), lambda i, *_: (0, 0)),
            ],
            out_specs=tok,
            scratch_shapes=[pltpu.VMEM((2, tile * OUT_SLAB, LANES), F32), pltpu.SemaphoreType.DMA((2,))],
        ),
        compiler_params=_cparams("arbitrary"),
        name="final",
    )(pos, x1.reshape(n, d), ys, mods, g_post_ffn)
    return out.reshape(b, s, d)


def _t5_bucket(rel):
    half = T5_BUCKETS // 2
    exact = half // 2
    ret = jnp.where(rel > 0, half, 0)
    n = jnp.abs(rel)
    nf = jnp.maximum(n, 1).astype(F32)
    large = exact + (jnp.log(nf / exact) / math.log(T5_MAX_DIST / exact) * (half - exact)).astype(jnp.int32)
    large = jnp.minimum(large, half - 1)
    return ret + jnp.where(n < exact, n, large)


def _toeplitz(u, n_rows, n_cols):
    lead, p = u.shape[:-1], n_rows + n_cols
    w = jnp.concatenate([u, jnp.zeros(lead + (1,), u.dtype)], axis=-1)
    periodic = jnp.broadcast_to(w[..., None, :], lead + (n_rows + 1, p)).reshape(lead + ((n_rows + 1) * p,))
    shifted = periodic[..., :n_rows * (p + 1)].reshape(lead + (n_rows, p + 1))[..., :n_cols]
    return shifted[..., ::-1, :]


def _bias_tables(t5_table, rel_table):
    ja = jnp.arange(CHUNK - 1 + A_BAND)
    ua = t5_table[_t5_bucket(ja - (CHUNK - 1) - A_WINDOW)].T.astype(F32)
    jb = jnp.arange(CHUNK - 1 + B_BAND)
    ub = rel_table[:, jnp.clip((CHUNK - 1) - jb + B_REACH, -REL_CLIP, REL_CLIP) + REL_CLIP].astype(F32)
    return _toeplitz(ua, CHUNK, A_BAND), _toeplitz(ub, CHUNK, B_BAND)


def _two_chunk(bias):
    masked = jnp.full(bias.shape[:-1] + (CHUNK,), MASKED, F32)
    return jnp.concatenate([jnp.concatenate([bias, masked], axis=-1), jnp.concatenate([masked, bias], axis=-1)], axis=-2)


def _pair_tables(bias_a, bias_b, sinks):
    a2, b2 = _two_chunk(bias_a), _two_chunk(bias_b)
    stack2 = lambda x, h0, h1: jnp.concatenate([x[h0], x[h1]], axis=0)
    bias2_a = jnp.stack([jnp.concatenate([stack2(a2, 4 * kv, 4 * kv + 2), stack2(a2, 4 * kv + 1, 4 * kv + 3)], axis=1)
                         for kv in range(A_KV_HEADS)])
    bias2_b = jnp.concatenate([b2[0::2], b2[1::2]], axis=-1)
    sk = jnp.broadcast_to(sinks.astype(F32).reshape(A_Q_HEADS, 1, 1), (A_Q_HEADS, ATT_BLOCK, 1))
    sinks2 = jnp.stack([jnp.stack([stack2(sk, 4 * kv, 4 * kv + 2), stack2(sk, 4 * kv + 1, 4 * kv + 3)])
                        for kv in range(A_KV_HEADS)])
    return bias2_a, sinks2, bias2_b


def _moe_and_final(x1, h2e, cls, rank, counts, mods, g_post_ffn, wgu, wd, seq_tile, moe_tile):
    b, s, d = x1.shape
    n = b * s
    n_tiles, pos, upos, tile_base, tile_ea, tile_eb, n_used = _moe_plan(
        counts.reshape(CLASS_ROWS), cls.reshape(n), rank.reshape(n), n, moe_tile)
    ys = _moe_grouped(h2e, _invert(upos, IN_SLAB), tile_base, tile_ea, tile_eb, n_used, wgu, wd, moe_tile, n_tiles)
    return _final(x1, ys, pos * OUT_SLAB, mods, g_post_ffn, seq_tile)


def kernel(x_prompt, x_sample, c_prompt, c_sample, cache_a_k, cache_a_v, cache_b_k, cache_b_v, w_ada, b_ada, g_pre_mix, g_post_mix, g_pre_ffn, g_post_ffn, w_in, a_sinks, t5_table, b_rel_table, w_proj_a, w_proj_b, w_gate, b_gate, w_o, w_route_g, b_route_g, w_route_e, b_route_e, w_e_gate, w_e_up, w_e_down):
    depth = w_in.shape[0]
    assert depth == 1
    l = 0
    bp, sp, d = x_prompt.shape
    bs, ts, _ = x_sample.shape

    mods = _ada(jnp.concatenate([c_prompt, c_sample], axis=0), w_ada[l], b_ada[l]).reshape(bp + bs, 6, d)
    mods_p, mods_s = mods[:bp], mods[bp:]

    w_in_bf = w_in[l].astype(BF16)
    wg, wpa, wpb, wo = w_gate[l].astype(BF16), w_proj_a[l].astype(BF16), w_proj_b[l].astype(BF16), w_o[l].astype(BF16)
    bg = b_gate[l].reshape(1, 2 * d)
    pad_rows = ROUTE_ROWS - N_EXPERTS - N_GROUPS
    wr = jnp.concatenate([w_route_e[l].T, w_route_g[l].T, jnp.zeros((pad_rows, d), F32)], axis=0).astype(BF16)
    br = jnp.concatenate([b_route_e[l], b_route_g[l], jnp.zeros((pad_rows,), F32)]).reshape(ROUTE_ROWS, 1)
    wgu = jnp.concatenate([w_e_gate[l], w_e_up[l]], axis=-1).astype(BF16)
    wd = w_e_down[l].astype(BF16)
    g1, g1p, g2, g2p = (g[l].reshape(1, d) for g in (g_pre_mix, g_post_mix, g_pre_ffn, g_post_ffn))

    bias_a, bias_b = _bias_tables(t5_table, b_rel_table[l])
    bias2_a, sinks2, bias2_b = _pair_tables(bias_a, bias_b, a_sinks[l])

    qa, ka, va, qb, kb, vb, sak_p, sav_p, sbk_p, sbv_p = _pre_prompt(x_prompt, mods_p, g1, w_in_bf)
    oa, ob = _attn_prompt(qa, ka, va, qb, kb, vb, bias2_a, sinks2, bias2_b)
    x1, h2e, cls, rank, counts = _post(x_prompt, oa, ob, mods_p, g1, g1p, g2, wg, bg, wpa, wpb, wo, wr, br, SEQ_TILE)
    y_prompt = _moe_and_final(x1, h2e, cls, rank, counts, mods_p, g2p, wgu, wd, SEQ_TILE, MOE_TILE_PROMPT)

    la, lb = cache_a_k.shape[2], cache_b_k.shape[2]
    cak = cache_a_k[l].reshape(bs, la, KVA_W)
    cav = cache_a_v[l].reshape(bs, la, KVA_W)
    cbk = cache_b_k[l].reshape(bs, lb, B_W)
    cbv = cache_b_v[l].reshape(bs, lb, B_W)
    qa, ka, va, qb, kb, vb, sak_s, sav_s, sbk_s, sbv_s = _pre_sample(x_sample, mods_s, g1, w_in_bf, cak, cav, cbk, cbv)
    oa, ob = _attn_sample(qa, ka, va, qb, kb, vb, bias2_a, sinks2, bias2_b)
    x1, h2e, cls, rank, counts = _post(x_sample, oa, ob, mods_s, g1, g1p, g2, wg, bg, wpa, wpb, wo, wr, br, SEQ_TILE)
    y_sample = _moe_and_final(x1, h2e, cls, rank, counts, mods_s, g2p, wgu, wd, ts, MOE_TILE_SAMPLE)

    a_state = lambda v, b, r: v.reshape(1, b, r, A_KV_HEADS, HEAD_DIM)
    b_state = lambda v, b, r: v.reshape(1, b, r, B_HEADS, HEAD_DIM)
    return (y_prompt, y_sample,
            a_state(sak_p, bp, A_WINDOW), a_state(sav_p, bp, A_WINDOW),
            b_state(sbk_p, bp, B_REACH), b_state(sbv_p, bp, B_REACH),
            a_state(sak_s, bs, la), a_state(sav_s, bs, la),
            b_state(sbk_s, bs, lb), b_state(sbv_s, bs, lb))
```
